```python
import math
import jax, jax.numpy as jnp
from jax import lax
import numpy as np

D_MODEL = 1024
BATCH = 8
SEQ = 2048
DEPTH = 2

CHUNK = 128
A_GROUPS = 4
A_GROUP_DIM = 128
A_WIDTH = A_GROUPS * A_GROUP_DIM
B_HEADS = 4
B_HEAD_DIM = 64
B_VDIM = 2 * B_HEAD_DIM
B_WIDTH = B_HEADS * B_VDIM
Q_BLOCK = 128
C_WINDOWS = (2, 4, 8, 16)
C_GROUPS = len(C_WINDOWS)
C_GROUP_DIM = 128
C_WIDTH = C_GROUPS * C_GROUP_DIM
N_BRANCH = 3
BRANCH_WIDTH = 512
IN_A = 2 * A_WIDTH
IN_Q = B_HEADS * 2 * B_HEAD_DIM
IN_K = B_HEADS * 2 * B_HEAD_DIM
IN_V = B_WIDTH
IN_C = C_WIDTH
IN_G = N_BRANCH * D_MODEL
IN_TOTAL = IN_A + IN_Q + IN_K + IN_V + IN_C + IN_G
SPLITS = tuple(int(s) for s in np.cumsum([IN_A, IN_Q, IN_K, IN_V, IN_C]))
D_FF = -(-8 * D_MODEL // (3 * 256)) * 256
EPS = 1e-6

kernel_name = "hybrid_gated_gmlp_diffattn_pool_block"


def rmsnorm(x, g):
    xf = x.astype(jnp.float32)
    y = xf * lax.rsqrt(jnp.mean(xf * xf, axis=-1, keepdims=True) + EPS)
    return (y * g.astype(jnp.float32)).astype(x.dtype)


def layernorm(x, g, b):
    xf = x.astype(jnp.float32)
    mu = jnp.mean(xf, axis=-1, keepdims=True)
    var = jnp.mean(jnp.square(xf - mu), axis=-1, keepdims=True)
    y = (xf - mu) * lax.rsqrt(var + EPS)
    return (y * g.astype(jnp.float32) + b.astype(jnp.float32)).astype(x.dtype)


def lambda_init_fn(layer_idx):
    return 0.8 - 0.6 * math.exp(-0.3 * layer_idx)


def gmlp_branch(za, vn_g, vn_b, w_s, b_s):
    bsz, s, _ = za.shape
    u, v = jnp.split(za, 2, axis=-1)
    v = layernorm(v, vn_g, vn_b)
    vc = v.reshape(bsz, s // CHUNK, CHUNK, A_GROUPS, A_GROUP_DIM)
    causal = jnp.tril(jnp.ones((CHUNK, CHUNK), dtype=bool))
    ws = jnp.where(causal[None], w_s, jnp.zeros_like(w_s))
    mixed = jnp.einsum('gts,bcsgd->bctgd', ws, vc) + b_s.T[None, None, :, :, None]
    return u * mixed.reshape(bsz, s, A_WIDTH)


def diff_attention(q, k, v, lq1, lk1, lq2, lk2, subln_g, lambda_init):
    bsz, s, _ = q.shape
    q = q.reshape(bsz, s, B_HEADS, 2, B_HEAD_DIM)
    k = k.reshape(bsz, s, B_HEADS, 2, B_HEAD_DIM)
    v = v.reshape(bsz, s, B_HEADS, B_VDIM)
    f32 = jnp.float32
    lam = (jnp.exp(jnp.sum(lq1.astype(f32) * lk1.astype(f32)))
           - jnp.exp(jnp.sum(lq2.astype(f32) * lk2.astype(f32))) + lambda_init)
    scale = B_HEAD_DIM ** -0.5
    nb = s // Q_BLOCK
    qb = jnp.moveaxis(q.reshape(bsz, nb, Q_BLOCK, B_HEADS, 2, B_HEAD_DIM), 1, 0)
    kpos = jnp.arange(s)

    def block(args):
        qblk, i = args
        sc = jnp.einsum('bqhcd,bkhcd->bhcqk', qblk, k).astype(f32) * scale
        qpos = i * Q_BLOCK + jnp.arange(Q_BLOCK)
        mask = kpos[None, :] <= qpos[:, None]
        sc = jnp.where(mask, sc, -jnp.inf)
        p = jax.nn.softmax(sc, axis=-1)
        w = p[:, :, 0] - lam * p[:, :, 1]
        return jnp.einsum('bhqk,bkhe->bqhe', w.astype(v.dtype), v)

    out = lax.map(block, (qb, jnp.arange(nb)))
    out = jnp.moveaxis(out, 0, 1).reshape(bsz, s, B_HEADS, B_VDIM)
    out = rmsnorm(out, subln_g) * (1.0 - lambda_init)
    return out.reshape(bsz, s, B_WIDTH)


def pool_branch(p, w_pool, pool_scale):
    bsz, s, _ = p.shape
    pf = p.astype(jnp.float32)
    csum = jnp.concatenate([jnp.zeros((bsz, 1, C_WIDTH), jnp.float32),
                            lax.cumsum(pf, axis=1)], axis=1)
    t = jnp.arange(s)
    outs = []
    for g, w in enumerate(C_WINDOWS):
        sl = slice(g * C_GROUP_DIM, (g + 1) * C_GROUP_DIM)
        c = csum[..., sl]
        lo = jnp.take(c, jnp.maximum(t + 1 - w, 0), axis=1)
        cnt = jnp.minimum(t + 1, w).astype(jnp.float32)
        outs.append((c[:, 1:] - lo) / cnt[None, :, None] - pf[..., sl])
    pooled = jnp.stack(outs, axis=2).astype(p.dtype)
    y = jnp.einsum('bsgc,gcd->bsgd', pooled, w_pool).reshape(bsz, s, C_WIDTH)
    return y * pool_scale


def setup_inputs(seed: int = 0) -> dict:
    key = jax.random.key(seed)
    ks = jax.random.split(key, 24)
    f32 = jnp.float32
    nrm = lambda k, shape, scale: jax.random.normal(k, shape, f32) * scale
    gain = lambda k, shape: 1.0 + 0.05 * jax.random.normal(k, shape, f32)
    L = DEPTH
    return {
        "x": jax.random.normal(ks[0], (BATCH, SEQ, D_MODEL), f32),
        "norm_mix_pre": gain(ks[1], (L, D_MODEL)),
        "w_in": nrm(ks[2], (L, D_MODEL, IN_TOTAL), D_MODEL ** -0.5),
        "gmlp_norm_g": gain(ks[3], (L, A_WIDTH)),
        "gmlp_norm_b": nrm(ks[4], (L, A_WIDTH), 0.02),
        "gmlp_w_s": nrm(ks[5], (L, A_GROUPS, CHUNK, CHUNK), CHUNK ** -0.5),
        "gmlp_b_s": gain(ks[6], (L, A_GROUPS, CHUNK)),
        "lambda_q1": nrm(ks[7], (L, B_HEAD_DIM), 0.1),
        "lambda_k1": nrm(ks[8], (L, B_HEAD_DIM), 0.1),
        "lambda_q2": nrm(ks[9], (L, B_HEAD_DIM), 0.1),
        "lambda_k2": nrm(ks[10], (L, B_HEAD_DIM), 0.1),
        "diff_subln_g": gain(ks[11], (L, B_VDIM)),
        "pool_w": nrm(ks[12], (L, C_GROUPS, C_GROUP_DIM, C_GROUP_DIM), C_GROUP_DIM ** -0.5),
        "pool_scale": gain(ks[13], (L, C_WIDTH)),
        "w_branch": nrm(ks[14], (L, N_BRANCH, BRANCH_WIDTH, D_MODEL), BRANCH_WIDTH ** -0.5),
        "w_out": nrm(ks[15], (L, D_MODEL, D_MODEL), D_MODEL ** -0.5),
        "norm_mix_post": gain(ks[16], (L, D_MODEL)),
        "norm_ffn_pre": gain(ks[17], (L, D_MODEL)),
        "w_ffn_in": nrm(ks[18], (L, D_MODEL, 2 * D_FF), D_MODEL ** -0.5),
        "w_ffn_out": nrm(ks[19], (L, D_FF, D_MODEL), D_FF ** -0.5),
        "norm_ffn_post": gain(ks[20], (L, D_MODEL)),
    }


def reference(x, norm_mix_pre, w_in, gmlp_norm_g, gmlp_norm_b, gmlp_w_s, gmlp_b_s,
              lambda_q1, lambda_k1, lambda_q2, lambda_k2, diff_subln_g, pool_w,
              pool_scale, w_branch, w_out, norm_mix_post, norm_ffn_pre, w_ffn_in,
              w_ffn_out, norm_ffn_post):
    bsz, s, _ = x.shape
    for l in range(DEPTH):
        h = rmsnorm(x, norm_mix_pre[l])
        z = h @ w_in[l]
        za, zq, zk, zv, zc, zg = jnp.split(z, SPLITS, axis=-1)
        ya = gmlp_branch(jax.nn.gelu(za, approximate=False), gmlp_norm_g[l],
                         gmlp_norm_b[l], gmlp_w_s[l], gmlp_b_s[l])
        yb = diff_attention(zq, zk, zv, lambda_q1[l], lambda_k1[l], lambda_q2[l],
                            lambda_k2[l], diff_subln_g[l], lambda_init_fn(l))
        yc = pool_branch(zc, pool_w[l], pool_scale[l])
        ys = jnp.stack([ya, yb, yc], axis=2)
        up = jnp.einsum('bsnw,nwd->bsnd', ys, w_branch[l])
        gates = jax.nn.sigmoid(zg).reshape(bsz, s, N_BRANCH, D_MODEL)
        merged = jnp.sum(gates * up, axis=2)
        x = x + rmsnorm(merged @ w_out[l], norm_mix_post[l])
        h = rmsnorm(x, norm_ffn_pre[l])
        g, u = jnp.split(h @ w_ffn_in[l], 2, axis=-1)
        f = (jax.nn.silu(g) * u) @ w_ffn_out[l]
        x = x + rmsnorm(f, norm_ffn_post[l])
    return x
```

```python
import functools
import math

import jax
import jax.numpy as jnp
from jax import lax
from jax.experimental import pallas as pl
from jax.experimental.pallas import tpu as pltpu

F32 = jnp.float32
BF16 = jnp.bfloat16

D_MODEL = 1024
SEQ = 2048
CHUNK = 128
A_GROUPS = 4
A_WIDTH = 512
B_HEADS = 4
B_HEAD_DIM = 64
B_VDIM = 128
B_WIDTH = 512
C_WINDOWS = (2, 4, 8, 16)
C_GROUP_DIM = 128
C_WIDTH = 512
N_BRANCH = 3
BRANCH_WIDTH = 512
D_FF = 2816
EPS = 1e-6

COL_A = 0
COL_Q = 2 * A_WIDTH
COL_K = COL_Q + 512
COL_V = COL_K + 512
COL_C = COL_V + B_WIDTH
COL_G = COL_C + C_WIDTH
IN_TOTAL = COL_G + N_BRANCH * D_MODEL

HALO = 16
LANES = 128
VMEM_LIMIT = 56 * 1024 * 1024

TM_PROJ = 256
TM_MERGE = 256
TQ = 256
FF_CHUNK = 512


def _rms(x, g):
    ms = jnp.mean(x * x, axis=-1, keepdims=True)
    return x * lax.rsqrt(ms + EPS) * g


def _const_spec(shape):
    nd = len(shape)
    return pl.BlockSpec(shape, lambda *_: (0,) * nd, pipeline_mode=pl.Buffered(1))


def _in_proj_kernel(x_ref, gpre_ref, w_ref, lng_ref, lnb_ref, ws_ref, bs_ref, pw_ref, psc_ref,
                    ya_ref, q_ref, k_ref, v_ref, yc_ref, gate_ref, cbuf):
    tm = x_ref.shape[0]
    pos = (pl.program_id(0) * tm) % SEQ
    h = _rms(x_ref[...], gpre_ref[...]).astype(BF16)

    def proj(lo, width):
        return jnp.dot(h, w_ref[:, lo:lo + width], preferred_element_type=F32)

    za = proj(COL_A, 2 * A_WIDTH)
    ga = 0.5 * za * (1.0 + lax.erf(za * math.sqrt(0.5)))
    u = ga[:, :A_WIDTH]
    vv = ga[:, A_WIDTH:]
    mu = jnp.mean(vv, axis=-1, keepdims=True)
    dv = vv - mu
    var = jnp.mean(dv * dv, axis=-1, keepdims=True)
    vn = (dv * lax.rsqrt(var + EPS) * lng_ref[...] + lnb_ref[...]).astype(BF16)
    trow = lax.broadcasted_iota(jnp.int32, (CHUNK, CHUNK), 0)
    tcol = lax.broadcasted_iota(jnp.int32, (CHUNK, CHUNK), 1)
    for g in range(A_GROUPS):
        wsg = jnp.where(tcol <= trow, ws_ref[g], 0.0).astype(BF16)
        cs = slice(g * CHUNK, (g + 1) * CHUNK)
        for c in range(tm // CHUNK):
            rs = slice(c * CHUNK, (c + 1) * CHUNK)
            mixed = jnp.dot(wsg, vn[rs, cs], preferred_element_type=F32) + bs_ref[g]
            ya_ref[rs, cs] = (u[rs, cs] * mixed).astype(BF16)

    q_ref[...] = (proj(COL_Q, 512) * (B_HEAD_DIM ** -0.5)).astype(BF16)
    k_ref[...] = proj(COL_K, 512).astype(BF16)
    v_ref[...] = proj(COL_V, B_WIDTH).astype(BF16)

    zc = proj(COL_C, C_WIDTH)

    @pl.when(pos == 0)
    def _():
        cbuf[0:HALO, :] = jnp.zeros((HALO, C_WIDTH), F32)

    cbuf[HALO:, :] = zc
    p_all = cbuf[...]
    cbuf[0:HALO, :] = p_all[tm:, :]
    s2 = p_all + pltpu.roll(p_all, 1, 0)
    s4 = s2[:, 128:] + pltpu.roll(s2[:, 128:], 2, 0)
    s8 = s4[:, 128:] + pltpu.roll(s4[:, 128:], 4, 0)
    s16 = s8[:, 128:] + pltpu.roll(s8[:, 128:], 8, 0)
    sums = (s2[HALO:, :128], s4[HALO:, :128], s8[HALO:, :128], s16[HALO:, :])
    tpos = pos + 1 + lax.broadcasted_iota(jnp.int32, (tm, C_GROUP_DIM), 0)
    for g, w in enumerate(C_WINDOWS):
        cs = slice(g * C_GROUP_DIM, (g + 1) * C_GROUP_DIM)
        inv_cnt = 1.0 / jnp.minimum(tpos, w).astype(F32)
        pooled = (sums[g] * inv_cnt - zc[:, cs]).astype(BF16)
        yc = jnp.dot(pooled, pw_ref[g], preferred_element_type=F32) * psc_ref[:, cs]
        yc_ref[:, cs] = yc.astype(BF16)

    for j in range(N_BRANCH):
        zg = proj(COL_G + j * D_MODEL, D_MODEL)
        gate_ref[:, j * D_MODEL:(j + 1) * D_MODEL] = jax.nn.sigmoid(zg)


def _in_proj(x, gpre, w_in, lng, lnb, ws, bs, pw, psc, layer):
    n = x.shape[0]
    tm = TM_PROJ
    row = lambda width: pl.BlockSpec((tm, width), lambda i: (i, 0))
    out_shapes = (
        jax.ShapeDtypeStruct((n, A_WIDTH), BF16),
        jax.ShapeDtypeStruct((n, 512), BF16),
        jax.ShapeDtypeStruct((n, 512), BF16),
        jax.ShapeDtypeStruct((n, B_WIDTH), BF16),
        jax.ShapeDtypeStruct((n, C_WIDTH), BF16),
        jax.ShapeDtypeStruct((n, N_BRANCH * D_MODEL), F32),
    )
    return pl.pallas_call(
        _in_proj_kernel,
        grid=(n // tm,),
        in_specs=[
            row(D_MODEL),
            _const_spec((1, D_MODEL)),
            _const_spec((D_MODEL, IN_TOTAL)),
            _const_spec((1, A_WIDTH)),
            _const_spec((1, A_WIDTH)),
            _const_spec((A_GROUPS, CHUNK, CHUNK)),
            _const_spec((A_GROUPS, CHUNK, CHUNK)),
            _const_spec((len(C_WINDOWS), C_GROUP_DIM, C_GROUP_DIM)),
            _const_spec((1, C_WIDTH)),
        ],
        out_specs=[row(A_WIDTH), row(512), row(512), row(B_WIDTH), row(C_WIDTH),
                   row(N_BRANCH * D_MODEL)],
        out_shape=out_shapes,
        scratch_shapes=[pltpu.VMEM((HALO + tm, C_WIDTH), F32)],
        compiler_params=pltpu.CompilerParams(
            dimension_semantics=("arbitrary",), vmem_limit_bytes=VMEM_LIMIT),
        name=f"in_proj_l{layer}",
    )(x, gpre, w_in, lng, lnb, ws, bs, pw, psc)


def _attn_kernel(q_ref, k_ref, v_ref, lq1_ref, lk1_ref, lq2_ref, lk2_ref, sg_ref, o_ref,
                 m_scr, l_scr, acc_scr, *, lambda_init):
    i = pl.program_id(2)
    tq = q_ref.shape[0]
    q = q_ref[...]
    lane = lax.broadcasted_iota(jnp.int32, q.shape, 1)
    zero = jnp.zeros_like(q)
    qs = (jnp.where(lane < B_HEAD_DIM, q, zero), jnp.where(lane >= B_HEAD_DIM, q, zero))
    m_scr[...] = jnp.full(m_scr.shape, -jnp.inf, F32)
    l_scr[...] = jnp.zeros(l_scr.shape, F32)
    acc_scr[...] = jnp.zeros(acc_scr.shape, F32)
    row = lax.broadcasted_iota(jnp.int32, (tq, tq), 0)
    col = lax.broadcasted_iota(jnp.int32, (tq, tq), 1)

    def step(j, masked):
        start = pl.multiple_of(j * tq, tq)
        kb = k_ref[pl.ds(start, tq), :]
        vb = v_ref[pl.ds(start, tq), :]
        for c in range(2):
            s = lax.dot_general(qs[c], kb, (((1,), (1,)), ((), ())), preferred_element_type=F32)
            if masked:
                s = jnp.where(col <= row, s, -jnp.inf)
            m_prev = m_scr[c]
            m_new = jnp.maximum(m_prev, jnp.max(s, axis=1, keepdims=True))
            alpha = jnp.exp(m_prev - m_new)
            p = jnp.exp(s - pltpu.repeat(m_new, tq // LANES, 1))
            l_scr[c] = alpha * l_scr[c] + jnp.sum(p, axis=1, keepdims=True)
            acc_scr[c] = alpha * acc_scr[c] + jnp.dot(p.astype(BF16), vb, preferred_element_type=F32)
            m_scr[c] = m_new

    def body(j, carry):
        step(j, False)
        return carry

    lax.fori_loop(0, i, body, 0)
    step(i, True)

    lam = (jnp.exp(jnp.sum(lq1_ref[...] * lk1_ref[...], keepdims=True))
           - jnp.exp(jnp.sum(lq2_ref[...] * lk2_ref[...], keepdims=True)) + lambda_init)
    o = acc_scr[0] / l_scr[0] - lam * (acc_scr[1] / l_scr[1])
    y = _rms(o, sg_ref[...]) * (1.0 - lambda_init)
    o_ref[...] = y.astype(BF16)


def _attention(q, k, v, lq1, lk1, lq2, lk2, sg, lambda_init, layer):
    n = q.shape[0]
    bsz = n // SEQ
    nq = SEQ // TQ
    kv_spec = pl.BlockSpec((SEQ, B_VDIM), lambda b, h, i: (b, h))
    vec = lambda width: pl.BlockSpec((1, width), lambda b, h, i: (0, 0))
    return pl.pallas_call(
        functools.partial(_attn_kernel, lambda_init=lambda_init),
        grid=(bsz, B_HEADS, nq),
        in_specs=[
            pl.BlockSpec((TQ, B_VDIM), lambda b, h, i: (b * nq + i, h)),
            kv_spec, kv_spec,
            vec(B_HEAD_DIM), vec(B_HEAD_DIM), vec(B_HEAD_DIM), vec(B_HEAD_DIM), vec(B_VDIM),
        ],
        out_specs=pl.BlockSpec((TQ, B_VDIM), lambda b, h, i: (b * nq + i, h)),
        out_shape=jax.ShapeDtypeStruct((n, B_WIDTH), BF16),
        scratch_shapes=[pltpu.VMEM((2, TQ, LANES), F32), pltpu.VMEM((2, TQ, LANES), F32),
                        pltpu.VMEM((2, TQ, B_VDIM), F32)],
        compiler_params=pltpu.CompilerParams(
            dimension_semantics=("arbitrary", "arbitrary", "arbitrary"),
            vmem_limit_bytes=VMEM_LIMIT),
        name=f"attn_l{layer}",
    )(q, k, v, lq1, lk1, lq2, lk2, sg)


def _merge_kernel(ya_ref, yb_ref, yc_ref, gate_ref, x_ref, wb_ref, wo_ref, gpost_ref,
                  gfpre_ref, wfi_ref, wfo_ref, gfpost_ref, o_ref):
    merged = None
    for j, y_ref in enumerate((ya_ref, yb_ref, yc_ref)):
        up = jnp.dot(y_ref[...], wb_ref[j], preferred_element_type=F32)
        term = gate_ref[:, j * D_MODEL:(j + 1) * D_MODEL] * up
        merged = term if merged is None else merged + term
    mix = jnp.dot(merged.astype(BF16), wo_ref[...], preferred_element_type=F32)
    x1 = x_ref[...] + _rms(mix, gpost_ref[...])

    h = _rms(x1, gfpre_ref[...]).astype(BF16)
    f = None
    for lo in range(0, D_FF, FF_CHUNK):
        width = min(FF_CHUNK, D_FF - lo)
        g = jnp.dot(h, wfi_ref[:, lo:lo + width], preferred_element_type=F32)
        u = jnp.dot(h, wfi_ref[:, D_FF + lo:D_FF + lo + width], preferred_element_type=F32)
        a = (g * jax.nn.sigmoid(g) * u).astype(BF16)
        part = jnp.dot(a, wfo_ref[lo:lo + width, :], preferred_element_type=F32)
        f = part if f is None else f + part
    o_ref[...] = x1 + _rms(f, gfpost_ref[...])


def _merge(ya, yb, yc, gates, x, wb, wo, gpost, gfpre, wfi, wfo, gfpost, layer):
    n = x.shape[0]
    tm = TM_MERGE
    row = lambda width: pl.BlockSpec((tm, width), lambda i: (i, 0))
    return pl.pallas_call(
        _merge_kernel,
        grid=(n // tm,),
        in_specs=[
            row(BRANCH_WIDTH), row(BRANCH_WIDTH), row(BRANCH_WIDTH),
            row(N_BRANCH * D_MODEL), row(D_MODEL),
            _const_spec((N_BRANCH, BRANCH_WIDTH, D_MODEL)),
            _const_spec((D_MODEL, D_MODEL)),
            _const_spec((1, D_MODEL)),
            _const_spec((1, D_MODEL)),
            _const_spec((D_MODEL, 2 * D_FF)),
            _const_spec((D_FF, D_MODEL)),
            _const_spec((1, D_MODEL)),
        ],
        out_specs=row(D_MODEL),
        out_shape=jax.ShapeDtypeStruct((n, D_MODEL), F32),
        compiler_params=pltpu.CompilerParams(
            dimension_semantics=("arbitrary",), vmem_limit_bytes=VMEM_LIMIT),
        name=f"merge_l{layer}",
    )(ya, yb, yc, gates, x, wb, wo, gpost, gfpre, wfi, wfo, gfpost)


def kernel(x, norm_mix_pre, w_in, gmlp_norm_g, gmlp_norm_b, gmlp_w_s, gmlp_b_s, lambda_q1, lambda_k1,
           lambda_q2, lambda_k2, diff_subln_g, pool_w, pool_scale, w_branch, w_out, norm_mix_post,
           norm_ffn_pre, w_ffn_in, w_ffn_out, norm_ffn_post):
    bsz, s, d = x.shape
    assert (s, d) == (SEQ, D_MODEL)
    depth = w_in.shape[0]
    xf = x.reshape(bsz * s, d)
    vec = lambda a: a.reshape(1, -1)
    for l in range(depth):
        lambda_init = 0.8 - 0.6 * math.exp(-0.3 * l)
        bs = jnp.broadcast_to(gmlp_b_s[l][:, :, None], (A_GROUPS, CHUNK, CHUNK))
        ya, q, k, v, yc, gates = _in_proj(
            xf, vec(norm_mix_pre[l]), w_in[l].astype(BF16), vec(gmlp_norm_g[l]),
            vec(gmlp_norm_b[l]), gmlp_w_s[l], bs, pool_w[l].astype(BF16), vec(pool_scale[l]), l)
        yb = _attention(q, k, v, vec(lambda_q1[l]), vec(lambda_k1[l]), vec(lambda_q2[l]),
                        vec(lambda_k2[l]), vec(diff_subln_g[l]), lambda_init, l)
        xf = _merge(ya, yb, yc, gates, xf, w_branch[l].astype(BF16), w_out[l].astype(BF16),
                    vec(norm_mix_post[l]), vec(norm_ffn_pre[l]), w_ffn_in[l].astype(BF16),
                    w_ffn_out[l].astype(BF16), vec(norm_ffn_post[l]), l)
    return xf.reshape(bsz, s, d)
```

```python
import functools
import math

import jax
import jax.numpy as jnp
from jax import lax
from jax.experimental import pallas as pl
from jax.experimental.pallas import tpu as pltpu

F32 = jnp.float32
BF16 = jnp.bfloat16

D_MODEL = 1024
SEQ = 2048
CHUNK = 128
A_GROUPS = 4
A_WIDTH = 512
B_HEADS = 4
B_HEAD_DIM = 64
B_VDIM = 128
B_WIDTH = 512
C_WINDOWS = (2, 4, 8, 16)
C_GROUP_DIM = 128
C_WIDTH = 512
N_BRANCH = 3
BRANCH_WIDTH = 512
D_FF = 2816
EPS = 1e-6

COL_A = 0
COL_Q = 2 * A_WIDTH
COL_K = COL_Q + 512
COL_V = COL_K + 512
COL_C = COL_V + B_WIDTH
COL_G = COL_C + C_WIDTH
IN_TOTAL = COL_G + N_BRANCH * D_MODEL

HALO = 16
LANES = 128
VMEM_LIMIT = 56 * 1024 * 1024

TM_PROJ = 256
TM_MERGE = 256
TQ = 256
FF_CHUNK = 512


def _rms(x, g):
    ms = jnp.mean(x * x, axis=-1, keepdims=True)
    return x * lax.rsqrt(ms + EPS) * g


def _const_spec(shape):
    nd = len(shape)
    return pl.BlockSpec(shape, lambda *_: (0,) * nd, pipeline_mode=pl.Buffered(1))


def _in_proj_kernel(x_ref, gpre_ref, w_ref, lng_ref, lnb_ref, ws_ref, bs_ref, pw_ref, psc_ref,
                    ya_ref, q_ref, k_ref, v_ref, yc_ref, gate_ref, cbuf):
    tm = x_ref.shape[0]
    pos = (pl.program_id(0) * tm) % SEQ
    h = _rms(x_ref[...], gpre_ref[...]).astype(BF16)

    def proj(lo, width):
        return jnp.dot(h, w_ref[:, lo:lo + width], preferred_element_type=F32)

    za = proj(COL_A, 2 * A_WIDTH)
    ga = 0.5 * za * (1.0 + lax.erf(za * math.sqrt(0.5)))
    u = ga[:, :A_WIDTH]
    vv = ga[:, A_WIDTH:]
    mu = jnp.mean(vv, axis=-1, keepdims=True)
    dv = vv - mu
    var = jnp.mean(dv * dv, axis=-1, keepdims=True)
    vn = (dv * lax.rsqrt(var + EPS) * lng_ref[...] + lnb_ref[...]).astype(BF16)
    trow = lax.broadcasted_iota(jnp.int32, (CHUNK, CHUNK), 0)
    tcol = lax.broadcasted_iota(jnp.int32, (CHUNK, CHUNK), 1)
    for g in range(A_GROUPS):
        wsg = jnp.where(tcol <= trow, ws_ref[g], 0.0).astype(BF16)
        cs = slice(g * CHUNK, (g + 1) * CHUNK)
        for c in range(tm // CHUNK):
            rs = slice(c * CHUNK, (c + 1) * CHUNK)
            mixed = jnp.dot(wsg, vn[rs, cs], preferred_element_type=F32) + bs_ref[g]
            ya_ref[rs, cs] = (u[rs, cs] * mixed).astype(BF16)

    q_ref[...] = (proj(COL_Q, 512) * (B_HEAD_DIM ** -0.5)).astype(BF16)
    k_ref[...] = proj(COL_K, 512).astype(BF16)
    v_ref[...] = proj(COL_V, B_WIDTH).astype(BF16)

    zc = proj(COL_C, C_WIDTH)

    @pl.when(pos == 0)
    def _():
        cbuf[0:HALO, :] = jnp.zeros((HALO, C_WIDTH), F32)

    cbuf[HALO:, :] = zc
    p_all = cbuf[...]
    cbuf[0:HALO, :] = p_all[tm:, :]
    s2 = p_all + pltpu.roll(p_all, 1, 0)
    s4 = s2[:, 128:] + pltpu.roll(s2[:, 128:], 2, 0)
    s8 = s4[:, 128:] + pltpu.roll(s4[:, 128:], 4, 0)
    s16 = s8[:, 128:] + pltpu.roll(s8[:, 128:], 8, 0)
    sums = (s2[HALO:, :128], s4[HALO:, :128], s8[HALO:, :128], s16[HALO:, :])
    tpos = pos + 1 + lax.broadcasted_iota(jnp.int32, (tm, C_GROUP_DIM), 0)
    for g, w in enumerate(C_WINDOWS):
        cs = slice(g * C_GROUP_DIM, (g + 1) * C_GROUP_DIM)
        inv_cnt = 1.0 / jnp.minimum(tpos, w).astype(F32)
        pooled = (sums[g] * inv_cnt - zc[:, cs]).astype(BF16)
        yc = jnp.dot(pooled, pw_ref[g], preferred_element_type=F32) * psc_ref[:, cs]
        yc_ref[:, cs] = yc.astype(BF16)

    for j in range(N_BRANCH):
        zg = proj(COL_G + j * D_MODEL, D_MODEL)
        gate_ref[:, j * D_MODEL:(j + 1) * D_MODEL] = jax.nn.sigmoid(zg)


def _in_proj(x, gpre, w_in, lng, lnb, ws, bs, pw, psc, layer):
    n = x.shape[0]
    tm = TM_PROJ
    row = lambda width: pl.BlockSpec((tm, width), lambda i: (i, 0))
    out_shapes = (
        jax.ShapeDtypeStruct((n, A_WIDTH), BF16),
        jax.ShapeDtypeStruct((n, 512), BF16),
        jax.ShapeDtypeStruct((n, 512), BF16),
        jax.ShapeDtypeStruct((n, B_WIDTH), BF16),
        jax.ShapeDtypeStruct((n, C_WIDTH), BF16),
        jax.ShapeDtypeStruct((n, N_BRANCH * D_MODEL), F32),
    )
    return pl.pallas_call(
        _in_proj_kernel,
        grid=(n // tm,),
        in_specs=[
            row(D_MODEL),
            _const_spec((1, D_MODEL)),
            _const_spec((D_MODEL, IN_TOTAL)),
            _const_spec((1, A_WIDTH)),
            _const_spec((1, A_WIDTH)),
            _const_spec((A_GROUPS, CHUNK, CHUNK)),
            _const_spec((A_GROUPS, CHUNK, CHUNK)),
            _const_spec((len(C_WINDOWS), C_GROUP_DIM, C_GROUP_DIM)),
            _const_spec((1, C_WIDTH)),
        ],
        out_specs=[row(A_WIDTH), row(512), row(512), row(B_WIDTH), row(C_WIDTH),
                   row(N_BRANCH * D_MODEL)],
        out_shape=out_shapes,
        scratch_shapes=[pltpu.VMEM((HALO + tm, C_WIDTH), F32)],
        compiler_params=pltpu.CompilerParams(
            dimension_semantics=("arbitrary",), vmem_limit_bytes=VMEM_LIMIT),
        name=f"in_proj_l{layer}",
    )(x, gpre, w_in, lng, lnb, ws, bs, pw, psc)


def _attn_tile(n_before, q_ref, k_ref, v_ref, lam, sg_ref, o_ref, lambda_init):
    tq = q_ref.shape[0]
    l0 = n_before * tq
    q = q_ref[...]
    lane = lax.broadcasted_iota(jnp.int32, q.shape, 1)
    zero = jnp.zeros_like(q)
    qs = (jnp.where(lane < B_HEAD_DIM, q, zero), jnp.where(lane >= B_HEAD_DIM, q, zero))
    row = lax.broadcasted_iota(jnp.int32, (tq, tq), 0)
    col = lax.broadcasted_iota(jnp.int32, (tq, tq), 1)
    nt = (((1,), (1,)), ((), ()))
    probs = []
    for c in range(2):
        s_diag = lax.dot_general(qs[c], k_ref[l0:l0 + tq, :], nt, preferred_element_type=F32)
        s_diag = jnp.where(col <= row, s_diag, -jnp.inf)
        m = jnp.max(s_diag, axis=1, keepdims=True)
        if n_before:
            s_main = lax.dot_general(qs[c], k_ref[0:l0, :], nt, preferred_element_type=F32)
            m = jnp.maximum(m, jnp.max(s_main, axis=1, keepdims=True))
            p_main = jnp.exp(s_main - m)
        p_diag = jnp.exp(s_diag - m)
        denom = jnp.sum(p_diag, axis=1, keepdims=True)
        if n_before:
            denom = denom + jnp.sum(p_main, axis=1, keepdims=True)
        r = (1.0 if c == 0 else lam) / denom
        probs.append((p_main * r if n_before else None, p_diag * r))
    o = jnp.dot((probs[0][1] - probs[1][1]).astype(BF16), v_ref[l0:l0 + tq, :],
                preferred_element_type=F32)
    if n_before:
        o = o + jnp.dot((probs[0][0] - probs[1][0]).astype(BF16), v_ref[0:l0, :],
                        preferred_element_type=F32)
    y = _rms(o, sg_ref[...]) * (1.0 - lambda_init)
    o_ref[...] = y.astype(BF16)


def _attn_kernel(q_ref, k_ref, v_ref, lq1_ref, lk1_ref, lq2_ref, lk2_ref, sg_ref, o_ref, *,
                 lambda_init):
    i = pl.program_id(2)
    lam = (jnp.exp(jnp.sum(lq1_ref[...] * lk1_ref[...], keepdims=True))
           - jnp.exp(jnp.sum(lq2_ref[...] * lk2_ref[...], keepdims=True)) + lambda_init)
    for n in range(k_ref.shape[0] // q_ref.shape[0]):
        @pl.when(i == n)
        def _(n=n):
            _attn_tile(n, q_ref, k_ref, v_ref, lam, sg_ref, o_ref, lambda_init)


def _attention(q, k, v, lq1, lk1, lq2, lk2, sg, lambda_init, layer):
    n = q.shape[0]
    bsz = n // SEQ
    nq = SEQ // TQ
    kv_spec = pl.BlockSpec((SEQ, B_VDIM), lambda b, h, i: (b, h))
    vec = lambda width: pl.BlockSpec((1, width), lambda b, h, i: (0, 0))
    return pl.pallas_call(
        functools.partial(_attn_kernel, lambda_init=lambda_init),
        grid=(bsz, B_HEADS, nq),
        in_specs=[
            pl.BlockSpec((TQ, B_VDIM), lambda b, h, i: (b * nq + i, h)),
            kv_spec, kv_spec,
            vec(B_HEAD_DIM), vec(B_HEAD_DIM), vec(B_HEAD_DIM), vec(B_HEAD_DIM), vec(B_VDIM),
        ],
        out_specs=pl.BlockSpec((TQ, B_VDIM), lambda b, h, i: (b * nq + i, h)),
        out_shape=jax.ShapeDtypeStruct((n, B_WIDTH), BF16),
        compiler_params=pltpu.CompilerParams(
            dimension_semantics=("arbitrary", "arbitrary", "arbitrary"),
            vmem_limit_bytes=VMEM_LIMIT),
        name=f"attn_l{layer}",
    )(q, k, v, lq1, lk1, lq2, lk2, sg)


def _merge_kernel(ya_ref, yb_ref, yc_ref, gate_ref, x_ref, wb_ref, wo_ref, gpost_ref,
                  gfpre_ref, wfi_ref, wfo_ref, gfpost_ref, o_ref):
    merged = None
    for j, y_ref in enumerate((ya_ref, yb_ref, yc_ref)):
        up = jnp.dot(y_ref[...], wb_ref[j], preferred_element_type=F32)
        term = gate_ref[:, j * D_MODEL:(j + 1) * D_MODEL] * up
        merged = term if merged is None else merged + term
    mix = jnp.dot(merged.astype(BF16), wo_ref[...], preferred_element_type=F32)
    x1 = x_ref[...] + _rms(mix, gpost_ref[...])

    h = _rms(x1, gfpre_ref[...]).astype(BF16)
    f = None
    for lo in range(0, D_FF, FF_CHUNK):
        width = min(FF_CHUNK, D_FF - lo)
        g = jnp.dot(h, wfi_ref[:, lo:lo + width], preferred_element_type=F32)
        u = jnp.dot(h, wfi_ref[:, D_FF + lo:D_FF + lo + width], preferred_element_type=F32)
        a = (g * jax.nn.sigmoid(g) * u).astype(BF16)
        part = jnp.dot(a, wfo_ref[lo:lo + width, :], preferred_element_type=F32)
        f = part if f is None else f + part
    o_ref[...] = x1 + _rms(f, gfpost_ref[...])


def _merge(ya, yb, yc, gates, x, wb, wo, gpost, gfpre, wfi, wfo, gfpost, layer):
    n = x.shape[0]
    tm = TM_MERGE
    row = lambda width: pl.BlockSpec((tm, width), lambda i: (i, 0))
    return pl.pallas_call(
        _merge_kernel,
        grid=(n // tm,),
        in_specs=[
            row(BRANCH_WIDTH), row(BRANCH_WIDTH), row(BRANCH_WIDTH),
            row(N_BRANCH * D_MODEL), row(D_MODEL),
            _const_spec((N_BRANCH, BRANCH_WIDTH, D_MODEL)),
            _const_spec((D_MODEL, D_MODEL)),
            _const_spec((1, D_MODEL)),
            _const_spec((1, D_MODEL)),
            _const_spec((D_MODEL, 2 * D_FF)),
            _const_spec((D_FF, D_MODEL)),
            _const_spec((1, D_MODEL)),
        ],
        out_specs=row(D_MODEL),
        out_shape=jax.ShapeDtypeStruct((n, D_MODEL), F32),
        compiler_params=pltpu.CompilerParams(
            dimension_semantics=("arbitrary",), vmem_limit_bytes=VMEM_LIMIT),
        name=f"merge_l{layer}",
    )(ya, yb, yc, gates, x, wb, wo, gpost, gfpre, wfi, wfo, gfpost)


def kernel(x, norm_mix_pre, w_in, gmlp_norm_g, gmlp_norm_b, gmlp_w_s, gmlp_b_s, lambda_q1, lambda_k1,
           lambda_q2, lambda_k2, diff_subln_g, pool_w, pool_scale, w_branch, w_out, norm_mix_post,
           norm_ffn_pre, w_ffn_in, w_ffn_out, norm_ffn_post):
    bsz, s, d = x.shape
    assert (s, d) == (SEQ, D_MODEL)
    depth = w_in.shape[0]
    xf = x.reshape(bsz * s, d)
    vec = lambda a: a.reshape(1, -1)
    for l in range(depth):
        lambda_init = 0.8 - 0.6 * math.exp(-0.3 * l)
        bs = jnp.broadcast_to(gmlp_b_s[l][:, :, None], (A_GROUPS, CHUNK, CHUNK))
        ya, q, k, v, yc, gates = _in_proj(
            xf, vec(norm_mix_pre[l]), w_in[l].astype(BF16), vec(gmlp_norm_g[l]),
            vec(gmlp_norm_b[l]), gmlp_w_s[l], bs, pool_w[l].astype(BF16), vec(pool_scale[l]), l)
        yb = _attention(q, k, v, vec(lambda_q1[l]), vec(lambda_k1[l]), vec(lambda_q2[l]),
                        vec(lambda_k2[l]), vec(diff_subln_g[l]), lambda_init, l)
        xf = _merge(ya, yb, yc, gates, xf, w_branch[l].astype(BF16), w_out[l].astype(BF16),
                    vec(norm_mix_post[l]), vec(norm_ffn_pre[l]), w_ffn_in[l].astype(BF16),
                    w_ffn_out[l].astype(BF16), vec(norm_ffn_post[l]), l)
    return xf.reshape(bsz, s, d)
```

```python
import functools
import math

import jax
import jax.numpy as jnp
from jax import lax
from jax.experimental import pallas as pl
from jax.experimental.pallas import tpu as pltpu

F32 = jnp.float32
BF16 = jnp.bfloat16

D_MODEL = 1024
SEQ = 2048
CHUNK = 128
A_GROUPS = 4
A_WIDTH = 512
B_HEADS = 4
B_HEAD_DIM = 64
B_VDIM = 128
B_WIDTH = 512
C_WINDOWS = (2, 4, 8, 16)
C_GROUP_DIM = 128
C_WIDTH = 512
N_BRANCH = 3
BRANCH_WIDTH = 512
D_FF = 2816
EPS = 1e-6

COL_A = 0
COL_Q = 2 * A_WIDTH
COL_K = COL_Q + 512
COL_V = COL_K + 512
COL_C = COL_V + B_WIDTH
COL_G = COL_C + C_WIDTH
IN_TOTAL = COL_G + N_BRANCH * D_MODEL

HALO = 16
LANES = 128
VMEM_LIMIT = 56 * 1024 * 1024

TM_PROJ = 512
TM_MERGE = 512
TQ = 256
FF_CHUNK = 512


def _rms(x, g):
    ms = jnp.mean(x * x, axis=-1, keepdims=True)
    return x * lax.rsqrt(ms + EPS) * g


def _const_spec(shape):
    nd = len(shape)
    return pl.BlockSpec(shape, lambda *_: (0,) * nd, pipeline_mode=pl.Buffered(1))


def _in_proj_kernel(x_ref, gpre_ref, w_ref, lng_ref, lnb_ref, ws_ref, bs_ref, pw_ref, psc_ref,
                    ya_ref, q_ref, k_ref, v_ref, yc_ref, gate_ref, cbuf):
    tm = x_ref.shape[0]
    pos = (pl.program_id(0) * tm) % SEQ
    h = _rms(x_ref[...], gpre_ref[...]).astype(BF16)

    def proj(lo, width):
        return jnp.dot(h, w_ref[:, lo:lo + width], preferred_element_type=F32)

    za = proj(COL_A, 2 * A_WIDTH)
    ga = 0.5 * za * (1.0 + lax.erf(za * math.sqrt(0.5)))
    u = ga[:, :A_WIDTH]
    vv = ga[:, A_WIDTH:]
    mu = jnp.mean(vv, axis=-1, keepdims=True)
    dv = vv - mu
    var = jnp.mean(dv * dv, axis=-1, keepdims=True)
    vn = (dv * lax.rsqrt(var + EPS) * lng_ref[...] + lnb_ref[...]).astype(BF16)
    trow = lax.broadcasted_iota(jnp.int32, (CHUNK, CHUNK), 0)
    tcol = lax.broadcasted_iota(jnp.int32, (CHUNK, CHUNK), 1)
    for g in range(A_GROUPS):
        wsg = jnp.where(tcol <= trow, ws_ref[g], 0.0).astype(BF16)
        cs = slice(g * CHUNK, (g + 1) * CHUNK)
        for c in range(tm // CHUNK):
            rs = slice(c * CHUNK, (c + 1) * CHUNK)
            mixed = jnp.dot(wsg, vn[rs, cs], preferred_element_type=F32) + bs_ref[g]
            ya_ref[rs, cs] = (u[rs, cs] * mixed).astype(BF16)

    q_ref[...] = (proj(COL_Q, 512) * (B_HEAD_DIM ** -0.5)).astype(BF16)
    k_ref[...] = proj(COL_K, 512).astype(BF16)
    v_ref[...] = proj(COL_V, B_WIDTH).astype(BF16)

    zc = proj(COL_C, C_WIDTH)

    @pl.when(pos == 0)
    def _():
        cbuf[0:HALO, :] = jnp.zeros((HALO, C_WIDTH), F32)

    cbuf[HALO:, :] = zc
    p_all = cbuf[...]
    cbuf[0:HALO, :] = p_all[tm:, :]
    s2 = p_all + pltpu.roll(p_all, 1, 0)
    s4 = s2[:, 128:] + pltpu.roll(s2[:, 128:], 2, 0)
    s8 = s4[:, 128:] + pltpu.roll(s4[:, 128:], 4, 0)
    s16 = s8[:, 128:] + pltpu.roll(s8[:, 128:], 8, 0)
    sums = (s2[HALO:, :128], s4[HALO:, :128], s8[HALO:, :128], s16[HALO:, :])
    tpos = pos + 1 + lax.broadcasted_iota(jnp.int32, (tm, C_GROUP_DIM), 0)
    for g, w in enumerate(C_WINDOWS):
        cs = slice(g * C_GROUP_DIM, (g + 1) * C_GROUP_DIM)
        inv_cnt = 1.0 / jnp.minimum(tpos, w).astype(F32)
        pooled = (sums[g] * inv_cnt - zc[:, cs]).astype(BF16)
        yc = jnp.dot(pooled, pw_ref[g], preferred_element_type=F32) * psc_ref[:, cs]
        yc_ref[:, cs] = yc.astype(BF16)

    for j in range(N_BRANCH):
        zg = proj(COL_G + j * D_MODEL, D_MODEL)
        gate_ref[:, j * D_MODEL:(j + 1) * D_MODEL] = jax.nn.sigmoid(zg)


def _in_proj(x, gpre, w_in, lng, lnb, ws, bs, pw, psc, layer):
    n = x.shape[0]
    tm = TM_PROJ
    row = lambda width: pl.BlockSpec((tm, width), lambda i: (i, 0))
    out_shapes = (
        jax.ShapeDtypeStruct((n, A_WIDTH), BF16),
        jax.ShapeDtypeStruct((n, 512), BF16),
        jax.ShapeDtypeStruct((n, 512), BF16),
        jax.ShapeDtypeStruct((n, B_WIDTH), BF16),
        jax.ShapeDtypeStruct((n, C_WIDTH), BF16),
        jax.ShapeDtypeStruct((n, N_BRANCH * D_MODEL), F32),
    )
    return pl.pallas_call(
        _in_proj_kernel,
        grid=(n // tm,),
        in_specs=[
            row(D_MODEL),
            _const_spec((1, D_MODEL)),
            _const_spec((D_MODEL, IN_TOTAL)),
            _const_spec((1, A_WIDTH)),
            _const_spec((1, A_WIDTH)),
            _const_spec((A_GROUPS, CHUNK, CHUNK)),
            _const_spec((A_GROUPS, CHUNK, CHUNK)),
            _const_spec((len(C_WINDOWS), C_GROUP_DIM, C_GROUP_DIM)),
            _const_spec((1, C_WIDTH)),
        ],
        out_specs=[row(A_WIDTH), row(512), row(512), row(B_WIDTH), row(C_WIDTH),
                   row(N_BRANCH * D_MODEL)],
        out_shape=out_shapes,
        scratch_shapes=[pltpu.VMEM((HALO + tm, C_WIDTH), F32)],
        compiler_params=pltpu.CompilerParams(
            dimension_semantics=("arbitrary",), vmem_limit_bytes=VMEM_LIMIT),
        name=f"in_proj_l{layer}",
    )(x, gpre, w_in, lng, lnb, ws, bs, pw, psc)


def _attn_head(n_before, q, k_ref, v_ref, hs, lam, sg, lambda_init):
    tq = q.shape[0]
    l0 = n_before * tq
    lane = lax.broadcasted_iota(jnp.int32, q.shape, 1)
    zero = jnp.zeros_like(q)
    qs = (jnp.where(lane < B_HEAD_DIM, q, zero), jnp.where(lane >= B_HEAD_DIM, q, zero))
    row = lax.broadcasted_iota(jnp.int32, (tq, tq), 0)
    col = lax.broadcasted_iota(jnp.int32, (tq, tq), 1)
    nt = (((1,), (1,)), ((), ()))
    probs = []
    for c in range(2):
        s_diag = lax.dot_general(qs[c], k_ref[l0:l0 + tq, hs], nt, preferred_element_type=F32)
        s_diag = jnp.where(col <= row, s_diag, -jnp.inf)
        m = jnp.max(s_diag, axis=1, keepdims=True)
        if n_before:
            s_main = lax.dot_general(qs[c], k_ref[0:l0, hs], nt, preferred_element_type=F32)
            m = jnp.maximum(m, jnp.max(s_main, axis=1, keepdims=True))
            p_main = jnp.exp(s_main - m)
        p_diag = jnp.exp(s_diag - m)
        denom = jnp.sum(p_diag, axis=1, keepdims=True)
        if n_before:
            denom = denom + jnp.sum(p_main, axis=1, keepdims=True)
        r = (1.0 if c == 0 else lam) / denom
        probs.append((p_main * r if n_before else None, p_diag * r))
    o = jnp.dot((probs[0][1] - probs[1][1]).astype(BF16), v_ref[l0:l0 + tq, hs],
                preferred_element_type=F32)
    if n_before:
        o = o + jnp.dot((probs[0][0] - probs[1][0]).astype(BF16), v_ref[0:l0, hs],
                        preferred_element_type=F32)
    return (_rms(o, sg) * (1.0 - lambda_init)).astype(BF16)


def _attn_kernel(q_ref, k_ref, v_ref, lq1_ref, lk1_ref, lq2_ref, lk2_ref, sg_ref, o_ref, *,
                 lambda_init):
    i = pl.program_id(1)
    lam = (jnp.exp(jnp.sum(lq1_ref[...] * lk1_ref[...], keepdims=True))
           - jnp.exp(jnp.sum(lq2_ref[...] * lk2_ref[...], keepdims=True)) + lambda_init)
    for n in range(k_ref.shape[0] // q_ref.shape[0]):
        @pl.when(i == n)
        def _(n=n):
            for h in range(B_HEADS):
                hs = slice(h * B_VDIM, (h + 1) * B_VDIM)
                o_ref[:, hs] = _attn_head(n, q_ref[:, hs], k_ref, v_ref, hs, lam, sg_ref[...],
                                          lambda_init)


def _attention(q, k, v, lq1, lk1, lq2, lk2, sg, lambda_init, layer):
    n = q.shape[0]
    bsz = n // SEQ
    nq = SEQ // TQ
    kv_spec = pl.BlockSpec((SEQ, B_WIDTH), lambda b, i: (b, 0))
    vec = lambda width: pl.BlockSpec((1, width), lambda b, i: (0, 0))
    return pl.pallas_call(
        functools.partial(_attn_kernel, lambda_init=lambda_init),
        grid=(bsz, nq),
        in_specs=[
            pl.BlockSpec((TQ, B_WIDTH), lambda b, i: (b * nq + i, 0)),
            kv_spec, kv_spec,
            vec(B_HEAD_DIM), vec(B_HEAD_DIM), vec(B_HEAD_DIM), vec(B_HEAD_DIM), vec(B_VDIM),
        ],
        out_specs=pl.BlockSpec((TQ, B_WIDTH), lambda b, i: (b * nq + i, 0)),
        out_shape=jax.ShapeDtypeStruct((n, B_WIDTH), BF16),
        compiler_params=pltpu.CompilerParams(
            dimension_semantics=("arbitrary", "arbitrary"), vmem_limit_bytes=VMEM_LIMIT),
        name=f"attn_l{layer}",
    )(q, k, v, lq1, lk1, lq2, lk2, sg)


def _merge_kernel(ya_ref, yb_ref, yc_ref, gate_ref, x_ref, wb_ref, wo_ref, gpost_ref,
                  gfpre_ref, wfi_ref, wfo_ref, gfpost_ref, o_ref):
    merged = None
    for j, y_ref in enumerate((ya_ref, yb_ref, yc_ref)):
        up = jnp.dot(y_ref[...], wb_ref[j], preferred_element_type=F32)
        term = gate_ref[:, j * D_MODEL:(j + 1) * D_MODEL] * up
        merged = term if merged is None else merged + term
    mix = jnp.dot(merged.astype(BF16), wo_ref[...], preferred_element_type=F32)
    x1 = x_ref[...] + _rms(mix, gpost_ref[...])

    h = _rms(x1, gfpre_ref[...]).astype(BF16)
    f = None
    for lo in range(0, D_FF, FF_CHUNK):
        width = min(FF_CHUNK, D_FF - lo)
        g = jnp.dot(h, wfi_ref[:, lo:lo + width], preferred_element_type=F32)
        u = jnp.dot(h, wfi_ref[:, D_FF + lo:D_FF + lo + width], preferred_element_type=F32)
        a = (g * jax.nn.sigmoid(g) * u).astype(BF16)
        part = jnp.dot(a, wfo_ref[lo:lo + width, :], preferred_element_type=F32)
        f = part if f is None else f + part
    o_ref[...] = x1 + _rms(f, gfpost_ref[...])


def _merge(ya, yb, yc, gates, x, wb, wo, gpost, gfpre, wfi, wfo, gfpost, layer):
    n = x.shape[0]
    tm = TM_MERGE
    row = lambda width: pl.BlockSpec((tm, width), lambda i: (i, 0))
    return pl.pallas_call(
        _merge_kernel,
        grid=(n // tm,),
        in_specs=[
            row(BRANCH_WIDTH), row(BRANCH_WIDTH), row(BRANCH_WIDTH),
            row(N_BRANCH * D_MODEL), row(D_MODEL),
            _const_spec((N_BRANCH, BRANCH_WIDTH, D_MODEL)),
            _const_spec((D_MODEL, D_MODEL)),
            _const_spec((1, D_MODEL)),
            _const_spec((1, D_MODEL)),
            _const_spec((D_MODEL, 2 * D_FF)),
            _const_spec((D_FF, D_MODEL)),
            _const_spec((1, D_MODEL)),
        ],
        out_specs=row(D_MODEL),
        out_shape=jax.ShapeDtypeStruct((n, D_MODEL), F32),
        compiler_params=pltpu.CompilerParams(
            dimension_semantics=("arbitrary",), vmem_limit_bytes=VMEM_LIMIT),
        name=f"merge_l{layer}",
    )(ya, yb, yc, gates, x, wb, wo, gpost, gfpre, wfi, wfo, gfpost)


def kernel(x, norm_mix_pre, w_in, gmlp_norm_g, gmlp_norm_b, gmlp_w_s, gmlp_b_s, lambda_q1, lambda_k1,
           lambda_q2, lambda_k2, diff_subln_g, pool_w, pool_scale, w_branch, w_out, norm_mix_post,
           norm_ffn_pre, w_ffn_in, w_ffn_out, norm_ffn_post):
    bsz, s, d = x.shape
    assert (s, d) == (SEQ, D_MODEL)
    depth = w_in.shape[0]
    xf = x.reshape(bsz * s, d)
    vec = lambda a: a.reshape(1, -1)
    for l in range(depth):
        lambda_init = 0.8 - 0.6 * math.exp(-0.3 * l)
        bs = jnp.broadcast_to(gmlp_b_s[l][:, :, None], (A_GROUPS, CHUNK, CHUNK))
        ya, q, k, v, yc, gates = _in_proj(
            xf, vec(norm_mix_pre[l]), w_in[l].astype(BF16), vec(gmlp_norm_g[l]),
            vec(gmlp_norm_b[l]), gmlp_w_s[l], bs, pool_w[l].astype(BF16), vec(pool_scale[l]), l)
        yb = _attention(q, k, v, vec(lambda_q1[l]), vec(lambda_k1[l]), vec(lambda_q2[l]),
                        vec(lambda_k2[l]), vec(diff_subln_g[l]), lambda_init, l)
        xf = _merge(ya, yb, yc, gates, xf, w_branch[l].astype(BF16), w_out[l].astype(BF16),
                    vec(norm_mix_post[l]), vec(norm_ffn_pre[l]), w_ffn_in[l].astype(BF16),
                    w_ffn_out[l].astype(BF16), vec(norm_ffn_post[l]), l)
    return xf.reshape(bsz, s, d)
```

```python
import functools
import math

import jax
import jax.numpy as jnp
from jax import lax
from jax.experimental import pallas as pl
from jax.experimental.pallas import tpu as pltpu

F32 = jnp.float32
BF16 = jnp.bfloat16

D_MODEL = 1024
SEQ = 2048
CHUNK = 128
A_GROUPS = 4
A_WIDTH = 512
B_HEADS = 4
B_HEAD_DIM = 64
B_VDIM = 128
B_WIDTH = 512
C_WINDOWS = (2, 4, 8, 16)
C_GROUP_DIM = 128
C_WIDTH = 512
N_BRANCH = 3
BRANCH_WIDTH = 512
D_FF = 2816
EPS = 1e-6

COL_A = 0
COL_Q = 2 * A_WIDTH
COL_K = COL_Q + 512
COL_V = COL_K + 512
COL_C = COL_V + B_WIDTH
COL_G = COL_C + C_WIDTH
IN_TOTAL = COL_G + N_BRANCH * D_MODEL

HALO = 16
LANES = 128
VMEM_LIMIT = 56 * 1024 * 1024

TM_PROJ = 512
TM_MERGE = 512
TQ = 256
FF_CHUNK = 512


def _rms(x, g):
    ms = jnp.mean(x * x, axis=-1, keepdims=True)
    return x * lax.rsqrt(ms + EPS) * g


def _layer_spec(layer, shape):
    nd = len(shape)
    return pl.BlockSpec((None,) + tuple(shape), lambda *_: (layer,) + (0,) * nd,
                        pipeline_mode=pl.Buffered(1))


def _in_proj_kernel(x_ref, gpre_ref, w_ref, lng_ref, lnb_ref, ws_ref, bs_ref, pw_ref, psc_ref,
                    ya_ref, q_ref, k_ref, v_ref, yc_ref, gate_ref, cbuf):
    tm = x_ref.shape[0]
    pos = (pl.program_id(0) * tm) % SEQ
    h = _rms(x_ref[...], gpre_ref[...]).astype(BF16)

    def proj(lo, width):
        return jnp.dot(h, w_ref[:, lo:lo + width], preferred_element_type=F32)

    za = proj(COL_A, 2 * A_WIDTH)
    ga = 0.5 * za * (1.0 + lax.erf(za * math.sqrt(0.5)))
    u = ga[:, :A_WIDTH]
    vv = ga[:, A_WIDTH:]
    mu = jnp.mean(vv, axis=-1, keepdims=True)
    dv = vv - mu
    var = jnp.mean(dv * dv, axis=-1, keepdims=True)
    vn = (dv * lax.rsqrt(var + EPS) * lng_ref[...] + lnb_ref[...]).astype(BF16)
    trow = lax.broadcasted_iota(jnp.int32, (CHUNK, CHUNK), 0)
    tcol = lax.broadcasted_iota(jnp.int32, (CHUNK, CHUNK), 1)
    for g in range(A_GROUPS):
        wsg = jnp.where(tcol <= trow, ws_ref[g], 0.0).astype(BF16)
        cs = slice(g * CHUNK, (g + 1) * CHUNK)
        for c in range(tm // CHUNK):
            rs = slice(c * CHUNK, (c + 1) * CHUNK)
            mixed = jnp.dot(wsg, vn[rs, cs], preferred_element_type=F32) + bs_ref[g]
            ya_ref[rs, cs] = (u[rs, cs] * mixed).astype(BF16)

    q_ref[...] = (proj(COL_Q, 512) * (B_HEAD_DIM ** -0.5)).astype(BF16)
    k_ref[...] = proj(COL_K, 512).astype(BF16)
    v_ref[...] = proj(COL_V, B_WIDTH).astype(BF16)

    zc = proj(COL_C, C_WIDTH)

    @pl.when(pos == 0)
    def _():
        cbuf[0:HALO, :] = jnp.zeros((HALO, C_WIDTH), F32)

    cbuf[HALO:, :] = zc
    p_all = cbuf[...]
    cbuf[0:HALO, :] = p_all[tm:, :]
    s2 = p_all + pltpu.roll(p_all, 1, 0)
    s4 = s2[:, 128:] + pltpu.roll(s2[:, 128:], 2, 0)
    s8 = s4[:, 128:] + pltpu.roll(s4[:, 128:], 4, 0)
    s16 = s8[:, 128:] + pltpu.roll(s8[:, 128:], 8, 0)
    sums = (s2[HALO:, :128], s4[HALO:, :128], s8[HALO:, :128], s16[HALO:, :])
    tpos = pos + 1 + lax.broadcasted_iota(jnp.int32, (tm, C_GROUP_DIM), 0)
    for g, w in enumerate(C_WINDOWS):
        cs = slice(g * C_GROUP_DIM, (g + 1) * C_GROUP_DIM)
        inv_cnt = 1.0 / jnp.minimum(tpos, w).astype(F32)
        pooled = (sums[g] * inv_cnt - zc[:, cs]).astype(BF16)
        yc = jnp.dot(pooled, pw_ref[g], preferred_element_type=F32) * psc_ref[:, cs]
        yc_ref[:, cs] = yc.astype(BF16)

    for j in range(N_BRANCH):
        zg = proj(COL_G + j * D_MODEL, D_MODEL)
        gate_ref[:, j * D_MODEL:(j + 1) * D_MODEL] = jax.nn.sigmoid(zg)


def _in_proj(x, gpre, w_in, lng, lnb, ws, bs, pw, psc, layer):
    n = x.shape[0]
    tm = TM_PROJ
    row = lambda width: pl.BlockSpec((tm, width), lambda i: (i, 0))
    out_shapes = (
        jax.ShapeDtypeStruct((n, A_WIDTH), BF16),
        jax.ShapeDtypeStruct((n, 512), BF16),
        jax.ShapeDtypeStruct((n, 512), BF16),
        jax.ShapeDtypeStruct((n, B_WIDTH), BF16),
        jax.ShapeDtypeStruct((n, C_WIDTH), BF16),
        jax.ShapeDtypeStruct((n, N_BRANCH * D_MODEL), F32),
    )
    return pl.pallas_call(
        _in_proj_kernel,
        grid=(n // tm,),
        in_specs=[
            row(D_MODEL),
            _layer_spec(layer, (1, D_MODEL)),
            _layer_spec(layer, (D_MODEL, IN_TOTAL)),
            _layer_spec(layer, (1, A_WIDTH)),
            _layer_spec(layer, (1, A_WIDTH)),
            _layer_spec(layer, (A_GROUPS, CHUNK, CHUNK)),
            _layer_spec(layer, (A_GROUPS, CHUNK, CHUNK)),
            _layer_spec(layer, (len(C_WINDOWS), C_GROUP_DIM, C_GROUP_DIM)),
            _layer_spec(layer, (1, C_WIDTH)),
        ],
        out_specs=[row(A_WIDTH), row(512), row(512), row(B_WIDTH), row(C_WIDTH),
                   row(N_BRANCH * D_MODEL)],
        out_shape=out_shapes,
        scratch_shapes=[pltpu.VMEM((HALO + tm, C_WIDTH), F32)],
        compiler_params=pltpu.CompilerParams(
            dimension_semantics=("arbitrary",), vmem_limit_bytes=VMEM_LIMIT),
        name=f"in_proj_l{layer}",
    )(x, gpre, w_in, lng, lnb, ws, bs, pw, psc)


def _attn_head(n_before, q, k_ref, v_ref, hs, lam, sg, lambda_init):
    tq = q.shape[0]
    l0 = n_before * tq
    lane = lax.broadcasted_iota(jnp.int32, q.shape, 1)
    zero = jnp.zeros_like(q)
    qs = (jnp.where(lane < B_HEAD_DIM, q, zero), jnp.where(lane >= B_HEAD_DIM, q, zero))
    row = lax.broadcasted_iota(jnp.int32, (tq, tq), 0)
    col = lax.broadcasted_iota(jnp.int32, (tq, tq), 1)
    nt = (((1,), (1,)), ((), ()))
    probs = []
    for c in range(2):
        s_diag = lax.dot_general(qs[c], k_ref[l0:l0 + tq, hs], nt, preferred_element_type=F32)
        s_diag = jnp.where(col <= row, s_diag, -jnp.inf)
        m = jnp.max(s_diag, axis=1, keepdims=True)
        if n_before:
            s_main = lax.dot_general(qs[c], k_ref[0:l0, hs], nt, preferred_element_type=F32)
            m = jnp.maximum(m, jnp.max(s_main, axis=1, keepdims=True))
            p_main = jnp.exp(s_main - m)
        p_diag = jnp.exp(s_diag - m)
        denom = jnp.sum(p_diag, axis=1, keepdims=True)
        if n_before:
            denom = denom + jnp.sum(p_main, axis=1, keepdims=True)
        r = (1.0 if c == 0 else lam) / denom
        probs.append((p_main * r if n_before else None, p_diag * r))
    o = jnp.dot((probs[0][1] - probs[1][1]).astype(BF16), v_ref[l0:l0 + tq, hs],
                preferred_element_type=F32)
    if n_before:
        o = o + jnp.dot((probs[0][0] - probs[1][0]).astype(BF16), v_ref[0:l0, hs],
                        preferred_element_type=F32)
    return (_rms(o, sg) * (1.0 - lambda_init)).astype(BF16)


def _attn_kernel(q_ref, k_ref, v_ref, lq1_ref, lk1_ref, lq2_ref, lk2_ref, sg_ref, o_ref, *,
                 lambda_init):
    i = pl.program_id(1)
    lam = (jnp.exp(jnp.sum(lq1_ref[...] * lk1_ref[...], keepdims=True))
           - jnp.exp(jnp.sum(lq2_ref[...] * lk2_ref[...], keepdims=True)) + lambda_init)
    for n in range(k_ref.shape[0] // q_ref.shape[0]):
        @pl.when(i == n)
        def _(n=n):
            for h in range(B_HEADS):
                hs = slice(h * B_VDIM, (h + 1) * B_VDIM)
                o_ref[:, hs] = _attn_head(n, q_ref[:, hs], k_ref, v_ref, hs, lam, sg_ref[...],
                                          lambda_init)


def _attention(q, k, v, lq1, lk1, lq2, lk2, sg, lambda_init, layer):
    n = q.shape[0]
    bsz = n // SEQ
    nq = SEQ // TQ
    kv_spec = pl.BlockSpec((SEQ, B_WIDTH), lambda b, i: (b, 0))
    vec = lambda width: _layer_spec(layer, (1, width))
    return pl.pallas_call(
        functools.partial(_attn_kernel, lambda_init=lambda_init),
        grid=(bsz, nq),
        in_specs=[
            pl.BlockSpec((TQ, B_WIDTH), lambda b, i: (b * nq + i, 0)),
            kv_spec, kv_spec,
            vec(B_HEAD_DIM), vec(B_HEAD_DIM), vec(B_HEAD_DIM), vec(B_HEAD_DIM), vec(B_VDIM),
        ],
        out_specs=pl.BlockSpec((TQ, B_WIDTH), lambda b, i: (b * nq + i, 0)),
        out_shape=jax.ShapeDtypeStruct((n, B_WIDTH), BF16),
        compiler_params=pltpu.CompilerParams(
            dimension_semantics=("arbitrary", "arbitrary"), vmem_limit_bytes=VMEM_LIMIT),
        name=f"attn_l{layer}",
    )(q, k, v, lq1, lk1, lq2, lk2, sg)


def _merge_kernel(ya_ref, yb_ref, yc_ref, gate_ref, x_ref, wb_ref, wo_ref, gpost_ref,
                  gfpre_ref, wfi_ref, wfo_ref, gfpost_ref, o_ref):
    merged = None
    for j, y_ref in enumerate((ya_ref, yb_ref, yc_ref)):
        up = jnp.dot(y_ref[...], wb_ref[j], preferred_element_type=F32)
        term = gate_ref[:, j * D_MODEL:(j + 1) * D_MODEL] * up
        merged = term if merged is None else merged + term
    mix = jnp.dot(merged.astype(BF16), wo_ref[...], preferred_element_type=F32)
    x1 = x_ref[...] + _rms(mix, gpost_ref[...])

    h = _rms(x1, gfpre_ref[...]).astype(BF16)
    f = None
    for lo in range(0, D_FF, FF_CHUNK):
        width = min(FF_CHUNK, D_FF - lo)
        g = jnp.dot(h, wfi_ref[:, lo:lo + width], preferred_element_type=F32)
        u = jnp.dot(h, wfi_ref[:, D_FF + lo:D_FF + lo + width], preferred_element_type=F32)
        a = (g * jax.nn.sigmoid(g) * u).astype(BF16)
        part = jnp.dot(a, wfo_ref[lo:lo + width, :], preferred_element_type=F32)
        f = part if f is None else f + part
    o_ref[...] = x1 + _rms(f, gfpost_ref[...])


def _merge(ya, yb, yc, gates, x, wb, wo, gpost, gfpre, wfi, wfo, gfpost, layer):
    n = x.shape[0]
    tm = TM_MERGE
    row = lambda width: pl.BlockSpec((tm, width), lambda i: (i, 0))
    return pl.pallas_call(
        _merge_kernel,
        grid=(n // tm,),
        in_specs=[
            row(BRANCH_WIDTH), row(BRANCH_WIDTH), row(BRANCH_WIDTH),
            row(N_BRANCH * D_MODEL), row(D_MODEL),
            _layer_spec(layer, (N_BRANCH, BRANCH_WIDTH, D_MODEL)),
            _layer_spec(layer, (D_MODEL, D_MODEL)),
            _layer_spec(layer, (1, D_MODEL)),
            _layer_spec(layer, (1, D_MODEL)),
            _layer_spec(layer, (D_MODEL, 2 * D_FF)),
            _layer_spec(layer, (D_FF, D_MODEL)),
            _layer_spec(layer, (1, D_MODEL)),
        ],
        out_specs=row(D_MODEL),
        out_shape=jax.ShapeDtypeStruct((n, D_MODEL), F32),
        compiler_params=pltpu.CompilerParams(
            dimension_semantics=("arbitrary",), vmem_limit_bytes=VMEM_LIMIT),
        name=f"merge_l{layer}",
    )(ya, yb, yc, gates, x, wb, wo, gpost, gfpre, wfi, wfo, gfpost)


def kernel(x, norm_mix_pre, w_in, gmlp_norm_g, gmlp_norm_b, gmlp_w_s, gmlp_b_s, lambda_q1, lambda_k1,
           lambda_q2, lambda_k2, diff_subln_g, pool_w, pool_scale, w_branch, w_out, norm_mix_post,
           norm_ffn_pre, w_ffn_in, w_ffn_out, norm_ffn_post):
    bsz, s, d = x.shape
    assert (s, d) == (SEQ, D_MODEL)
    depth = w_in.shape[0]
    xf = x.reshape(bsz * s, d)
    vec = lambda a: a.reshape(depth, 1, -1)
    w_in, pool_w, w_branch, w_out, w_ffn_in, w_ffn_out = (
        a.astype(BF16) for a in (w_in, pool_w, w_branch, w_out, w_ffn_in, w_ffn_out))
    bs = jnp.broadcast_to(gmlp_b_s[:, :, :, None], (depth, A_GROUPS, CHUNK, CHUNK))
    for l in range(depth):
        lambda_init = 0.8 - 0.6 * math.exp(-0.3 * l)
        ya, q, k, v, yc, gates = _in_proj(
            xf, vec(norm_mix_pre), w_in, vec(gmlp_norm_g), vec(gmlp_norm_b), gmlp_w_s, bs,
            pool_w, vec(pool_scale), l)
        yb = _attention(q, k, v, vec(lambda_q1), vec(lambda_k1), vec(lambda_q2), vec(lambda_k2),
                        vec(diff_subln_g), lambda_init, l)
        xf = _merge(ya, yb, yc, gates, xf, w_branch, w_out, vec(norm_mix_post), vec(norm_ffn_pre),
                    w_ffn_in, w_ffn_out, vec(norm_ffn_post), l)
    return xf.reshape(bsz, s, d)
```

```python
import functools
import math

import jax
import jax.numpy as jnp
from jax import lax
from jax.experimental import pallas as pl
from jax.experimental.pallas import tpu as pltpu

F32 = jnp.float32
BF16 = jnp.bfloat16

D_MODEL = 1024
SEQ = 2048
CHUNK = 128
A_GROUPS = 4
A_WIDTH = 512
B_HEADS = 4
B_HEAD_DIM = 64
B_VDIM = 128
B_WIDTH = 512
C_WINDOWS = (2, 4, 8, 16)
C_GROUP_DIM = 128
C_WIDTH = 512
N_BRANCH = 3
BRANCH_WIDTH = 512
D_FF = 2816
EPS = 1e-6

COL_A = 0
COL_Q = 2 * A_WIDTH
COL_K = COL_Q + 512
COL_V = COL_K + 512
COL_C = COL_V + B_WIDTH
COL_G = COL_C + C_WIDTH
IN_TOTAL = COL_G + N_BRANCH * D_MODEL

HALO = 16
LANES = 128
VMEM_LIMIT = 56 * 1024 * 1024

TM_PROJ = 512
TM_MERGE = 512
TQ = 256
FF_CHUNK = 512


def _rms(x, g):
    ms = jnp.mean(x * x, axis=-1, keepdims=True)
    return x * lax.rsqrt(ms + EPS) * g


def _layer_spec(layer, shape):
    nd = len(shape)
    return pl.BlockSpec((None,) + tuple(shape), lambda *_: (layer,) + (0,) * nd,
                        pipeline_mode=pl.Buffered(1))


def _in_proj_kernel(x_ref, gpre_ref, w_ref, wvt_ref, lng_ref, lnb_ref, ws_ref, bs_ref, pw_ref, psc_ref,
                    ya_ref, q_ref, k_ref, vt_ref, yc_ref, gate_ref, cbuf):
    tm = x_ref.shape[0]
    pos = (pl.program_id(0) * tm) % SEQ
    h = _rms(x_ref[...], gpre_ref[...]).astype(BF16)

    def proj(lo, width):
        return jnp.dot(h, w_ref[:, lo:lo + width], preferred_element_type=F32)

    def gate(j):
        zg = proj(COL_G + j * D_MODEL, D_MODEL)
        gate_ref[:, j * D_MODEL:(j + 1) * D_MODEL] = jax.nn.sigmoid(zg)


    za = proj(COL_A, 2 * A_WIDTH)
    gate(0)
    ga = 0.5 * za * (1.0 + lax.erf(za * math.sqrt(0.5)))
    u = ga[:, :A_WIDTH]
    vv = ga[:, A_WIDTH:]
    mu = jnp.mean(vv, axis=-1, keepdims=True)
    dv = vv - mu
    var = jnp.mean(dv * dv, axis=-1, keepdims=True)
    vn = (dv * lax.rsqrt(var + EPS) * lng_ref[...] + lnb_ref[...]).astype(BF16)
    trow = lax.broadcasted_iota(jnp.int32, (CHUNK, CHUNK), 0)
    tcol = lax.broadcasted_iota(jnp.int32, (CHUNK, CHUNK), 1)
    for g in range(A_GROUPS):
        wsg = jnp.where(tcol <= trow, ws_ref[g], 0.0).astype(BF16)
        cs = slice(g * CHUNK, (g + 1) * CHUNK)
        for c in range(tm // CHUNK):
            rs = slice(c * CHUNK, (c + 1) * CHUNK)
            mixed = jnp.dot(wsg, vn[rs, cs], preferred_element_type=F32) + bs_ref[g]
            ya_ref[rs, cs] = (u[rs, cs] * mixed).astype(BF16)

    q_ref[...] = (proj(COL_Q, 512) * (B_HEAD_DIM ** -0.5 * math.log2(math.e))).astype(BF16)
    k_ref[...] = proj(COL_K, 512).astype(BF16)
    vt = lax.dot_general(wvt_ref[...], h, (((1,), (1,)), ((), ())),
                         preferred_element_type=F32).astype(BF16)
    for j in range(tm // TQ):
        vt_ref[j] = vt[:, j * TQ:(j + 1) * TQ]

    gate(1)

    zc = proj(COL_C, C_WIDTH)
    gate(2)

    @pl.when(pos == 0)
    def _():
        cbuf[0:HALO, :] = jnp.zeros((HALO, C_WIDTH), F32)

    cbuf[HALO:, :] = zc
    p_all = cbuf[...]
    cbuf[0:HALO, :] = p_all[tm:, :]
    s2 = p_all + pltpu.roll(p_all, 1, 0)
    s4 = s2[:, 128:] + pltpu.roll(s2[:, 128:], 2, 0)
    s8 = s4[:, 128:] + pltpu.roll(s4[:, 128:], 4, 0)
    s16 = s8[:, 128:] + pltpu.roll(s8[:, 128:], 8, 0)
    sums = (s2[HALO:, :128], s4[HALO:, :128], s8[HALO:, :128], s16[HALO:, :])
    tpos = pos + 1 + lax.broadcasted_iota(jnp.int32, (tm, C_GROUP_DIM), 0)
    for g, w in enumerate(C_WINDOWS):
        cs = slice(g * C_GROUP_DIM, (g + 1) * C_GROUP_DIM)
        inv_cnt = 1.0 / jnp.minimum(tpos, w).astype(F32)
        pooled = (sums[g] * inv_cnt - zc[:, cs]).astype(BF16)
        yc = jnp.dot(pooled, pw_ref[g], preferred_element_type=F32) * psc_ref[:, cs]
        yc_ref[:, cs] = yc.astype(BF16)


def _in_proj(x, gpre, w_in, wvt, lng, lnb, ws, bs, pw, psc, layer):
    n = x.shape[0]
    tm = TM_PROJ
    row = lambda width: pl.BlockSpec((tm, width), lambda i: (i, 0))
    out_shapes = (
        jax.ShapeDtypeStruct((n, A_WIDTH), BF16),
        jax.ShapeDtypeStruct((n, 512), BF16),
        jax.ShapeDtypeStruct((n, 512), BF16),
        jax.ShapeDtypeStruct((n // TQ, B_WIDTH, TQ), BF16),
        jax.ShapeDtypeStruct((n, C_WIDTH), BF16),
        jax.ShapeDtypeStruct((n, N_BRANCH * D_MODEL), F32),
    )
    return pl.pallas_call(
        _in_proj_kernel,
        grid=(n // tm,),
        in_specs=[
            row(D_MODEL),
            _layer_spec(layer, (1, D_MODEL)),
            _layer_spec(layer, (D_MODEL, IN_TOTAL)),
            _layer_spec(layer, (B_WIDTH, D_MODEL)),
            _layer_spec(layer, (1, A_WIDTH)),
            _layer_spec(layer, (1, A_WIDTH)),
            _layer_spec(layer, (A_GROUPS, CHUNK, CHUNK)),
            _layer_spec(layer, (A_GROUPS, CHUNK, CHUNK)),
            _layer_spec(layer, (len(C_WINDOWS), C_GROUP_DIM, C_GROUP_DIM)),
            _layer_spec(layer, (1, C_WIDTH)),
        ],
        out_specs=[row(A_WIDTH), row(512), row(512),
                   pl.BlockSpec((tm // TQ, B_WIDTH, TQ), lambda i: (i, 0, 0)), row(C_WIDTH),
                   row(N_BRANCH * D_MODEL)],
        out_shape=out_shapes,
        scratch_shapes=[pltpu.VMEM((HALO + tm, C_WIDTH), F32)],
        compiler_params=pltpu.CompilerParams(
            dimension_semantics=("arbitrary",), vmem_limit_bytes=VMEM_LIMIT),
        name=f"in_proj_l{layer}",
    )(x, gpre, w_in, wvt, lng, lnb, ws, bs, pw, psc)


SUBLANES = 8
N_STREAMS = 2 * B_HEADS


def _sublane_partial_sum(p):
    parts = [p[r:r + SUBLANES] for r in range(0, p.shape[0], SUBLANES)]
    return functools.reduce(jnp.add, parts)


def _attn_kernel(q_ref, qn_ref, k_ref, k0n_ref, vt_ref, lq1_ref, lk1_ref, lq2_ref, lk2_ref, sgb_ref,
                 o_ref, sa_scr, sb_scr, ma_scr, mb_scr, m_scr, l_scr, acc_scr, *, lambda_init):
    b = pl.program_id(0)
    i = pl.program_id(1)
    nq = pl.num_programs(1)
    tq = q_ref.shape[0]
    heads = [slice(h * B_VDIM, (h + 1) * B_VDIM) for h in range(B_HEADS)]
    lane = lax.broadcasted_iota(jnp.int32, (tq, B_VDIM), 1)
    zero = jnp.zeros((tq, B_VDIM), BF16)
    key = lax.broadcasted_iota(jnp.int32, (tq, tq), 0)
    qry = lax.broadcasted_iota(jnp.int32, (tq, tq), 1)
    nt = (((1,), (1,)), ((), ()))

    def split_components(qr):
        out = []
        for hs in heads:
            qh = qr[:, hs]
            out += [jnp.where(lane < B_HEAD_DIM, qh, zero), jnp.where(lane >= B_HEAD_DIM, qh, zero)]
        return out

    def scores(qs, kb_of, s_scr, mx_scr, visible=None):
        for h, hs in enumerate(heads):
            kb = kb_of(hs)
            for c in range(2):
                st = 2 * h + c
                s = lax.dot_general(kb, qs[st], nt, preferred_element_type=F32)
                if visible is not None:
                    s = jnp.where(visible, s, -jnp.inf)
                s_scr[st] = s
                mx_scr[st] = jnp.max(s, axis=0, keepdims=True)

    def consume(j, s_scr, mx_scr):
        for h, hs in enumerate(heads):
            vtb = vt_ref[j, hs, :]
            for c in range(2):
                st = 2 * h + c
                m_old = m_scr[st]
                m_new = jnp.maximum(m_old, mx_scr[st])
                alpha = jnp.exp2(m_old - m_new)
                p = jnp.exp2(s_scr[st] - m_new)
                l_scr[st] = alpha * l_scr[st] + _sublane_partial_sum(p)
                acc_scr[st] = alpha * acc_scr[st] + jnp.dot(vtb, p.astype(BF16),
                                                            preferred_element_type=F32)
                m_scr[st] = m_new

    qs = split_components(q_ref)
    key_tile = lambda j: (lambda hs: k_ref[pl.ds(pl.multiple_of(j * tq, tq), tq), hs])
    causal = key <= qry

    m_scr[...] = jnp.full(m_scr.shape, -jnp.inf, F32)
    l_scr[...] = jnp.zeros(l_scr.shape, F32)
    acc_scr[...] = jnp.zeros(acc_scr.shape, F32)

    @pl.when((b == 0) & (i == 0))
    def _():
        scores(qs, key_tile(0), sa_scr, ma_scr, causal)

    def pair(t, carry):
        j = 2 * t
        scores(qs, key_tile(j + 1), sb_scr, mb_scr)
        consume(j, sa_scr, ma_scr)
        scores(qs, key_tile(j + 2), sa_scr, ma_scr)
        consume(j + 1, sb_scr, mb_scr)
        return carry

    n_pairs = lax.shift_right_logical(jnp.maximum(i - 1, 0), 1)
    lax.fori_loop(0, n_pairs, pair, 0)
    done = 2 * n_pairs

    @pl.when(i == 0)
    def _():
        consume(0, sa_scr, ma_scr)

    @pl.when((i & 1) == 1)
    def _():
        scores(qs, key_tile(i), sb_scr, mb_scr, causal)
        consume(done, sa_scr, ma_scr)
        consume(i, sb_scr, mb_scr)

    @pl.when(((i & 1) == 0) & (i > 0))
    def _():
        scores(qs, key_tile(done + 1), sb_scr, mb_scr)
        consume(done, sa_scr, ma_scr)
        scores(qs, key_tile(i), sa_scr, ma_scr, causal)
        consume(done + 1, sb_scr, mb_scr)
        consume(i, sa_scr, ma_scr)

    lam = (jnp.exp(jnp.sum(lq1_ref[...] * lk1_ref[...], keepdims=True))
           - jnp.exp(jnp.sum(lq2_ref[...] * lk2_ref[...], keepdims=True)) + lambda_init)
    for h, hs in enumerate(heads):
        l0 = jnp.sum(l_scr[2 * h], axis=0, keepdims=True)
        l1 = jnp.sum(l_scr[2 * h + 1], axis=0, keepdims=True)
        ot = acc_scr[2 * h] * (1.0 / l0) - acc_scr[2 * h + 1] * (lam / l1)
        ms = jnp.mean(ot * ot, axis=0, keepdims=True)
        y = ot * lax.rsqrt(ms + EPS) * sgb_ref[...] * (1.0 - lambda_init)
        o_ref[:, hs] = y.T.astype(BF16)

    next_is_diag = i == nq - 1
    visible_next = key <= qry + jnp.where(next_is_diag, 0, tq)
    scores(split_components(qn_ref), lambda hs: k0n_ref[:, hs], sa_scr, ma_scr, visible_next)


def _attention(q, k, vt, lq1, lk1, lq2, lk2, sgb, lambda_init, layer):
    n = q.shape[0]
    bsz = n // SEQ
    nq = SEQ // TQ
    last = bsz * nq - 1
    vec = lambda width: _layer_spec(layer, (1, width))
    stream = lambda *shape: pltpu.VMEM((N_STREAMS,) + shape, F32)
    return pl.pallas_call(
        functools.partial(_attn_kernel, lambda_init=lambda_init),
        grid=(bsz, nq),
        in_specs=[
            pl.BlockSpec((TQ, B_WIDTH), lambda b, i: (b * nq + i, 0)),
            pl.BlockSpec((TQ, B_WIDTH), lambda b, i: (jnp.minimum(b * nq + i + 1, last), 0)),
            pl.BlockSpec((SEQ, B_WIDTH), lambda b, i: (b, 0)),
            pl.BlockSpec((TQ, B_WIDTH),
                         lambda b, i: (jnp.minimum(b * nq + i + 1, last) // nq * nq, 0)),
            pl.BlockSpec((nq, B_WIDTH, TQ), lambda b, i: (b, 0, 0)),
            vec(B_HEAD_DIM), vec(B_HEAD_DIM), vec(B_HEAD_DIM), vec(B_HEAD_DIM),
            _layer_spec(layer, (B_VDIM, TQ)),
        ],
        out_specs=pl.BlockSpec((TQ, B_WIDTH), lambda b, i: (b * nq + i, 0)),
        out_shape=jax.ShapeDtypeStruct((n, B_WIDTH), BF16),
        scratch_shapes=[
            stream(TQ, TQ), stream(TQ, TQ),
            stream(1, TQ), stream(1, TQ),
            stream(1, TQ),
            stream(SUBLANES, TQ),
            stream(B_VDIM, TQ),
        ],
        compiler_params=pltpu.CompilerParams(
            dimension_semantics=("arbitrary", "arbitrary"), vmem_limit_bytes=VMEM_LIMIT),
        name=f"attn_l{layer}",
    )(q, q, k, k, vt, lq1, lk1, lq2, lk2, sgb)


def _merge_kernel(ya_ref, yb_ref, yc_ref, gate_ref, x_ref, wb_ref, wo_ref, gpost_ref,
                  gfpre_ref, wfi_ref, wfo_ref, gfpost_ref, o_ref):
    merged = None
    for j, y_ref in enumerate((ya_ref, yb_ref, yc_ref)):
        up = jnp.dot(y_ref[...], wb_ref[j], preferred_element_type=F32)
        term = gate_ref[:, j * D_MODEL:(j + 1) * D_MODEL] * up
        merged = term if merged is None else merged + term
    mix = jnp.dot(merged.astype(BF16), wo_ref[...], preferred_element_type=F32)
    x1 = x_ref[...] + _rms(mix, gpost_ref[...])

    h = _rms(x1, gfpre_ref[...]).astype(BF16)
    f = None
    for lo in range(0, D_FF, FF_CHUNK):
        width = min(FF_CHUNK, D_FF - lo)
        g = jnp.dot(h, wfi_ref[:, lo:lo + width], preferred_element_type=F32)
        u = jnp.dot(h, wfi_ref[:, D_FF + lo:D_FF + lo + width], preferred_element_type=F32)
        a = (g * jax.nn.sigmoid(g) * u).astype(BF16)
        part = jnp.dot(a, wfo_ref[lo:lo + width, :], preferred_element_type=F32)
        f = part if f is None else f + part
    o_ref[...] = x1 + _rms(f, gfpost_ref[...])


def _merge(ya, yb, yc, gates, x, wb, wo, gpost, gfpre, wfi, wfo, gfpost, layer):
    n = x.shape[0]
    tm = TM_MERGE
    row = lambda width: pl.BlockSpec((tm, width), lambda i: (i, 0))
    return pl.pallas_call(
        _merge_kernel,
        grid=(n // tm,),
        in_specs=[
            row(BRANCH_WIDTH), row(BRANCH_WIDTH), row(BRANCH_WIDTH),
            row(N_BRANCH * D_MODEL), row(D_MODEL),
            _layer_spec(layer, (N_BRANCH, BRANCH_WIDTH, D_MODEL)),
            _layer_spec(layer, (D_MODEL, D_MODEL)),
            _layer_spec(layer, (1, D_MODEL)),
            _layer_spec(layer, (1, D_MODEL)),
            _layer_spec(layer, (D_MODEL, 2 * D_FF)),
            _layer_spec(layer, (D_FF, D_MODEL)),
            _layer_spec(layer, (1, D_MODEL)),
        ],
        out_specs=row(D_MODEL),
        out_shape=jax.ShapeDtypeStruct((n, D_MODEL), F32),
        compiler_params=pltpu.CompilerParams(
            dimension_semantics=("arbitrary",), vmem_limit_bytes=VMEM_LIMIT),
        name=f"merge_l{layer}",
    )(ya, yb, yc, gates, x, wb, wo, gpost, gfpre, wfi, wfo, gfpost)


def kernel(x, norm_mix_pre, w_in, gmlp_norm_g, gmlp_norm_b, gmlp_w_s, gmlp_b_s, lambda_q1, lambda_k1,
           lambda_q2, lambda_k2, diff_subln_g, pool_w, pool_scale, w_branch, w_out, norm_mix_post,
           norm_ffn_pre, w_ffn_in, w_ffn_out, norm_ffn_post):
    bsz, s, d = x.shape
    assert (s, d) == (SEQ, D_MODEL)
    depth = w_in.shape[0]
    xf = x.reshape(bsz * s, d)
    vec = lambda a: a.reshape(depth, 1, -1)
    w_in, pool_w, w_branch, w_out, w_ffn_in, w_ffn_out = (
        a.astype(BF16) for a in (w_in, pool_w, w_branch, w_out, w_ffn_in, w_ffn_out))
    bs = jnp.broadcast_to(gmlp_b_s[:, :, :, None], (depth, A_GROUPS, CHUNK, CHUNK))
    wvt = jnp.swapaxes(w_in[:, :, COL_V:COL_V + B_WIDTH], 1, 2)
    sgb = jnp.broadcast_to(diff_subln_g[:, :, None], (depth, B_VDIM, TQ))
    for l in range(depth):
        lambda_init = 0.8 - 0.6 * math.exp(-0.3 * l)
        ya, q, k, vt, yc, gates = _in_proj(
            xf, vec(norm_mix_pre), w_in, wvt, vec(gmlp_norm_g), vec(gmlp_norm_b), gmlp_w_s, bs,
            pool_w, vec(pool_scale), l)
        yb = _attention(q, k, vt, vec(lambda_q1), vec(lambda_k1), vec(lambda_q2), vec(lambda_k2),
                        sgb, lambda_init, l)
        xf = _merge(ya, yb, yc, gates, xf, w_branch, w_out, vec(norm_mix_post), vec(norm_ffn_pre),
                    w_ffn_in, w_ffn_out, vec(norm_ffn_post), l)
    return xf.reshape(bsz, s, d)
```

```python
import functools
import math

import jax
import jax.numpy as jnp
from jax import lax
from jax.experimental import pallas as pl
from jax.experimental.pallas import tpu as pltpu

F32 = jnp.float32
BF16 = jnp.bfloat16

D_MODEL = 1024
SEQ = 2048
CHUNK = 128
A_GROUPS = 4
A_WIDTH = 512
B_HEADS = 4
B_HEAD_DIM = 64
B_VDIM = 128
B_WIDTH = 512
C_WINDOWS = (2, 4, 8, 16)
C_GROUP_DIM = 128
C_WIDTH = 512
N_BRANCH = 3
BRANCH_WIDTH = 512
D_FF = 2816
EPS = 1e-6

COL_A = 0
COL_Q = 2 * A_WIDTH
COL_K = COL_Q + 512
COL_V = COL_K + 512
COL_C = COL_V + B_WIDTH
COL_G = COL_C + C_WIDTH
IN_TOTAL = COL_G + N_BRANCH * D_MODEL

HALO = 16
LANES = 128
VMEM_LIMIT = 56 * 1024 * 1024

TM_PROJ = 512
TM_MERGE = 512
TQ = 256
FF_CHUNK = 512


def _rms(x, g):
    ms = jnp.mean(x * x, axis=-1, keepdims=True)
    return x * lax.rsqrt(ms + EPS) * g


def _layer_spec(layer, shape):
    nd = len(shape)
    return pl.BlockSpec((None,) + tuple(shape), lambda *_: (layer,) + (0,) * nd,
                        pipeline_mode=pl.Buffered(1))


def _in_proj_kernel(x_ref, gpre_ref, w_ref, wvt_ref, lng_ref, lnb_ref, ws_ref, bs_ref, pw_ref, psc_ref,
                    ya_ref, q_ref, q2_ref, k_ref, k0_ref, vt_ref, yc_ref, gate_ref, cbuf):
    tm = x_ref.shape[0]
    pos = (pl.program_id(0) * tm) % SEQ
    h = _rms(x_ref[...], gpre_ref[...]).astype(BF16)

    def proj(lo, width):
        return jnp.dot(h, w_ref[:, lo:lo + width], preferred_element_type=F32)

    def gate(j):
        zg = proj(COL_G + j * D_MODEL, D_MODEL)
        gate_ref[:, j * D_MODEL:(j + 1) * D_MODEL] = jax.nn.sigmoid(zg)


    za = proj(COL_A, 2 * A_WIDTH)
    gate(0)
    ga = 0.5 * za * (1.0 + lax.erf(za * math.sqrt(0.5)))
    u = ga[:, :A_WIDTH]
    vv = ga[:, A_WIDTH:]
    mu = jnp.mean(vv, axis=-1, keepdims=True)
    dv = vv - mu
    var = jnp.mean(dv * dv, axis=-1, keepdims=True)
    vn = (dv * lax.rsqrt(var + EPS) * lng_ref[...] + lnb_ref[...]).astype(BF16)
    trow = lax.broadcasted_iota(jnp.int32, (CHUNK, CHUNK), 0)
    tcol = lax.broadcasted_iota(jnp.int32, (CHUNK, CHUNK), 1)
    for g in range(A_GROUPS):
        wsg = jnp.where(tcol <= trow, ws_ref[g], 0.0).astype(BF16)
        cs = slice(g * CHUNK, (g + 1) * CHUNK)
        for c in range(tm // CHUNK):
            rs = slice(c * CHUNK, (c + 1) * CHUNK)
            mixed = jnp.dot(wsg, vn[rs, cs], preferred_element_type=F32) + bs_ref[g]
            ya_ref[rs, cs] = (u[rs, cs] * mixed).astype(BF16)

    q = (proj(COL_Q, 512) * (B_HEAD_DIM ** -0.5 * math.log2(math.e))).astype(BF16)
    k = proj(COL_K, 512).astype(BF16)
    q_ref[...] = q
    k_ref[...] = k
    q2_ref[...] = q

    @pl.when(pos == 0)
    def _():
        k0_ref[...] = k[0:TQ, :]

    vt = lax.dot_general(wvt_ref[...], h, (((1,), (1,)), ((), ())),
                         preferred_element_type=F32).astype(BF16)
    for j in range(tm // TQ):
        vt_ref[j] = vt[:, j * TQ:(j + 1) * TQ]

    gate(1)

    zc = proj(COL_C, C_WIDTH)
    gate(2)

    @pl.when(pos == 0)
    def _():
        cbuf[0:HALO, :] = jnp.zeros((HALO, C_WIDTH), F32)

    cbuf[HALO:, :] = zc
    p_all = cbuf[...]
    cbuf[0:HALO, :] = p_all[tm:, :]
    s2 = p_all + pltpu.roll(p_all, 1, 0)
    s4 = s2[:, 128:] + pltpu.roll(s2[:, 128:], 2, 0)
    s8 = s4[:, 128:] + pltpu.roll(s4[:, 128:], 4, 0)
    s16 = s8[:, 128:] + pltpu.roll(s8[:, 128:], 8, 0)
    sums = (s2[HALO:, :128], s4[HALO:, :128], s8[HALO:, :128], s16[HALO:, :])
    tpos = pos + 1 + lax.broadcasted_iota(jnp.int32, (tm, C_GROUP_DIM), 0)
    for g, w in enumerate(C_WINDOWS):
        cs = slice(g * C_GROUP_DIM, (g + 1) * C_GROUP_DIM)
        inv_cnt = 1.0 / jnp.minimum(tpos, w).astype(F32)
        pooled = (sums[g] * inv_cnt - zc[:, cs]).astype(BF16)
        yc = jnp.dot(pooled, pw_ref[g], preferred_element_type=F32) * psc_ref[:, cs]
        yc_ref[:, cs] = yc.astype(BF16)


def _in_proj(x, gpre, w_in, wvt, lng, lnb, ws, bs, pw, psc, layer):
    n = x.shape[0]
    tm = TM_PROJ
    row = lambda width: pl.BlockSpec((tm, width), lambda i: (i, 0))
    out_shapes = (
        jax.ShapeDtypeStruct((n, A_WIDTH), BF16),
        jax.ShapeDtypeStruct((n, 512), BF16),
        jax.ShapeDtypeStruct((n, 512), BF16),
        jax.ShapeDtypeStruct((n, 512), BF16),
        jax.ShapeDtypeStruct((n // SEQ * TQ, 512), BF16),
        jax.ShapeDtypeStruct((n // TQ, B_WIDTH, TQ), BF16),
        jax.ShapeDtypeStruct((n, C_WIDTH), BF16),
        jax.ShapeDtypeStruct((n, N_BRANCH * D_MODEL), F32),
    )
    return pl.pallas_call(
        _in_proj_kernel,
        grid=(n // tm,),
        in_specs=[
            row(D_MODEL),
            _layer_spec(layer, (1, D_MODEL)),
            _layer_spec(layer, (D_MODEL, IN_TOTAL)),
            _layer_spec(layer, (B_WIDTH, D_MODEL)),
            _layer_spec(layer, (1, A_WIDTH)),
            _layer_spec(layer, (1, A_WIDTH)),
            _layer_spec(layer, (A_GROUPS, CHUNK, CHUNK)),
            _layer_spec(layer, (A_GROUPS, CHUNK, CHUNK)),
            _layer_spec(layer, (len(C_WINDOWS), C_GROUP_DIM, C_GROUP_DIM)),
            _layer_spec(layer, (1, C_WIDTH)),
        ],
        out_specs=[row(A_WIDTH), row(512), row(512), row(512),
                   pl.BlockSpec((TQ, 512), lambda i: (i * tm // SEQ, 0)),
                   pl.BlockSpec((tm // TQ, B_WIDTH, TQ), lambda i: (i, 0, 0)), row(C_WIDTH),
                   row(N_BRANCH * D_MODEL)],
        out_shape=out_shapes,
        scratch_shapes=[pltpu.VMEM((HALO + tm, C_WIDTH), F32)],
        compiler_params=pltpu.CompilerParams(
            dimension_semantics=("arbitrary",), vmem_limit_bytes=VMEM_LIMIT),
        name=f"in_proj_l{layer}",
    )(x, gpre, w_in, wvt, lng, lnb, ws, bs, pw, psc)


SUBLANES = 8
N_STREAMS = 2 * B_HEADS


def _sublane_partial_sum(p):
    parts = [p[r:r + SUBLANES] for r in range(0, p.shape[0], SUBLANES)]
    return functools.reduce(jnp.add, parts)


def _attn_kernel(q_ref, qn_ref, k_ref, k0n_ref, vt_ref, lq1_ref, lk1_ref, lq2_ref, lk2_ref, sgb_ref,
                 o_ref, sa_scr, sb_scr, ma_scr, mb_scr, m_scr, l_scr, acc_scr, *, lambda_init):
    b = pl.program_id(0)
    i = pl.program_id(1)
    nq = pl.num_programs(1)
    tq = q_ref.shape[0]
    heads = [slice(h * B_VDIM, (h + 1) * B_VDIM) for h in range(B_HEADS)]
    lane = lax.broadcasted_iota(jnp.int32, (tq, B_VDIM), 1)
    zero = jnp.zeros((tq, B_VDIM), BF16)
    key = lax.broadcasted_iota(jnp.int32, (tq, tq), 0)
    qry = lax.broadcasted_iota(jnp.int32, (tq, tq), 1)
    nt = (((1,), (1,)), ((), ()))

    def split_components(qr):
        out = []
        for hs in heads:
            qh = qr[:, hs]
            out += [jnp.where(lane < B_HEAD_DIM, qh, zero), jnp.where(lane >= B_HEAD_DIM, qh, zero)]
        return out

    def scores(qs, kb_of, s_scr, mx_scr, visible=None):
        for h, hs in enumerate(heads):
            kb = kb_of(hs)
            for c in range(2):
                st = 2 * h + c
                s = lax.dot_general(kb, qs[st], nt, preferred_element_type=F32)
                if visible is not None:
                    s = jnp.where(visible, s, -jnp.inf)
                s_scr[st] = s
                mx_scr[st] = jnp.max(s, axis=0, keepdims=True)

    def consume(j, s_scr, mx_scr):
        for h, hs in enumerate(heads):
            vtb = vt_ref[j, hs, :]
            for c in range(2):
                st = 2 * h + c
                m_old = m_scr[st]
                m_new = jnp.maximum(m_old, mx_scr[st])
                alpha = jnp.exp2(m_old - m_new)
                p = jnp.exp2(s_scr[st] - m_new)
                l_scr[st] = alpha * l_scr[st] + _sublane_partial_sum(p)
                acc_scr[st] = alpha * acc_scr[st] + jnp.dot(vtb, p.astype(BF16),
                                                            preferred_element_type=F32)
                m_scr[st] = m_new

    qs = split_components(q_ref)
    key_tile = lambda j: (lambda hs: k_ref[pl.ds(pl.multiple_of(j * tq, tq), tq), hs])
    causal = key <= qry

    m_scr[...] = jnp.full(m_scr.shape, -jnp.inf, F32)
    l_scr[...] = jnp.zeros(l_scr.shape, F32)
    acc_scr[...] = jnp.zeros(acc_scr.shape, F32)

    @pl.when((b == 0) & (i == 0))
    def _():
        scores(qs, key_tile(0), sa_scr, ma_scr, causal)

    def pair(t, carry):
        j = 2 * t
        scores(qs, key_tile(j + 1), sb_scr, mb_scr)
        consume(j, sa_scr, ma_scr)
        scores(qs, key_tile(j + 2), sa_scr, ma_scr)
        consume(j + 1, sb_scr, mb_scr)
        return carry

    n_pairs = lax.shift_right_logical(jnp.maximum(i - 1, 0), 1)
    lax.fori_loop(0, n_pairs, pair, 0)
    done = 2 * n_pairs

    @pl.when(i == 0)
    def _():
        consume(0, sa_scr, ma_scr)

    @pl.when((i & 1) == 1)
    def _():
        scores(qs, key_tile(i), sb_scr, mb_scr, causal)
        consume(done, sa_scr, ma_scr)
        consume(i, sb_scr, mb_scr)

    @pl.when(((i & 1) == 0) & (i > 0))
    def _():
        scores(qs, key_tile(done + 1), sb_scr, mb_scr)
        consume(done, sa_scr, ma_scr)
        scores(qs, key_tile(i), sa_scr, ma_scr, causal)
        consume(done + 1, sb_scr, mb_scr)
        consume(i, sa_scr, ma_scr)

    lam = (jnp.exp(jnp.sum(lq1_ref[...] * lk1_ref[...], keepdims=True))
           - jnp.exp(jnp.sum(lq2_ref[...] * lk2_ref[...], keepdims=True)) + lambda_init)
    for h, hs in enumerate(heads):
        l0 = jnp.sum(l_scr[2 * h], axis=0, keepdims=True)
        l1 = jnp.sum(l_scr[2 * h + 1], axis=0, keepdims=True)
        ot = acc_scr[2 * h] * (1.0 / l0) - acc_scr[2 * h + 1] * (lam / l1)
        ms = jnp.mean(ot * ot, axis=0, keepdims=True)
        y = ot * lax.rsqrt(ms + EPS) * sgb_ref[...] * (1.0 - lambda_init)
        o_ref[:, hs] = y.T.astype(BF16)

    next_is_diag = i == nq - 1
    visible_next = key <= qry + jnp.where(next_is_diag, 0, tq)
    scores(split_components(qn_ref), lambda hs: k0n_ref[:, hs], sa_scr, ma_scr, visible_next)


def _attention(q, q2, k, k0, vt, lq1, lk1, lq2, lk2, sgb, lambda_init, layer):
    n = q.shape[0]
    bsz = n // SEQ
    nq = SEQ // TQ
    last = bsz * nq - 1
    vec = lambda width: _layer_spec(layer, (1, width))
    stream = lambda *shape: pltpu.VMEM((N_STREAMS,) + shape, F32)
    return pl.pallas_call(
        functools.partial(_attn_kernel, lambda_init=lambda_init),
        grid=(bsz, nq),
        in_specs=[
            pl.BlockSpec((TQ, B_WIDTH), lambda b, i: (b * nq + i, 0)),
            pl.BlockSpec((TQ, B_WIDTH), lambda b, i: (jnp.minimum(b * nq + i + 1, last), 0)),
            pl.BlockSpec((SEQ, B_WIDTH), lambda b, i: (b, 0)),
            pl.BlockSpec((TQ, B_WIDTH), lambda b, i: (jnp.minimum(b * nq + i + 1, last) // nq, 0)),
            pl.BlockSpec((nq, B_WIDTH, TQ), lambda b, i: (b, 0, 0)),
            vec(B_HEAD_DIM), vec(B_HEAD_DIM), vec(B_HEAD_DIM), vec(B_HEAD_DIM),
            _layer_spec(layer, (B_VDIM, TQ)),
        ],
        out_specs=pl.BlockSpec((TQ, B_WIDTH), lambda b, i: (b * nq + i, 0)),
        out_shape=jax.ShapeDtypeStruct((n, B_WIDTH), BF16),
        scratch_shapes=[
            stream(TQ, TQ), stream(TQ, TQ),
            stream(1, TQ), stream(1, TQ),
            stream(1, TQ),
            stream(SUBLANES, TQ),
            stream(B_VDIM, TQ),
        ],
        compiler_params=pltpu.CompilerParams(
            dimension_semantics=("arbitrary", "arbitrary"), vmem_limit_bytes=VMEM_LIMIT),
        name=f"attn_l{layer}",
    )(q, q2, k, k0, vt, lq1, lk1, lq2, lk2, sgb)


def _merge_kernel(ya_ref, yb_ref, yc_ref, gate_ref, x_ref, wb_ref, wo_ref, gpost_ref,
                  gfpre_ref, wfi_ref, wfo_ref, gfpost_ref, o_ref):
    merged = None
    for j, y_ref in enumerate((ya_ref, yb_ref, yc_ref)):
        up = jnp.dot(y_ref[...], wb_ref[j], preferred_element_type=F32)
        term = gate_ref[:, j * D_MODEL:(j + 1) * D_MODEL] * up
        merged = term if merged is None else merged + term
    mix = jnp.dot(merged.astype(BF16), wo_ref[...], preferred_element_type=F32)
    x1 = x_ref[...] + _rms(mix, gpost_ref[...])

    h = _rms(x1, gfpre_ref[...]).astype(BF16)
    f = None
    for lo in range(0, D_FF, FF_CHUNK):
        width = min(FF_CHUNK, D_FF - lo)
        g = jnp.dot(h, wfi_ref[:, lo:lo + width], preferred_element_type=F32)
        u = jnp.dot(h, wfi_ref[:, D_FF + lo:D_FF + lo + width], preferred_element_type=F32)
        a = (g * jax.nn.sigmoid(g) * u).astype(BF16)
        part = jnp.dot(a, wfo_ref[lo:lo + width, :], preferred_element_type=F32)
        f = part if f is None else f + part
    o_ref[...] = x1 + _rms(f, gfpost_ref[...])


def _merge(ya, yb, yc, gates, x, wb, wo, gpost, gfpre, wfi, wfo, gfpost, layer):
    n = x.shape[0]
    tm = TM_MERGE
    row = lambda width: pl.BlockSpec((tm, width), lambda i: (i, 0))
    return pl.pallas_call(
        _merge_kernel,
        grid=(n // tm,),
        in_specs=[
            row(BRANCH_WIDTH), row(BRANCH_WIDTH), row(BRANCH_WIDTH),
            row(N_BRANCH * D_MODEL), row(D_MODEL),
            _layer_spec(layer, (N_BRANCH, BRANCH_WIDTH, D_MODEL)),
            _layer_spec(layer, (D_MODEL, D_MODEL)),
            _layer_spec(layer, (1, D_MODEL)),
            _layer_spec(layer, (1, D_MODEL)),
            _layer_spec(layer, (D_MODEL, 2 * D_FF)),
            _layer_spec(layer, (D_FF, D_MODEL)),
            _layer_spec(layer, (1, D_MODEL)),
        ],
        out_specs=row(D_MODEL),
        out_shape=jax.ShapeDtypeStruct((n, D_MODEL), F32),
        compiler_params=pltpu.CompilerParams(
            dimension_semantics=("arbitrary",), vmem_limit_bytes=VMEM_LIMIT),
        name=f"merge_l{layer}",
    )(ya, yb, yc, gates, x, wb, wo, gpost, gfpre, wfi, wfo, gfpost)


def kernel(x, norm_mix_pre, w_in, gmlp_norm_g, gmlp_norm_b, gmlp_w_s, gmlp_b_s, lambda_q1, lambda_k1,
           lambda_q2, lambda_k2, diff_subln_g, pool_w, pool_scale, w_branch, w_out, norm_mix_post,
           norm_ffn_pre, w_ffn_in, w_ffn_out, norm_ffn_post):
    bsz, s, d = x.shape
    assert (s, d) == (SEQ, D_MODEL)
    depth = w_in.shape[0]
    xf = x.reshape(bsz * s, d)
    vec = lambda a: a.reshape(depth, 1, -1)
    w_in, pool_w, w_branch, w_out, w_ffn_in, w_ffn_out = (
        a.astype(BF16) for a in (w_in, pool_w, w_branch, w_out, w_ffn_in, w_ffn_out))
    bs = jnp.broadcast_to(gmlp_b_s[:, :, :, None], (depth, A_GROUPS, CHUNK, CHUNK))
    wvt = jnp.swapaxes(w_in[:, :, COL_V:COL_V + B_WIDTH], 1, 2)
    sgb = jnp.broadcast_to(diff_subln_g[:, :, None], (depth, B_VDIM, TQ))
    for l in range(depth):
        lambda_init = 0.8 - 0.6 * math.exp(-0.3 * l)
        ya, q, q2, k, k0, vt, yc, gates = _in_proj(
            xf, vec(norm_mix_pre), w_in, wvt, vec(gmlp_norm_g), vec(gmlp_norm_b), gmlp_w_s, bs,
            pool_w, vec(pool_scale), l)
        yb = _attention(q, q2, k, k0, vt, vec(lambda_q1), vec(lambda_k1), vec(lambda_q2), vec(lambda_k2),
                        sgb, lambda_init, l)
        xf = _merge(ya, yb, yc, gates, xf, w_branch, w_out, vec(norm_mix_post), vec(norm_ffn_pre),
                    w_ffn_in, w_ffn_out, vec(norm_ffn_post), l)
    return xf.reshape(bsz, s, d)
```

```python
import functools
import math

import jax
import jax.numpy as jnp
from jax import lax
from jax.experimental import pallas as pl
from jax.experimental.pallas import tpu as pltpu

F32 = jnp.float32
BF16 = jnp.bfloat16

D_MODEL = 1024
SEQ = 2048
CHUNK = 128
A_GROUPS = 4
A_WIDTH = 512
B_HEADS = 4
B_HEAD_DIM = 64
B_VDIM = 128
B_WIDTH = 512
C_WINDOWS = (2, 4, 8, 16)
C_GROUP_DIM = 128
C_WIDTH = 512
N_BRANCH = 3
BRANCH_WIDTH = 512
D_FF = 2816
EPS = 1e-6

COL_A = 0
COL_Q = 2 * A_WIDTH
COL_K = COL_Q + 512
COL_V = COL_K + 512
COL_C = COL_V + B_WIDTH
COL_G = COL_C + C_WIDTH
IN_TOTAL = COL_G + N_BRANCH * D_MODEL

HALO = 16
LANES = 128
VMEM_LIMIT = 56 * 1024 * 1024

TM_PROJ = 512
TM_MERGE = 512
TQ = 256
FF_CHUNK = 512


def _rms(x, g):
    ms = jnp.mean(x * x, axis=-1, keepdims=True)
    return x * lax.rsqrt(ms + EPS) * g


def _layer_spec(layer, shape):
    nd = len(shape)
    return pl.BlockSpec((None,) + tuple(shape), lambda *_: (layer,) + (0,) * nd,
                        pipeline_mode=pl.Buffered(1))


def _in_proj_kernel(x_ref, gpre_ref, w_ref, lng_ref, lnb_ref, ws_ref, bs_ref, pw_ref, psc_ref,
                    ya_ref, q_ref, k_ref, vt_ref, yc_ref, gate_ref, cbuf):
    tm = x_ref.shape[0]
    pos = (pl.program_id(0) * tm) % SEQ
    h = _rms(x_ref[...], gpre_ref[...]).astype(BF16)

    def proj(lo, width):
        return jnp.dot(h, w_ref[:, lo:lo + width], preferred_element_type=F32)

    def gate(j):
        zg = proj(COL_G + j * D_MODEL, D_MODEL)
        gate_ref[:, j * D_MODEL:(j + 1) * D_MODEL] = jax.nn.sigmoid(zg)


    za = proj(COL_A, 2 * A_WIDTH)
    gate(0)
    ga = 0.5 * za * (1.0 + lax.erf(za * math.sqrt(0.5)))
    u = ga[:, :A_WIDTH]
    vv = ga[:, A_WIDTH:]
    mu = jnp.mean(vv, axis=-1, keepdims=True)
    dv = vv - mu
    var = jnp.mean(dv * dv, axis=-1, keepdims=True)
    vn = (dv * lax.rsqrt(var + EPS) * lng_ref[...] + lnb_ref[...]).astype(BF16)
    trow = lax.broadcasted_iota(jnp.int32, (CHUNK, CHUNK), 0)
    tcol = lax.broadcasted_iota(jnp.int32, (CHUNK, CHUNK), 1)
    for g in range(A_GROUPS):
        wsg = jnp.where(tcol <= trow, ws_ref[g], 0.0).astype(BF16)
        cs = slice(g * CHUNK, (g + 1) * CHUNK)
        for c in range(tm // CHUNK):
            rs = slice(c * CHUNK, (c + 1) * CHUNK)
            mixed = jnp.dot(wsg, vn[rs, cs], preferred_element_type=F32) + bs_ref[g]
            ya_ref[rs, cs] = (u[rs, cs] * mixed).astype(BF16)

    q_ref[...] = (proj(COL_Q, 512) * (B_HEAD_DIM ** -0.5 * math.log2(math.e))).astype(BF16)
    k_ref[...] = proj(COL_K, 512).astype(BF16)
    vt = proj(COL_V, B_WIDTH).T.astype(BF16)
    for j in range(tm // TQ):
        vt_ref[j] = vt[:, j * TQ:(j + 1) * TQ]

    gate(1)

    zc = proj(COL_C, C_WIDTH)
    gate(2)

    @pl.when(pos == 0)
    def _():
        cbuf[0:HALO, :] = jnp.zeros((HALO, C_WIDTH), F32)

    cbuf[HALO:, :] = zc
    p_all = cbuf[...]
    cbuf[0:HALO, :] = p_all[tm:, :]
    s2 = p_all + pltpu.roll(p_all, 1, 0)
    s4 = s2[:, 128:] + pltpu.roll(s2[:, 128:], 2, 0)
    s8 = s4[:, 128:] + pltpu.roll(s4[:, 128:], 4, 0)
    s16 = s8[:, 128:] + pltpu.roll(s8[:, 128:], 8, 0)
    sums = (s2[HALO:, :128], s4[HALO:, :128], s8[HALO:, :128], s16[HALO:, :])
    tpos = pos + 1 + lax.broadcasted_iota(jnp.int32, (tm, C_GROUP_DIM), 0)
    for g, w in enumerate(C_WINDOWS):
        cs = slice(g * C_GROUP_DIM, (g + 1) * C_GROUP_DIM)
        inv_cnt = 1.0 / jnp.minimum(tpos, w).astype(F32)
        pooled = (sums[g] * inv_cnt - zc[:, cs]).astype(BF16)
        yc = jnp.dot(pooled, pw_ref[g], preferred_element_type=F32) * psc_ref[:, cs]
        yc_ref[:, cs] = yc.astype(BF16)


def _in_proj(x, gpre, w_in, lng, lnb, ws, bs, pw, psc, layer):
    n = x.shape[0]
    tm = TM_PROJ
    row = lambda width: pl.BlockSpec((tm, width), lambda i: (i, 0))
    out_shapes = (
        jax.ShapeDtypeStruct((n, A_WIDTH), BF16),
        jax.ShapeDtypeStruct((n, 512), BF16),
        jax.ShapeDtypeStruct((n, 512), BF16),
        jax.ShapeDtypeStruct((n // TQ, B_WIDTH, TQ), BF16),
        jax.ShapeDtypeStruct((n, C_WIDTH), BF16),
        jax.ShapeDtypeStruct((n, N_BRANCH * D_MODEL), F32),
    )
    return pl.pallas_call(
        _in_proj_kernel,
        grid=(n // tm,),
        in_specs=[
            row(D_MODEL),
            _layer_spec(layer, (1, D_MODEL)),
            _layer_spec(layer, (D_MODEL, IN_TOTAL)),
            _layer_spec(layer, (1, A_WIDTH)),
            _layer_spec(layer, (1, A_WIDTH)),
            _layer_spec(layer, (A_GROUPS, CHUNK, CHUNK)),
            _layer_spec(layer, (A_GROUPS, CHUNK, CHUNK)),
            _layer_spec(layer, (len(C_WINDOWS), C_GROUP_DIM, C_GROUP_DIM)),
            _layer_spec(layer, (1, C_WIDTH)),
        ],
        out_specs=[row(A_WIDTH), row(512), row(512),
                   pl.BlockSpec((tm // TQ, B_WIDTH, TQ), lambda i: (i, 0, 0)), row(C_WIDTH),
                   row(N_BRANCH * D_MODEL)],
        out_shape=out_shapes,
        scratch_shapes=[pltpu.VMEM((HALO + tm, C_WIDTH), F32)],
        compiler_params=pltpu.CompilerParams(
            dimension_semantics=("arbitrary",), vmem_limit_bytes=VMEM_LIMIT),
        name=f"in_proj_l{layer}",
    )(x, gpre, w_in, lng, lnb, ws, bs, pw, psc)


SUBLANES = 8
N_STREAMS = 2 * B_HEADS


def _sublane_partial_sum(p):
    parts = [p[r:r + SUBLANES] for r in range(0, p.shape[0], SUBLANES)]
    return functools.reduce(jnp.add, parts)


def _attn_kernel(q_ref, qn_ref, k_ref, k0n_ref, vt_ref, lq1_ref, lk1_ref, lq2_ref, lk2_ref, sgb_ref,
                 o_ref, sa_scr, sb_scr, ma_scr, mb_scr, m_scr, l_scr, acc_scr, *, lambda_init):
    b = pl.program_id(0)
    i = pl.program_id(1)
    nq = pl.num_programs(1)
    tq = q_ref.shape[0]
    heads = [slice(h * B_VDIM, (h + 1) * B_VDIM) for h in range(B_HEADS)]
    lane = lax.broadcasted_iota(jnp.int32, (tq, B_VDIM), 1)
    zero = jnp.zeros((tq, B_VDIM), BF16)
    key = lax.broadcasted_iota(jnp.int32, (tq, tq), 0)
    qry = lax.broadcasted_iota(jnp.int32, (tq, tq), 1)
    nt = (((1,), (1,)), ((), ()))

    def split_components(qr):
        out = []
        for hs in heads:
            qh = qr[:, hs]
            out += [jnp.where(lane < B_HEAD_DIM, qh, zero), jnp.where(lane >= B_HEAD_DIM, qh, zero)]
        return out

    def scores(qs, kb_of, s_scr, mx_scr, visible=None):
        for h, hs in enumerate(heads):
            kb = kb_of(hs)
            for c in range(2):
                st = 2 * h + c
                s = lax.dot_general(kb, qs[st], nt, preferred_element_type=F32)
                if visible is not None:
                    s = jnp.where(visible, s, -jnp.inf)
                s_scr[st] = s
                mx_scr[st] = jnp.max(s, axis=0, keepdims=True)

    def consume(j, s_scr, mx_scr):
        for h, hs in enumerate(heads):
            vtb = vt_ref[j, hs, :]
            for c in range(2):
                st = 2 * h + c
                m_old = m_scr[st]
                m_new = jnp.maximum(m_old, mx_scr[st])
                alpha = jnp.exp2(m_old - m_new)
                p = jnp.exp2(s_scr[st] - m_new)
                l_scr[st] = alpha * l_scr[st] + _sublane_partial_sum(p)
                acc_scr[st] = alpha * acc_scr[st] + jnp.dot(vtb, p.astype(BF16),
                                                            preferred_element_type=F32)
                m_scr[st] = m_new

    qs = split_components(q_ref)
    key_tile = lambda j: (lambda hs: k_ref[pl.ds(pl.multiple_of(j * tq, tq), tq), hs])
    causal = key <= qry

    m_scr[...] = jnp.full(m_scr.shape, -jnp.inf, F32)
    l_scr[...] = jnp.zeros(l_scr.shape, F32)
    acc_scr[...] = jnp.zeros(acc_scr.shape, F32)

    @pl.when((b == 0) & (i == 0))
    def _():
        scores(qs, key_tile(0), sa_scr, ma_scr, causal)

    def pair(t, carry):
        j = 2 * t
        scores(qs, key_tile(j + 1), sb_scr, mb_scr)
        consume(j, sa_scr, ma_scr)
        scores(qs, key_tile(j + 2), sa_scr, ma_scr)
        consume(j + 1, sb_scr, mb_scr)
        return carry

    n_pairs = lax.shift_right_logical(jnp.maximum(i - 1, 0), 1)
    lax.fori_loop(0, n_pairs, pair, 0)
    done = 2 * n_pairs

    @pl.when(i == 0)
    def _():
        consume(0, sa_scr, ma_scr)

    @pl.when((i & 1) == 1)
    def _():
        scores(qs, key_tile(i), sb_scr, mb_scr, causal)
        consume(done, sa_scr, ma_scr)
        consume(i, sb_scr, mb_scr)

    @pl.when(((i & 1) == 0) & (i > 0))
    def _():
        scores(qs, key_tile(done + 1), sb_scr, mb_scr)
        consume(done, sa_scr, ma_scr)
        scores(qs, key_tile(i), sa_scr, ma_scr, causal)
        consume(done + 1, sb_scr, mb_scr)
        consume(i, sa_scr, ma_scr)

    lam = (jnp.exp(jnp.sum(lq1_ref[...] * lk1_ref[...], keepdims=True))
           - jnp.exp(jnp.sum(lq2_ref[...] * lk2_ref[...], keepdims=True)) + lambda_init)
    for h, hs in enumerate(heads):
        l0 = jnp.sum(l_scr[2 * h], axis=0, keepdims=True)
        l1 = jnp.sum(l_scr[2 * h + 1], axis=0, keepdims=True)
        ot = acc_scr[2 * h] * (1.0 / l0) - acc_scr[2 * h + 1] * (lam / l1)
        ms = jnp.mean(ot * ot, axis=0, keepdims=True)
        y = ot * lax.rsqrt(ms + EPS) * sgb_ref[...] * (1.0 - lambda_init)
        o_ref[:, hs] = y.T.astype(BF16)

    next_is_diag = i == nq - 1
    visible_next = key <= qry + jnp.where(next_is_diag, 0, tq)
    scores(split_components(qn_ref), lambda hs: k0n_ref[:, hs], sa_scr, ma_scr, visible_next)


def _attention(q, k, vt, lq1, lk1, lq2, lk2, sgb, lambda_init, layer):
    n = q.shape[0]
    bsz = n // SEQ
    nq = SEQ // TQ
    last = bsz * nq - 1
    vec = lambda width: _layer_spec(layer, (1, width))
    stream = lambda *shape: pltpu.VMEM((N_STREAMS,) + shape, F32)
    return pl.pallas_call(
        functools.partial(_attn_kernel, lambda_init=lambda_init),
        grid=(bsz, nq),
        in_specs=[
            pl.BlockSpec((TQ, B_WIDTH), lambda b, i: (b * nq + i, 0)),
            pl.BlockSpec((TQ, B_WIDTH), lambda b, i: (jnp.minimum(b * nq + i + 1, last), 0)),
            pl.BlockSpec((SEQ, B_WIDTH), lambda b, i: (b, 0)),
            pl.BlockSpec((TQ, B_WIDTH),
                         lambda b, i: (jnp.minimum(b * nq + i + 1, last) // nq * nq, 0)),
            pl.BlockSpec((nq, B_WIDTH, TQ), lambda b, i: (b, 0, 0)),
            vec(B_HEAD_DIM), vec(B_HEAD_DIM), vec(B_HEAD_DIM), vec(B_HEAD_DIM),
            _layer_spec(layer, (B_VDIM, TQ)),
        ],
        out_specs=pl.BlockSpec((TQ, B_WIDTH), lambda b, i: (b * nq + i, 0)),
        out_shape=jax.ShapeDtypeStruct((n, B_WIDTH), BF16),
        scratch_shapes=[
            stream(TQ, TQ), stream(TQ, TQ),
            stream(1, TQ), stream(1, TQ),
            stream(1, TQ),
            stream(SUBLANES, TQ),
            stream(B_VDIM, TQ),
        ],
        compiler_params=pltpu.CompilerParams(
            dimension_semantics=("arbitrary", "arbitrary"), vmem_limit_bytes=VMEM_LIMIT),
        name=f"attn_l{layer}",
    )(q, q, k, k, vt, lq1, lk1, lq2, lk2, sgb)


def _merge_kernel(ya_ref, yb_ref, yc_ref, gate_ref, x_ref, wb_ref, wo_ref, gpost_ref,
                  gfpre_ref, wfi_ref, wfo_ref, gfpost_ref, o_ref):
    merged = None
    for j, y_ref in enumerate((ya_ref, yb_ref, yc_ref)):
        up = jnp.dot(y_ref[...], wb_ref[j], preferred_element_type=F32)
        term = gate_ref[:, j * D_MODEL:(j + 1) * D_MODEL] * up
        merged = term if merged is None else merged + term
    mix = jnp.dot(merged.astype(BF16), wo_ref[...], preferred_element_type=F32)
    x1 = x_ref[...] + _rms(mix, gpost_ref[...])

    h = _rms(x1, gfpre_ref[...]).astype(BF16)
    f = None
    for lo in range(0, D_FF, FF_CHUNK):
        width = min(FF_CHUNK, D_FF - lo)
        g = jnp.dot(h, wfi_ref[:, lo:lo + width], preferred_element_type=F32)
        u = jnp.dot(h, wfi_ref[:, D_FF + lo:D_FF + lo + width], preferred_element_type=F32)
        a = (g * jax.nn.sigmoid(g) * u).astype(BF16)
        part = jnp.dot(a, wfo_ref[lo:lo + width, :], preferred_element_type=F32)
        f = part if f is None else f + part
    o_ref[...] = x1 + _rms(f, gfpost_ref[...])


def _merge(ya, yb, yc, gates, x, wb, wo, gpost, gfpre, wfi, wfo, gfpost, layer):
    n = x.shape[0]
    tm = TM_MERGE
    row = lambda width: pl.BlockSpec((tm, width), lambda i: (i, 0))
    return pl.pallas_call(
        _merge_kernel,
        grid=(n // tm,),
        in_specs=[
            row(BRANCH_WIDTH), row(BRANCH_WIDTH), row(BRANCH_WIDTH),
            row(N_BRANCH * D_MODEL), row(D_MODEL),
            _layer_spec(layer, (N_BRANCH, BRANCH_WIDTH, D_MODEL)),
            _layer_spec(layer, (D_MODEL, D_MODEL)),
            _layer_spec(layer, (1, D_MODEL)),
            _layer_spec(layer, (1, D_MODEL)),
            _layer_spec(layer, (D_MODEL, 2 * D_FF)),
            _layer_spec(layer, (D_FF, D_MODEL)),
            _layer_spec(layer, (1, D_MODEL)),
        ],
        out_specs=row(D_MODEL),
        out_shape=jax.ShapeDtypeStruct((n, D_MODEL), F32),
        compiler_params=pltpu.CompilerParams(
            dimension_semantics=("arbitrary",), vmem_limit_bytes=VMEM_LIMIT),
        name=f"merge_l{layer}",
    )(ya, yb, yc, gates, x, wb, wo, gpost, gfpre, wfi, wfo, gfpost)


def kernel(x, norm_mix_pre, w_in, gmlp_norm_g, gmlp_norm_b, gmlp_w_s, gmlp_b_s, lambda_q1, lambda_k1,
           lambda_q2, lambda_k2, diff_subln_g, pool_w, pool_scale, w_branch, w_out, norm_mix_post,
           norm_ffn_pre, w_ffn_in, w_ffn_out, norm_ffn_post):
    bsz, s, d = x.shape
    assert (s, d) == (SEQ, D_MODEL)
    depth = w_in.shape[0]
    xf = x.reshape(bsz * s, d)
    vec = lambda a: a.reshape(depth, 1, -1)
    w_in, pool_w, w_branch, w_out, w_ffn_in, w_ffn_out = (
        a.astype(BF16) for a in (w_in, pool_w, w_branch, w_out, w_ffn_in, w_ffn_out))
    bs = jnp.broadcast_to(gmlp_b_s[:, :, :, None], (depth, A_GROUPS, CHUNK, CHUNK))
    sgb = jnp.broadcast_to(diff_subln_g[:, :, None], (depth, B_VDIM, TQ))
    for l in range(depth):
        lambda_init = 0.8 - 0.6 * math.exp(-0.3 * l)
        ya, q, k, vt, yc, gates = _in_proj(
            xf, vec(norm_mix_pre), w_in, vec(gmlp_norm_g), vec(gmlp_norm_b), gmlp_w_s, bs,
            pool_w, vec(pool_scale), l)
        yb = _attention(q, k, vt, vec(lambda_q1), vec(lambda_k1), vec(lambda_q2), vec(lambda_k2),
                        sgb, lambda_init, l)
        xf = _merge(ya, yb, yc, gates, xf, w_branch, w_out, vec(norm_mix_post), vec(norm_ffn_pre),
                    w_ffn_in, w_ffn_out, vec(norm_ffn_post), l)
    return xf.reshape(bsz, s, d)
```

```python
import functools
import math

import jax
import jax.numpy as jnp
from jax import lax
from jax.experimental import pallas as pl
from jax.experimental.pallas import tpu as pltpu

F32 = jnp.float32
BF16 = jnp.bfloat16

D_MODEL = 1024
SEQ = 2048
CHUNK = 128
A_GROUPS = 4
A_WIDTH = 512
B_HEADS = 4
B_HEAD_DIM = 64
B_VDIM = 128
B_WIDTH = 512
C_WINDOWS = (2, 4, 8, 16)
C_GROUP_DIM = 128
C_WIDTH = 512
N_BRANCH = 3
BRANCH_WIDTH = 512
D_FF = 2816
EPS = 1e-6

COL_A = 0
COL_Q = 2 * A_WIDTH
COL_K = COL_Q + 512
COL_V = COL_K + 512
COL_C = COL_V + B_WIDTH
COL_G = COL_C + C_WIDTH
IN_TOTAL = COL_G + N_BRANCH * D_MODEL

HALO = 16
LANES = 128
VMEM_LIMIT = 56 * 1024 * 1024

TM_PROJ = 512
TM_MERGE = 512
TQ = 256
FF_CHUNK = 512


def _rms(x, g):
    ms = jnp.mean(x * x, axis=-1, keepdims=True)
    return x * lax.rsqrt(ms + EPS) * g


def _layer_spec(layer, shape):
    nd = len(shape)
    return pl.BlockSpec((None,) + tuple(shape), lambda *_: (layer,) + (0,) * nd,
                        pipeline_mode=pl.Buffered(1))


def _in_proj_kernel(x_ref, gpre_ref, w_ref, lng_ref, lnb_ref, ws_ref, bs_ref, pw_ref, psc_ref,
                    ya_ref, q_ref, k_ref, vt_ref, yc_ref, gate_ref, cbuf):
    tm = x_ref.shape[0]
    pos = (pl.program_id(0) * tm) % SEQ
    h = _rms(x_ref[...], gpre_ref[...]).astype(BF16)

    def proj(lo, width):
        return jnp.dot(h, w_ref[:, lo:lo + width].astype(BF16), preferred_element_type=F32)

    def gate(j):
        zg = proj(COL_G + j * D_MODEL, D_MODEL)
        gate_ref[:, j * D_MODEL:(j + 1) * D_MODEL] = jax.nn.sigmoid(zg)


    za = proj(COL_A, 2 * A_WIDTH)
    gate(0)
    ga = 0.5 * za * (1.0 + lax.erf(za * math.sqrt(0.5)))
    u = ga[:, :A_WIDTH]
    vv = ga[:, A_WIDTH:]
    mu = jnp.mean(vv, axis=-1, keepdims=True)
    dv = vv - mu
    var = jnp.mean(dv * dv, axis=-1, keepdims=True)
    vn = (dv * lax.rsqrt(var + EPS) * lng_ref[...] + lnb_ref[...]).astype(BF16)
    trow = lax.broadcasted_iota(jnp.int32, (CHUNK, CHUNK), 0)
    tcol = lax.broadcasted_iota(jnp.int32, (CHUNK, CHUNK), 1)
    for g in range(A_GROUPS):
        wsg = jnp.where(tcol <= trow, ws_ref[g], 0.0).astype(BF16)
        cs = slice(g * CHUNK, (g + 1) * CHUNK)
        for c in range(tm // CHUNK):
            rs = slice(c * CHUNK, (c + 1) * CHUNK)
            mixed = jnp.dot(wsg, vn[rs, cs], preferred_element_type=F32) + bs_ref[g]
            ya_ref[rs, cs] = (u[rs, cs] * mixed).astype(BF16)

    q_ref[...] = (proj(COL_Q, 512) * (B_HEAD_DIM ** -0.5 * math.log2(math.e))).astype(BF16)
    k_ref[...] = proj(COL_K, 512).astype(BF16)
    vt = proj(COL_V, B_WIDTH).T.astype(BF16)
    for j in range(tm // TQ):
        vt_ref[j] = vt[:, j * TQ:(j + 1) * TQ]

    gate(1)

    zc = proj(COL_C, C_WIDTH)
    gate(2)

    @pl.when(pos == 0)
    def _():
        cbuf[0:HALO, :] = jnp.zeros((HALO, C_WIDTH), F32)

    cbuf[HALO:, :] = zc
    p_all = cbuf[...]
    cbuf[0:HALO, :] = p_all[tm:, :]
    s2 = p_all + pltpu.roll(p_all, 1, 0)
    s4 = s2[:, 128:] + pltpu.roll(s2[:, 128:], 2, 0)
    s8 = s4[:, 128:] + pltpu.roll(s4[:, 128:], 4, 0)
    s16 = s8[:, 128:] + pltpu.roll(s8[:, 128:], 8, 0)
    sums = (s2[HALO:, :128], s4[HALO:, :128], s8[HALO:, :128], s16[HALO:, :])
    tpos = pos + 1 + lax.broadcasted_iota(jnp.int32, (tm, C_GROUP_DIM), 0)
    for g, w in enumerate(C_WINDOWS):
        cs = slice(g * C_GROUP_DIM, (g + 1) * C_GROUP_DIM)
        inv_cnt = 1.0 / jnp.minimum(tpos, w).astype(F32)
        pooled = (sums[g] * inv_cnt - zc[:, cs]).astype(BF16)
        yc = jnp.dot(pooled, pw_ref[g], preferred_element_type=F32) * psc_ref[:, cs]
        yc_ref[:, cs] = yc.astype(BF16)


def _in_proj(x, gpre, w_in, lng, lnb, ws, bs, pw, psc, layer):
    n = x.shape[0]
    tm = TM_PROJ
    row = lambda width: pl.BlockSpec((tm, width), lambda i: (i, 0))
    out_shapes = (
        jax.ShapeDtypeStruct((n, A_WIDTH), BF16),
        jax.ShapeDtypeStruct((n, 512), BF16),
        jax.ShapeDtypeStruct((n, 512), BF16),
        jax.ShapeDtypeStruct((n // TQ, B_WIDTH, TQ), BF16),
        jax.ShapeDtypeStruct((n, C_WIDTH), BF16),
        jax.ShapeDtypeStruct((n, N_BRANCH * D_MODEL), F32),
    )
    return pl.pallas_call(
        _in_proj_kernel,
        grid=(n // tm,),
        in_specs=[
            row(D_MODEL),
            _layer_spec(layer, (1, D_MODEL)),
            _layer_spec(layer, (D_MODEL, IN_TOTAL)),
            _layer_spec(layer, (1, A_WIDTH)),
            _layer_spec(layer, (1, A_WIDTH)),
            _layer_spec(layer, (A_GROUPS, CHUNK, CHUNK)),
            _layer_spec(layer, (A_GROUPS, CHUNK, CHUNK)),
            _layer_spec(layer, (len(C_WINDOWS), C_GROUP_DIM, C_GROUP_DIM)),
            _layer_spec(layer, (1, C_WIDTH)),
        ],
        out_specs=[row(A_WIDTH), row(512), row(512),
                   pl.BlockSpec((tm // TQ, B_WIDTH, TQ), lambda i: (i, 0, 0)), row(C_WIDTH),
                   row(N_BRANCH * D_MODEL)],
        out_shape=out_shapes,
        scratch_shapes=[pltpu.VMEM((HALO + tm, C_WIDTH), F32)],
        compiler_params=pltpu.CompilerParams(
            dimension_semantics=("arbitrary",), vmem_limit_bytes=VMEM_LIMIT),
        name=f"in_proj_l{layer}",
    )(x, gpre, w_in, lng, lnb, ws, bs, pw, psc)


SUBLANES = 8
N_STREAMS = 2 * B_HEADS


def _sublane_partial_sum(p):
    parts = [p[r:r + SUBLANES] for r in range(0, p.shape[0], SUBLANES)]
    return functools.reduce(jnp.add, parts)


def _attn_kernel(q_ref, qn_ref, k_ref, k0n_ref, vt_ref, lq1_ref, lk1_ref, lq2_ref, lk2_ref, sgb_ref,
                 o_ref, sa_scr, sb_scr, ma_scr, mb_scr, m_scr, l_scr, acc_scr, *, lambda_init):
    b = pl.program_id(0)
    i = pl.program_id(1)
    nq = pl.num_programs(1)
    tq = q_ref.shape[0]
    heads = [slice(h * B_VDIM, (h + 1) * B_VDIM) for h in range(B_HEADS)]
    lane = lax.broadcasted_iota(jnp.int32, (tq, B_VDIM), 1)
    zero = jnp.zeros((tq, B_VDIM), BF16)
    key = lax.broadcasted_iota(jnp.int32, (tq, tq), 0)
    qry = lax.broadcasted_iota(jnp.int32, (tq, tq), 1)
    nt = (((1,), (1,)), ((), ()))

    def split_components(qr):
        out = []
        for hs in heads:
            qh = qr[:, hs]
            out += [jnp.where(lane < B_HEAD_DIM, qh, zero), jnp.where(lane >= B_HEAD_DIM, qh, zero)]
        return out

    def scores(qs, kb_of, s_scr, mx_scr, visible=None):
        for h, hs in enumerate(heads):
            kb = kb_of(hs)
            for c in range(2):
                st = 2 * h + c
                s = lax.dot_general(kb, qs[st], nt, preferred_element_type=F32)
                if visible is not None:
                    s = jnp.where(visible, s, -jnp.inf)
                s_scr[st] = s
                mx_scr[st] = jnp.max(s, axis=0, keepdims=True)

    def consume(j, s_scr, mx_scr):
        for h, hs in enumerate(heads):
            vtb = vt_ref[j, hs, :]
            for c in range(2):
                st = 2 * h + c
                m_old = m_scr[st]
                m_new = jnp.maximum(m_old, mx_scr[st])
                alpha = jnp.exp2(m_old - m_new)
                p = jnp.exp2(s_scr[st] - m_new)
                l_scr[st] = alpha * l_scr[st] + _sublane_partial_sum(p)
                acc_scr[st] = alpha * acc_scr[st] + jnp.dot(vtb, p.astype(BF16),
                                                            preferred_element_type=F32)
                m_scr[st] = m_new

    qs = split_components(q_ref)
    key_tile = lambda j: (lambda hs: k_ref[pl.ds(pl.multiple_of(j * tq, tq), tq), hs])
    causal = key <= qry

    m_scr[...] = jnp.full(m_scr.shape, -jnp.inf, F32)
    l_scr[...] = jnp.zeros(l_scr.shape, F32)
    acc_scr[...] = jnp.zeros(acc_scr.shape, F32)

    @pl.when((b == 0) & (i == 0))
    def _():
        scores(qs, key_tile(0), sa_scr, ma_scr, causal)

    def pair(t, carry):
        j = 2 * t
        scores(qs, key_tile(j + 1), sb_scr, mb_scr)
        consume(j, sa_scr, ma_scr)
        scores(qs, key_tile(j + 2), sa_scr, ma_scr)
        consume(j + 1, sb_scr, mb_scr)
        return carry

    n_pairs = lax.shift_right_logical(jnp.maximum(i - 1, 0), 1)
    lax.fori_loop(0, n_pairs, pair, 0)
    done = 2 * n_pairs

    @pl.when(i == 0)
    def _():
        consume(0, sa_scr, ma_scr)

    @pl.when((i & 1) == 1)
    def _():
        scores(qs, key_tile(i), sb_scr, mb_scr, causal)
        consume(done, sa_scr, ma_scr)
        consume(i, sb_scr, mb_scr)

    @pl.when(((i & 1) == 0) & (i > 0))
    def _():
        scores(qs, key_tile(done + 1), sb_scr, mb_scr)
        consume(done, sa_scr, ma_scr)
        scores(qs, key_tile(i), sa_scr, ma_scr, causal)
        consume(done + 1, sb_scr, mb_scr)
        consume(i, sa_scr, ma_scr)

    lam = (jnp.exp(jnp.sum(lq1_ref[...] * lk1_ref[...], keepdims=True))
           - jnp.exp(jnp.sum(lq2_ref[...] * lk2_ref[...], keepdims=True)) + lambda_init)
    for h, hs in enumerate(heads):
        l0 = jnp.sum(l_scr[2 * h], axis=0, keepdims=True)
        l1 = jnp.sum(l_scr[2 * h + 1], axis=0, keepdims=True)
        ot = acc_scr[2 * h] * (1.0 / l0) - acc_scr[2 * h + 1] * (lam / l1)
        ms = jnp.mean(ot * ot, axis=0, keepdims=True)
        y = ot * lax.rsqrt(ms + EPS) * sgb_ref[...] * (1.0 - lambda_init)
        o_ref[:, hs] = y.T.astype(BF16)

    next_is_diag = i == nq - 1
    visible_next = key <= qry + jnp.where(next_is_diag, 0, tq)
    scores(split_components(qn_ref), lambda hs: k0n_ref[:, hs], sa_scr, ma_scr, visible_next)


def _attention(q, k, vt, lq1, lk1, lq2, lk2, sgb, lambda_init, layer):
    n = q.shape[0]
    bsz = n // SEQ
    nq = SEQ // TQ
    last = bsz * nq - 1
    vec = lambda width: _layer_spec(layer, (1, width))
    stream = lambda *shape: pltpu.VMEM((N_STREAMS,) + shape, F32)
    return pl.pallas_call(
        functools.partial(_attn_kernel, lambda_init=lambda_init),
        grid=(bsz, nq),
        in_specs=[
            pl.BlockSpec((TQ, B_WIDTH), lambda b, i: (b * nq + i, 0)),
            pl.BlockSpec((TQ, B_WIDTH), lambda b, i: (jnp.minimum(b * nq + i + 1, last), 0)),
            pl.BlockSpec((SEQ, B_WIDTH), lambda b, i: (b, 0)),
            pl.BlockSpec((TQ, B_WIDTH),
                         lambda b, i: (jnp.minimum(b * nq + i + 1, last) // nq * nq, 0)),
            pl.BlockSpec((nq, B_WIDTH, TQ), lambda b, i: (b, 0, 0)),
            vec(B_HEAD_DIM), vec(B_HEAD_DIM), vec(B_HEAD_DIM), vec(B_HEAD_DIM),
            _layer_spec(layer, (B_VDIM, TQ)),
        ],
        out_specs=pl.BlockSpec((TQ, B_WIDTH), lambda b, i: (b * nq + i, 0)),
        out_shape=jax.ShapeDtypeStruct((n, B_WIDTH), BF16),
        scratch_shapes=[
            stream(TQ, TQ), stream(TQ, TQ),
            stream(1, TQ), stream(1, TQ),
            stream(1, TQ),
            stream(SUBLANES, TQ),
            stream(B_VDIM, TQ),
        ],
        compiler_params=pltpu.CompilerParams(
            dimension_semantics=("arbitrary", "arbitrary"), vmem_limit_bytes=VMEM_LIMIT),
        name=f"attn_l{layer}",
    )(q, q, k, k, vt, lq1, lk1, lq2, lk2, sgb)


def _merge_kernel(ya_ref, yb_ref, yc_ref, gate_ref, x_ref, wb_ref, wo_ref, gpost_ref,
                  gfpre_ref, wfi_ref, wfo_ref, gfpost_ref, o_ref):
    merged = None
    for j, y_ref in enumerate((ya_ref, yb_ref, yc_ref)):
        up = jnp.dot(y_ref[...], wb_ref[j], preferred_element_type=F32)
        term = gate_ref[:, j * D_MODEL:(j + 1) * D_MODEL] * up
        merged = term if merged is None else merged + term
    mix = jnp.dot(merged.astype(BF16), wo_ref[...], preferred_element_type=F32)
    x1 = x_ref[...] + _rms(mix, gpost_ref[...])

    h = _rms(x1, gfpre_ref[...]).astype(BF16)
    f = None
    for lo in range(0, D_FF, FF_CHUNK):
        width = min(FF_CHUNK, D_FF - lo)
        g = jnp.dot(h, wfi_ref[:, lo:lo + width], preferred_element_type=F32)
        u = jnp.dot(h, wfi_ref[:, D_FF + lo:D_FF + lo + width], preferred_element_type=F32)
        a = (g * jax.nn.sigmoid(g) * u).astype(BF16)
        part = jnp.dot(a, wfo_ref[lo:lo + width, :], preferred_element_type=F32)
        f = part if f is None else f + part
    o_ref[...] = x1 + _rms(f, gfpost_ref[...])


def _merge(ya, yb, yc, gates, x, wb, wo, gpost, gfpre, wfi, wfo, gfpost, layer):
    n = x.shape[0]
    tm = TM_MERGE
    row = lambda width: pl.BlockSpec((tm, width), lambda i: (i, 0))
    return pl.pallas_call(
        _merge_kernel,
        grid=(n // tm,),
        in_specs=[
            row(BRANCH_WIDTH), row(BRANCH_WIDTH), row(BRANCH_WIDTH),
            row(N_BRANCH * D_MODEL), row(D_MODEL),
            _layer_spec(layer, (N_BRANCH, BRANCH_WIDTH, D_MODEL)),
            _layer_spec(layer, (D_MODEL, D_MODEL)),
            _layer_spec(layer, (1, D_MODEL)),
            _layer_spec(layer, (1, D_MODEL)),
            _layer_spec(layer, (D_MODEL, 2 * D_FF)),
            _layer_spec(layer, (D_FF, D_MODEL)),
            _layer_spec(layer, (1, D_MODEL)),
        ],
        out_specs=row(D_MODEL),
        out_shape=jax.ShapeDtypeStruct((n, D_MODEL), F32),
        compiler_params=pltpu.CompilerParams(
            dimension_semantics=("arbitrary",), vmem_limit_bytes=VMEM_LIMIT),
        name=f"merge_l{layer}",
    )(ya, yb, yc, gates, x, wb, wo, gpost, gfpre, wfi, wfo, gfpost)


def kernel(x, norm_mix_pre, w_in, gmlp_norm_g, gmlp_norm_b, gmlp_w_s, gmlp_b_s, lambda_q1, lambda_k1,
           lambda_q2, lambda_k2, diff_subln_g, pool_w, pool_scale, w_branch, w_out, norm_mix_post,
           norm_ffn_pre, w_ffn_in, w_ffn_out, norm_ffn_post):
    bsz, s, d = x.shape
    assert (s, d) == (SEQ, D_MODEL)
    depth = w_in.shape[0]
    xf = x.reshape(bsz * s, d)
    vec = lambda a: a.reshape(depth, 1, -1)
    pool_w, w_branch, w_out, w_ffn_in, w_ffn_out = (
        a.astype(BF16) for a in (pool_w, w_branch, w_out, w_ffn_in, w_ffn_out))
    bs = jnp.broadcast_to(gmlp_b_s[:, :, :, None], (depth, A_GROUPS, CHUNK, CHUNK))
    sgb = jnp.broadcast_to(diff_subln_g[:, :, None], (depth, B_VDIM, TQ))
    for l in range(depth):
        lambda_init = 0.8 - 0.6 * math.exp(-0.3 * l)
        ya, q, k, vt, yc, gates = _in_proj(
            xf, vec(norm_mix_pre), w_in, vec(gmlp_norm_g), vec(gmlp_norm_b), gmlp_w_s, bs,
            pool_w, vec(pool_scale), l)
        yb = _attention(q, k, vt, vec(lambda_q1), vec(lambda_k1), vec(lambda_q2), vec(lambda_k2),
                        sgb, lambda_init, l)
        xf = _merge(ya, yb, yc, gates, xf, w_branch, w_out, vec(norm_mix_post), vec(norm_ffn_pre),
                    w_ffn_in, w_ffn_out, vec(norm_ffn_post), l)
    return xf.reshape(bsz, s, d)
```

```python
import functools
import math

import jax
import jax.numpy as jnp
from jax import lax
from jax.experimental import pallas as pl
from jax.experimental.pallas import tpu as pltpu

F32 = jnp.float32
BF16 = jnp.bfloat16

D_MODEL = 1024
SEQ = 2048
CHUNK = 128
A_GROUPS = 4
A_WIDTH = 512
B_HEADS = 4
B_HEAD_DIM = 64
B_VDIM = 128
B_WIDTH = 512
C_WINDOWS = (2, 4, 8, 16)
C_GROUP_DIM = 128
C_WIDTH = 512
N_BRANCH = 3
BRANCH_WIDTH = 512
D_FF = 2816
EPS = 1e-6

COL_A = 0
COL_Q = 2 * A_WIDTH
COL_K = COL_Q + 512
COL_V = COL_K + 512
COL_C = COL_V + B_WIDTH
COL_G = COL_C + C_WIDTH
IN_TOTAL = COL_G + N_BRANCH * D_MODEL

HALO = 16
LANES = 128
VMEM_LIMIT = 56 * 1024 * 1024

TM_PROJ = 512
TM_MERGE = 512
TQ = 256
FF_CHUNK = 512


def _rms(x, g):
    ms = jnp.mean(x * x, axis=-1, keepdims=True)
    return x * lax.rsqrt(ms + EPS) * g


def _layer_spec(layer, shape):
    nd = len(shape)
    return pl.BlockSpec((None,) + tuple(shape), lambda *_: (layer,) + (0,) * nd,
                        pipeline_mode=pl.Buffered(1))


def _in_proj_kernel(x_ref, gpre_ref, w_ref, lng_ref, lnb_ref, ws_ref, bs_ref, pw_ref, psc_ref,
                    ya_ref, q_ref, k_ref, vt_ref, yc_ref, gate_ref, cbuf):
    tm = x_ref.shape[0]
    pos = (pl.program_id(0) * tm) % SEQ
    h = _rms(x_ref[...], gpre_ref[...]).astype(BF16)

    def proj(lo, width):
        return jnp.dot(h, w_ref[:, lo:lo + width].astype(BF16), preferred_element_type=F32)

    def gate(j):
        zg = proj(COL_G + j * D_MODEL, D_MODEL)
        gate_ref[:, j * D_MODEL:(j + 1) * D_MODEL] = jax.nn.sigmoid(zg)


    za = proj(COL_A, 2 * A_WIDTH)
    gate(0)
    ga = 0.5 * za * (1.0 + lax.erf(za * math.sqrt(0.5)))
    u = ga[:, :A_WIDTH]
    vv = ga[:, A_WIDTH:]
    mu = jnp.mean(vv, axis=-1, keepdims=True)
    dv = vv - mu
    var = jnp.mean(dv * dv, axis=-1, keepdims=True)
    vn = (dv * lax.rsqrt(var + EPS) * lng_ref[...] + lnb_ref[...]).astype(BF16)
    trow = lax.broadcasted_iota(jnp.int32, (CHUNK, CHUNK), 0)
    tcol = lax.broadcasted_iota(jnp.int32, (CHUNK, CHUNK), 1)
    for g in range(A_GROUPS):
        wsg = jnp.where(tcol <= trow, ws_ref[g], 0.0).astype(BF16)
        cs = slice(g * CHUNK, (g + 1) * CHUNK)
        for c in range(tm // CHUNK):
            rs = slice(c * CHUNK, (c + 1) * CHUNK)
            mixed = jnp.dot(wsg, vn[rs, cs], preferred_element_type=F32) + bs_ref[g]
            ya_ref[rs, cs] = (u[rs, cs] * mixed).astype(BF16)

    q_ref[...] = (proj(COL_Q, 512) * (B_HEAD_DIM ** -0.5 * math.log2(math.e))).astype(BF16)
    k_ref[...] = proj(COL_K, 512).astype(BF16)
    vt = proj(COL_V, B_WIDTH).T.astype(BF16)
    for j in range(tm // TQ):
        vt_ref[j] = vt[:, j * TQ:(j + 1) * TQ]

    gate(1)

    zc = proj(COL_C, C_WIDTH)
    gate(2)

    @pl.when(pos == 0)
    def _():
        cbuf[0:HALO, :] = jnp.zeros((HALO, C_WIDTH), F32)

    cbuf[HALO:, :] = zc
    p_all = cbuf[...]
    cbuf[0:HALO, :] = p_all[tm:, :]
    s2 = p_all + pltpu.roll(p_all, 1, 0)
    s4 = s2[:, 128:] + pltpu.roll(s2[:, 128:], 2, 0)
    s8 = s4[:, 128:] + pltpu.roll(s4[:, 128:], 4, 0)
    s16 = s8[:, 128:] + pltpu.roll(s8[:, 128:], 8, 0)
    sums = (s2[HALO:, :128], s4[HALO:, :128], s8[HALO:, :128], s16[HALO:, :])
    tpos = pos + 1 + lax.broadcasted_iota(jnp.int32, (tm, C_GROUP_DIM), 0)
    for g, w in enumerate(C_WINDOWS):
        cs = slice(g * C_GROUP_DIM, (g + 1) * C_GROUP_DIM)
        inv_cnt = 1.0 / jnp.minimum(tpos, w).astype(F32)
        pooled = (sums[g] * inv_cnt - zc[:, cs]).astype(BF16)
        yc = jnp.dot(pooled, pw_ref[g], preferred_element_type=F32) * psc_ref[:, cs]
        yc_ref[:, cs] = yc.astype(BF16)


def _in_proj(x, gpre, w_in, lng, lnb, ws, bs, pw, psc, layer):
    n = x.shape[0]
    tm = TM_PROJ
    row = lambda width: pl.BlockSpec((tm, width), lambda i: (i, 0))
    out_shapes = (
        jax.ShapeDtypeStruct((n, A_WIDTH), BF16),
        jax.ShapeDtypeStruct((n, 512), BF16),
        jax.ShapeDtypeStruct((n, 512), BF16),
        jax.ShapeDtypeStruct((n // TQ, B_WIDTH, TQ), BF16),
        jax.ShapeDtypeStruct((n, C_WIDTH), BF16),
        jax.ShapeDtypeStruct((n, N_BRANCH * D_MODEL), F32),
    )
    return pl.pallas_call(
        _in_proj_kernel,
        grid=(n // tm,),
        in_specs=[
            row(D_MODEL),
            _layer_spec(layer, (1, D_MODEL)),
            _layer_spec(layer, (D_MODEL, IN_TOTAL)),
            _layer_spec(layer, (1, A_WIDTH)),
            _layer_spec(layer, (1, A_WIDTH)),
            _layer_spec(layer, (A_GROUPS, CHUNK, CHUNK)),
            _layer_spec(layer, (A_GROUPS, CHUNK, CHUNK)),
            _layer_spec(layer, (len(C_WINDOWS), C_GROUP_DIM, C_GROUP_DIM)),
            _layer_spec(layer, (1, C_WIDTH)),
        ],
        out_specs=[row(A_WIDTH), row(512), row(512),
                   pl.BlockSpec((tm // TQ, B_WIDTH, TQ), lambda i: (i, 0, 0)), row(C_WIDTH),
                   row(N_BRANCH * D_MODEL)],
        out_shape=out_shapes,
        scratch_shapes=[pltpu.VMEM((HALO + tm, C_WIDTH), F32)],
        compiler_params=pltpu.CompilerParams(
            dimension_semantics=("arbitrary",), vmem_limit_bytes=VMEM_LIMIT),
        name=f"in_proj_l{layer}",
    )(x, gpre, w_in, lng, lnb, ws, bs, pw, psc)


ONES_ROWS = 16
N_STREAMS = 2 * B_HEADS


def _attn_kernel(q_ref, qn_ref, k_ref, k0n_ref, vt_ref, lq1_ref, lk1_ref, lq2_ref, lk2_ref, sgb_ref,
                 o_ref, sa_scr, sb_scr, ma_scr, mb_scr, m_scr, acc_scr, *, lambda_init):
    b = pl.program_id(0)
    i = pl.program_id(1)
    nq = pl.num_programs(1)
    tq = q_ref.shape[0]
    heads = [slice(h * B_VDIM, (h + 1) * B_VDIM) for h in range(B_HEADS)]
    lane = lax.broadcasted_iota(jnp.int32, (tq, B_VDIM), 1)
    zero = jnp.zeros((tq, B_VDIM), BF16)
    key = lax.broadcasted_iota(jnp.int32, (tq, tq), 0)
    qry = lax.broadcasted_iota(jnp.int32, (tq, tq), 1)
    nt = (((1,), (1,)), ((), ()))

    def split_components(qr):
        out = []
        for hs in heads:
            qh = qr[:, hs]
            out += [jnp.where(lane < B_HEAD_DIM, qh, zero), jnp.where(lane >= B_HEAD_DIM, qh, zero)]
        return out

    def scores(qs, kb_of, s_scr, mx_scr, visible=None):
        for h, hs in enumerate(heads):
            kb = kb_of(hs)
            for c in range(2):
                st = 2 * h + c
                s = lax.dot_general(kb, qs[st], nt, preferred_element_type=F32)
                if visible is not None:
                    s = jnp.where(visible, s, -jnp.inf)
                s_scr[st] = s
                mx_scr[st] = jnp.max(s, axis=0, keepdims=True)

    ones_rows = jnp.ones((ONES_ROWS, tq), BF16)

    def consume(j, s_scr, mx_scr):
        for h, hs in enumerate(heads):
            vtb = jnp.concatenate([vt_ref[j, hs, :], ones_rows], axis=0)
            for c in range(2):
                st = 2 * h + c
                m_old = m_scr[st]
                m_new = jnp.maximum(m_old, mx_scr[st])
                alpha = jnp.exp2(m_old - m_new)
                p = jnp.exp2(s_scr[st] - m_new)
                acc_scr[st] = alpha * acc_scr[st] + jnp.dot(vtb, p.astype(BF16),
                                                            preferred_element_type=F32)
                m_scr[st] = m_new

    qs = split_components(q_ref)
    key_tile = lambda j: (lambda hs: k_ref[pl.ds(pl.multiple_of(j * tq, tq), tq), hs])
    causal = key <= qry

    m_scr[...] = jnp.full(m_scr.shape, -jnp.inf, F32)
    acc_scr[...] = jnp.zeros(acc_scr.shape, F32)

    @pl.when((b == 0) & (i == 0))
    def _():
        scores(qs, key_tile(0), sa_scr, ma_scr, causal)

    def pair(t, carry):
        j = 2 * t
        scores(qs, key_tile(j + 1), sb_scr, mb_scr)
        consume(j, sa_scr, ma_scr)
        scores(qs, key_tile(j + 2), sa_scr, ma_scr)
        consume(j + 1, sb_scr, mb_scr)
        return carry

    n_pairs = lax.shift_right_logical(jnp.maximum(i - 1, 0), 1)
    lax.fori_loop(0, n_pairs, pair, 0)
    done = 2 * n_pairs

    @pl.when(i == 0)
    def _():
        consume(0, sa_scr, ma_scr)

    @pl.when((i & 1) == 1)
    def _():
        scores(qs, key_tile(i), sb_scr, mb_scr, causal)
        consume(done, sa_scr, ma_scr)
        consume(i, sb_scr, mb_scr)

    @pl.when(((i & 1) == 0) & (i > 0))
    def _():
        scores(qs, key_tile(done + 1), sb_scr, mb_scr)
        consume(done, sa_scr, ma_scr)
        scores(qs, key_tile(i), sa_scr, ma_scr, causal)
        consume(done + 1, sb_scr, mb_scr)
        consume(i, sa_scr, ma_scr)

    lam = (jnp.exp(jnp.sum(lq1_ref[...] * lk1_ref[...], keepdims=True))
           - jnp.exp(jnp.sum(lq2_ref[...] * lk2_ref[...], keepdims=True)) + lambda_init)
    for h, hs in enumerate(heads):
        a0, a1 = acc_scr[2 * h], acc_scr[2 * h + 1]
        l0, l1 = a0[B_VDIM:B_VDIM + 1], a1[B_VDIM:B_VDIM + 1]
        ot = a0[:B_VDIM] * (1.0 / l0) - a1[:B_VDIM] * (lam / l1)
        ms = jnp.mean(ot * ot, axis=0, keepdims=True)
        y = ot * lax.rsqrt(ms + EPS) * sgb_ref[...] * (1.0 - lambda_init)
        o_ref[:, hs] = y.T.astype(BF16)

    next_is_diag = i == nq - 1
    visible_next = key <= qry + jnp.where(next_is_diag, 0, tq)
    scores(split_components(qn_ref), lambda hs: k0n_ref[:, hs], sa_scr, ma_scr, visible_next)


def _attention(q, k, vt, lq1, lk1, lq2, lk2, sgb, lambda_init, layer):
    n = q.shape[0]
    bsz = n // SEQ
    nq = SEQ // TQ
    last = bsz * nq - 1
    vec = lambda width: _layer_spec(layer, (1, width))
    stream = lambda *shape: pltpu.VMEM((N_STREAMS,) + shape, F32)
    return pl.pallas_call(
        functools.partial(_attn_kernel, lambda_init=lambda_init),
        grid=(bsz, nq),
        in_specs=[
            pl.BlockSpec((TQ, B_WIDTH), lambda b, i: (b * nq + i, 0)),
            pl.BlockSpec((TQ, B_WIDTH), lambda b, i: (jnp.minimum(b * nq + i + 1, last), 0)),
            pl.BlockSpec((SEQ, B_WIDTH), lambda b, i: (b, 0)),
            pl.BlockSpec((TQ, B_WIDTH),
                         lambda b, i: (jnp.minimum(b * nq + i + 1, last) // nq * nq, 0)),
            pl.BlockSpec((nq, B_WIDTH, TQ), lambda b, i: (b, 0, 0)),
            vec(B_HEAD_DIM), vec(B_HEAD_DIM), vec(B_HEAD_DIM), vec(B_HEAD_DIM),
            _layer_spec(layer, (B_VDIM, TQ)),
        ],
        out_specs=pl.BlockSpec((TQ, B_WIDTH), lambda b, i: (b * nq + i, 0)),
        out_shape=jax.ShapeDtypeStruct((n, B_WIDTH), BF16),
        scratch_shapes=[
            stream(TQ, TQ), stream(TQ, TQ),
            stream(1, TQ), stream(1, TQ),
            stream(1, TQ),
            stream(B_VDIM + ONES_ROWS, TQ),
        ],
        compiler_params=pltpu.CompilerParams(
            dimension_semantics=("arbitrary", "arbitrary"), vmem_limit_bytes=VMEM_LIMIT),
        name=f"attn_l{layer}",
    )(q, q, k, k, vt, lq1, lk1, lq2, lk2, sgb)


def _merge_kernel(ya_ref, yb_ref, yc_ref, gate_ref, x_ref, wb_ref, wo_ref, gpost_ref,
                  gfpre_ref, wfi_ref, wfo_ref, gfpost_ref, o_ref):
    merged = None
    for j, y_ref in enumerate((ya_ref, yb_ref, yc_ref)):
        up = jnp.dot(y_ref[...], wb_ref[j], preferred_element_type=F32)
        term = gate_ref[:, j * D_MODEL:(j + 1) * D_MODEL] * up
        merged = term if merged is None else merged + term
    mix = jnp.dot(merged.astype(BF16), wo_ref[...], preferred_element_type=F32)
    x1 = x_ref[...] + _rms(mix, gpost_ref[...])

    h = _rms(x1, gfpre_ref[...]).astype(BF16)
    f = None
    for lo in range(0, D_FF, FF_CHUNK):
        width = min(FF_CHUNK, D_FF - lo)
        g = jnp.dot(h, wfi_ref[:, lo:lo + width], preferred_element_type=F32)
        u = jnp.dot(h, wfi_ref[:, D_FF + lo:D_FF + lo + width], preferred_element_type=F32)
        a = (g * jax.nn.sigmoid(g) * u).astype(BF16)
        part = jnp.dot(a, wfo_ref[lo:lo + width, :], preferred_element_type=F32)
        f = part if f is None else f + part
    o_ref[...] = x1 + _rms(f, gfpost_ref[...])


def _merge(ya, yb, yc, gates, x, wb, wo, gpost, gfpre, wfi, wfo, gfpost, layer):
    n = x.shape[0]
    tm = TM_MERGE
    row = lambda width: pl.BlockSpec((tm, width), lambda i: (i, 0))
    return pl.pallas_call(
        _merge_kernel,
        grid=(n // tm,),
        in_specs=[
            row(BRANCH_WIDTH), row(BRANCH_WIDTH), row(BRANCH_WIDTH),
            row(N_BRANCH * D_MODEL), row(D_MODEL),
            _layer_spec(layer, (N_BRANCH, BRANCH_WIDTH, D_MODEL)),
            _layer_spec(layer, (D_MODEL, D_MODEL)),
            _layer_spec(layer, (1, D_MODEL)),
            _layer_spec(layer, (1, D_MODEL)),
            _layer_spec(layer, (D_MODEL, 2 * D_FF)),
            _layer_spec(layer, (D_FF, D_MODEL)),
            _layer_spec(layer, (1, D_MODEL)),
        ],
        out_specs=row(D_MODEL),
        out_shape=jax.ShapeDtypeStruct((n, D_MODEL), F32),
        compiler_params=pltpu.CompilerParams(
            dimension_semantics=("arbitrary",), vmem_limit_bytes=VMEM_LIMIT),
        name=f"merge_l{layer}",
    )(ya, yb, yc, gates, x, wb, wo, gpost, gfpre, wfi, wfo, gfpost)


def kernel(x, norm_mix_pre, w_in, gmlp_norm_g, gmlp_norm_b, gmlp_w_s, gmlp_b_s, lambda_q1, lambda_k1,
           lambda_q2, lambda_k2, diff_subln_g, pool_w, pool_scale, w_branch, w_out, norm_mix_post,
           norm_ffn_pre, w_ffn_in, w_ffn_out, norm_ffn_post):
    bsz, s, d = x.shape
    assert (s, d) == (SEQ, D_MODEL)
    depth = w_in.shape[0]
    xf = x.reshape(bsz * s, d)
    vec = lambda a: a.reshape(depth, 1, -1)
    pool_w, w_branch, w_out, w_ffn_in, w_ffn_out = (
        a.astype(BF16) for a in (pool_w, w_branch, w_out, w_ffn_in, w_ffn_out))
    bs = jnp.broadcast_to(gmlp_b_s[:, :, :, None], (depth, A_GROUPS, CHUNK, CHUNK))
    sgb = jnp.broadcast_to(diff_subln_g[:, :, None], (depth, B_VDIM, TQ))
    for l in range(depth):
        lambda_init = 0.8 - 0.6 * math.exp(-0.3 * l)
        ya, q, k, vt, yc, gates = _in_proj(
            xf, vec(norm_mix_pre), w_in, vec(gmlp_norm_g), vec(gmlp_norm_b), gmlp_w_s, bs,
            pool_w, vec(pool_scale), l)
        yb = _attention(q, k, vt, vec(lambda_q1), vec(lambda_k1), vec(lambda_q2), vec(lambda_k2),
                        sgb, lambda_init, l)
        xf = _merge(ya, yb, yc, gates, xf, w_branch, w_out, vec(norm_mix_post), vec(norm_ffn_pre),
                    w_ffn_in, w_ffn_out, vec(norm_ffn_post), l)
    return xf.reshape(bsz, s, d)
```

```python
import functools
import math

import jax
import jax.numpy as jnp
from jax import lax
from jax.experimental import pallas as pl
from jax.experimental.pallas import tpu as pltpu

F32 = jnp.float32
BF16 = jnp.bfloat16

D_MODEL = 1024
SEQ = 2048
CHUNK = 128
A_GROUPS = 4
A_WIDTH = 512
B_HEADS = 4
B_HEAD_DIM = 64
B_VDIM = 128
B_WIDTH = 512
C_WINDOWS = (2, 4, 8, 16)
C_GROUP_DIM = 128
C_WIDTH = 512
N_BRANCH = 3
BRANCH_WIDTH = 512
D_FF = 2816
EPS = 1e-6

COL_A = 0
COL_Q = 2 * A_WIDTH
COL_K = COL_Q + 512
COL_V = COL_K + 512
COL_C = COL_V + B_WIDTH
COL_G = COL_C + C_WIDTH
IN_TOTAL = COL_G + N_BRANCH * D_MODEL

HALO = 16
LANES = 128
VMEM_LIMIT = 56 * 1024 * 1024

TM_PROJ = 512
TM_MERGE = 512
TQ = 256
FF_CHUNK = 512


def _rms(x, g):
    ms = jnp.mean(x * x, axis=-1, keepdims=True)
    return x * lax.rsqrt(ms + EPS) * g


def _layer_spec(layer, shape):
    nd = len(shape)
    return pl.BlockSpec((None,) + tuple(shape), lambda *_: (layer,) + (0,) * nd,
                        pipeline_mode=pl.Buffered(1))


def _in_proj_kernel(x_ref, gpre_ref, w_ref, lng_ref, lnb_ref, ws_ref, bs_ref, pw_ref, psc_ref,
                    ya_ref, q_ref, k_ref, vt_ref, yc_ref, gate_ref, cbuf):
    tm = x_ref.shape[0]
    pos = (pl.program_id(0) * tm) % SEQ
    h = _rms(x_ref[...], gpre_ref[...]).astype(BF16)

    def proj(lo, width):
        return jnp.dot(h, w_ref[:, lo:lo + width].astype(BF16), preferred_element_type=F32)

    def gate(j):
        zg = proj(COL_G + j * D_MODEL, D_MODEL)
        gate_ref[:, j * D_MODEL:(j + 1) * D_MODEL] = jax.nn.sigmoid(zg)


    zc = proj(COL_C, C_WIDTH)
    gate(0)

    @pl.when(pos == 0)
    def _():
        cbuf[0:HALO, :] = jnp.zeros((HALO, C_WIDTH), F32)

    cbuf[HALO:, :] = zc
    p_all = cbuf[...]
    cbuf[0:HALO, :] = p_all[tm:, :]
    s2 = p_all + pltpu.roll(p_all, 1, 0)
    s4 = s2[:, 128:] + pltpu.roll(s2[:, 128:], 2, 0)
    s8 = s4[:, 128:] + pltpu.roll(s4[:, 128:], 4, 0)
    s16 = s8[:, 128:] + pltpu.roll(s8[:, 128:], 8, 0)
    sums = (s2[HALO:, :128], s4[HALO:, :128], s8[HALO:, :128], s16[HALO:, :])
    head_pos = 1 + lax.broadcasted_iota(jnp.int32, (HALO, C_GROUP_DIM), 0)
    for g, w in enumerate(C_WINDOWS):
        cs = slice(g * C_GROUP_DIM, (g + 1) * C_GROUP_DIM)
        head_cnt = jnp.where(pos == 0, jnp.minimum(head_pos, w), w).astype(F32)
        mean = jnp.concatenate([sums[g][:HALO] * (1.0 / head_cnt), sums[g][HALO:] * (1.0 / w)],
                               axis=0)
        pooled = (mean - zc[:, cs]).astype(BF16)
        yc = jnp.dot(pooled, pw_ref[g], preferred_element_type=F32) * psc_ref[:, cs]
        yc_ref[:, cs] = yc.astype(BF16)

    za = proj(COL_A, 2 * A_WIDTH)
    gate(1)
    ga =0.5 * za * (1.0 + lax.erf(za * math.sqrt(0.5)))
    u = ga[:, :A_WIDTH]
    vv = ga[:, A_WIDTH:]
    mu = jnp.mean(vv, axis=-1, keepdims=True)
    dv = vv - mu
    var = jnp.mean(dv * dv, axis=-1, keepdims=True)
    vn = (dv * lax.rsqrt(var + EPS) * lng_ref[...] + lnb_ref[...]).astype(BF16)
    trow = lax.broadcasted_iota(jnp.int32, (CHUNK, CHUNK), 0)
    tcol = lax.broadcasted_iota(jnp.int32, (CHUNK, CHUNK), 1)
    for g in range(A_GROUPS):
        wsg = jnp.where(tcol <= trow, ws_ref[g], 0.0).astype(BF16)
        cs = slice(g * CHUNK, (g + 1) * CHUNK)
        for c in range(tm // CHUNK):
            rs = slice(c * CHUNK, (c + 1) * CHUNK)
            mixed = jnp.dot(wsg, vn[rs, cs], preferred_element_type=F32) + bs_ref[g]
            ya_ref[rs, cs] = (u[rs, cs] * mixed).astype(BF16)

    gate(2)

    q_ref[...] = (proj(COL_Q, 512) * (B_HEAD_DIM ** -0.5 * math.log2(math.e))).astype(BF16)
    k_ref[...] = proj(COL_K, 512).astype(BF16)
    vt = proj(COL_V, B_WIDTH).T.astype(BF16)
    for j in range(tm // TQ):
        vt_ref[j] = vt[:, j * TQ:(j + 1) * TQ]


def _in_proj(x, gpre, w_in, lng, lnb, ws, bs, pw, psc, layer):
    n = x.shape[0]
    tm = TM_PROJ
    row = lambda width: pl.BlockSpec((tm, width), lambda i: (i, 0))
    out_shapes = (
        jax.ShapeDtypeStruct((n, A_WIDTH), BF16),
        jax.ShapeDtypeStruct((n, 512), BF16),
        jax.ShapeDtypeStruct((n, 512), BF16),
        jax.ShapeDtypeStruct((n // TQ, B_WIDTH, TQ), BF16),
        jax.ShapeDtypeStruct((n, C_WIDTH), BF16),
        jax.ShapeDtypeStruct((n, N_BRANCH * D_MODEL), F32),
    )
    return pl.pallas_call(
        _in_proj_kernel,
        grid=(n // tm,),
        in_specs=[
            row(D_MODEL),
            _layer_spec(layer, (1, D_MODEL)),
            _layer_spec(layer, (D_MODEL, IN_TOTAL)),
            _layer_spec(layer, (1, A_WIDTH)),
            _layer_spec(layer, (1, A_WIDTH)),
            _layer_spec(layer, (A_GROUPS, CHUNK, CHUNK)),
            _layer_spec(layer, (A_GROUPS, CHUNK, CHUNK)),
            _layer_spec(layer, (len(C_WINDOWS), C_GROUP_DIM, C_GROUP_DIM)),
            _layer_spec(layer, (1, C_WIDTH)),
        ],
        out_specs=[row(A_WIDTH), row(512), row(512),
                   pl.BlockSpec((tm // TQ, B_WIDTH, TQ), lambda i: (i, 0, 0)), row(C_WIDTH),
                   row(N_BRANCH * D_MODEL)],
        out_shape=out_shapes,
        scratch_shapes=[pltpu.VMEM((HALO + tm, C_WIDTH), F32)],
        compiler_params=pltpu.CompilerParams(
            dimension_semantics=("arbitrary",), vmem_limit_bytes=VMEM_LIMIT),
        name=f"in_proj_l{layer}",
    )(x, gpre, w_in, lng, lnb, ws, bs, pw, psc)


ONES_ROWS = 16
N_STREAMS = 2 * B_HEADS


def _attn_kernel(q_ref, qn_ref, k_ref, k0n_ref, vt_ref, lq1_ref, lk1_ref, lq2_ref, lk2_ref, sgb_ref,
                 o_ref, sa_scr, sb_scr, ma_scr, mb_scr, m_scr, acc_scr, *, lambda_init):
    b = pl.program_id(0)
    i = pl.program_id(1)
    nq = pl.num_programs(1)
    tq = q_ref.shape[0]
    heads = [slice(h * B_VDIM, (h + 1) * B_VDIM) for h in range(B_HEADS)]
    lane = lax.broadcasted_iota(jnp.int32, (tq, B_VDIM), 1)
    zero = jnp.zeros((tq, B_VDIM), BF16)
    key = lax.broadcasted_iota(jnp.int32, (tq, tq), 0)
    qry = lax.broadcasted_iota(jnp.int32, (tq, tq), 1)
    nt = (((1,), (1,)), ((), ()))

    def split_components(qr):
        out = []
        for hs in heads:
            qh = qr[:, hs]
            out += [jnp.where(lane < B_HEAD_DIM, qh, zero), jnp.where(lane >= B_HEAD_DIM, qh, zero)]
        return out

    def scores(qs, kb_of, s_scr, mx_scr, visible=None):
        for h, hs in enumerate(heads):
            kb = kb_of(hs)
            for c in range(2):
                st = 2 * h + c
                s = lax.dot_general(kb, qs[st], nt, preferred_element_type=F32)
                if visible is not None:
                    s = jnp.where(visible, s, -jnp.inf)
                s_scr[st] = s
                mx_scr[st] = jnp.max(s, axis=0, keepdims=True)

    ones_rows = jnp.ones((ONES_ROWS, tq), BF16)

    def consume(j, s_scr, mx_scr):
        for h, hs in enumerate(heads):
            vtb = jnp.concatenate([vt_ref[j, hs, :], ones_rows], axis=0)
            for c in range(2):
                st = 2 * h + c
                m_old = m_scr[st]
                m_new = jnp.maximum(m_old, mx_scr[st])
                alpha = jnp.exp2(m_old - m_new)
                p = jnp.exp2(s_scr[st] - m_new)
                acc_scr[st] = alpha * acc_scr[st] + jnp.dot(vtb, p.astype(BF16),
                                                            preferred_element_type=F32)
                m_scr[st] = m_new

    qs = split_components(q_ref)
    key_tile = lambda j: (lambda hs: k_ref[pl.ds(pl.multiple_of(j * tq, tq), tq), hs])
    causal = key <= qry

    m_scr[...] = jnp.full(m_scr.shape, -jnp.inf, F32)
    acc_scr[...] = jnp.zeros(acc_scr.shape, F32)

    @pl.when((b == 0) & (i == 0))
    def _():
        scores(qs, key_tile(0), sa_scr, ma_scr, causal)

    def pair(t, carry):
        j = 2 * t
        scores(qs, key_tile(j + 1), sb_scr, mb_scr)
        consume(j, sa_scr, ma_scr)
        scores(qs, key_tile(j + 2), sa_scr, ma_scr)
        consume(j + 1, sb_scr, mb_scr)
        return carry

    n_pairs = lax.shift_right_logical(jnp.maximum(i - 1, 0), 1)
    lax.fori_loop(0, n_pairs, pair, 0)
    done = 2 * n_pairs

    @pl.when(i == 0)
    def _():
        consume(0, sa_scr, ma_scr)

    @pl.when((i & 1) == 1)
    def _():
        scores(qs, key_tile(i), sb_scr, mb_scr, causal)
        consume(done, sa_scr, ma_scr)
        consume(i, sb_scr, mb_scr)

    @pl.when(((i & 1) == 0) & (i > 0))
    def _():
        scores(qs, key_tile(done + 1), sb_scr, mb_scr)
        consume(done, sa_scr, ma_scr)
        scores(qs, key_tile(i), sa_scr, ma_scr, causal)
        consume(done + 1, sb_scr, mb_scr)
        consume(i, sa_scr, ma_scr)

    lam = (jnp.exp(jnp.sum(lq1_ref[...] * lk1_ref[...], keepdims=True))
           - jnp.exp(jnp.sum(lq2_ref[...] * lk2_ref[...], keepdims=True)) + lambda_init)
    for h, hs in enumerate(heads):
        a0, a1 = acc_scr[2 * h], acc_scr[2 * h + 1]
        l0, l1 = a0[B_VDIM:B_VDIM + 1], a1[B_VDIM:B_VDIM + 1]
        ot = a0[:B_VDIM] * (1.0 / l0) - a1[:B_VDIM] * (lam / l1)
        ms = jnp.mean(ot * ot, axis=0, keepdims=True)
        y = ot * lax.rsqrt(ms + EPS) * sgb_ref[...] * (1.0 - lambda_init)
        o_ref[:, hs] = y.T.astype(BF16)

    next_is_diag = i == nq - 1
    visible_next = key <= qry + jnp.where(next_is_diag, 0, tq)
    scores(split_components(qn_ref), lambda hs: k0n_ref[:, hs], sa_scr, ma_scr, visible_next)


def _attention(q, k, vt, lq1, lk1, lq2, lk2, sgb, lambda_init, layer):
    n = q.shape[0]
    bsz = n // SEQ
    nq = SEQ // TQ
    last = bsz * nq - 1
    vec = lambda width: _layer_spec(layer, (1, width))
    stream = lambda *shape: pltpu.VMEM((N_STREAMS,) + shape, F32)
    return pl.pallas_call(
        functools.partial(_attn_kernel, lambda_init=lambda_init),
        grid=(bsz, nq),
        in_specs=[
            pl.BlockSpec((TQ, B_WIDTH), lambda b, i: (b * nq + i, 0)),
            pl.BlockSpec((TQ, B_WIDTH), lambda b, i: (jnp.minimum(b * nq + i + 1, last), 0)),
            pl.BlockSpec((SEQ, B_WIDTH), lambda b, i: (b, 0)),
            pl.BlockSpec((TQ, B_WIDTH),
                         lambda b, i: (jnp.minimum(b * nq + i + 1, last) // nq * nq, 0)),
            pl.BlockSpec((nq, B_WIDTH, TQ), lambda b, i: (b, 0, 0)),
            vec(B_HEAD_DIM), vec(B_HEAD_DIM), vec(B_HEAD_DIM), vec(B_HEAD_DIM),
            _layer_spec(layer, (B_VDIM, TQ)),
        ],
        out_specs=pl.BlockSpec((TQ, B_WIDTH), lambda b, i: (b * nq + i, 0)),
        out_shape=jax.ShapeDtypeStruct((n, B_WIDTH), BF16),
        scratch_shapes=[
            stream(TQ, TQ), stream(TQ, TQ),
            stream(1, TQ), stream(1, TQ),
            stream(1, TQ),
            stream(B_VDIM + ONES_ROWS, TQ),
        ],
        compiler_params=pltpu.CompilerParams(
            dimension_semantics=("arbitrary", "arbitrary"), vmem_limit_bytes=VMEM_LIMIT),
        name=f"attn_l{layer}",
    )(q, q, k, k, vt, lq1, lk1, lq2, lk2, sgb)


def _merge_kernel(ya_ref, yb_ref, yc_ref, gate_ref, x_ref, wb_ref, wo_ref, gpost_ref,
                  gfpre_ref, wfi_ref, wfo_ref, gfpost_ref, o_ref):
    merged = None
    for j, y_ref in enumerate((ya_ref, yb_ref, yc_ref)):
        up = jnp.dot(y_ref[...], wb_ref[j], preferred_element_type=F32)
        term = gate_ref[:, j * D_MODEL:(j + 1) * D_MODEL] * up
        merged = term if merged is None else merged + term
    mix = jnp.dot(merged.astype(BF16), wo_ref[...], preferred_element_type=F32)
    x1 = x_ref[...] + _rms(mix, gpost_ref[...])

    h = _rms(x1, gfpre_ref[...]).astype(BF16)
    f = None
    for lo in range(0, D_FF, FF_CHUNK):
        width = min(FF_CHUNK, D_FF - lo)
        g = jnp.dot(h, wfi_ref[:, lo:lo + width], preferred_element_type=F32)
        u = jnp.dot(h, wfi_ref[:, D_FF + lo:D_FF + lo + width], preferred_element_type=F32)
        a = (g * jax.nn.sigmoid(g) * u).astype(BF16)
        part = jnp.dot(a, wfo_ref[lo:lo + width, :], preferred_element_type=F32)
        f = part if f is None else f + part
    o_ref[...] = x1 + _rms(f, gfpost_ref[...])


def _merge(ya, yb, yc, gates, x, wb, wo, gpost, gfpre, wfi, wfo, gfpost, layer):
    n = x.shape[0]
    tm = TM_MERGE
    row = lambda width: pl.BlockSpec((tm, width), lambda i: (i, 0))
    return pl.pallas_call(
        _merge_kernel,
        grid=(n // tm,),
        in_specs=[
            row(BRANCH_WIDTH), row(BRANCH_WIDTH), row(BRANCH_WIDTH),
            row(N_BRANCH * D_MODEL), row(D_MODEL),
            _layer_spec(layer, (N_BRANCH, BRANCH_WIDTH, D_MODEL)),
            _layer_spec(layer, (D_MODEL, D_MODEL)),
            _layer_spec(layer, (1, D_MODEL)),
            _layer_spec(layer, (1, D_MODEL)),
            _layer_spec(layer, (D_MODEL, 2 * D_FF)),
            _layer_spec(layer, (D_FF, D_MODEL)),
            _layer_spec(layer, (1, D_MODEL)),
        ],
        out_specs=row(D_MODEL),
        out_shape=jax.ShapeDtypeStruct((n, D_MODEL), F32),
        compiler_params=pltpu.CompilerParams(
            dimension_semantics=("arbitrary",), vmem_limit_bytes=VMEM_LIMIT),
        name=f"merge_l{layer}",
    )(ya, yb, yc, gates, x, wb, wo, gpost, gfpre, wfi, wfo, gfpost)


def kernel(x, norm_mix_pre, w_in, gmlp_norm_g, gmlp_norm_b, gmlp_w_s, gmlp_b_s, lambda_q1, lambda_k1,
           lambda_q2, lambda_k2, diff_subln_g, pool_w, pool_scale, w_branch, w_out, norm_mix_post,
           norm_ffn_pre, w_ffn_in, w_ffn_out, norm_ffn_post):
    bsz, s, d = x.shape
    assert (s, d) == (SEQ, D_MODEL)
    depth = w_in.shape[0]
    xf = x.reshape(bsz * s, d)
    vec = lambda a: a.reshape(depth, 1, -1)
    pool_w, w_branch, w_out, w_ffn_in, w_ffn_out = (
        a.astype(BF16) for a in (pool_w, w_branch, w_out, w_ffn_in, w_ffn_out))
    bs = jnp.broadcast_to(gmlp_b_s[:, :, :, None], (depth, A_GROUPS, CHUNK, CHUNK))
    sgb = jnp.broadcast_to(diff_subln_g[:, :, None], (depth, B_VDIM, TQ))
    for l in range(depth):
        lambda_init = 0.8 - 0.6 * math.exp(-0.3 * l)
        ya, q, k, vt, yc, gates = _in_proj(
            xf, vec(norm_mix_pre), w_in, vec(gmlp_norm_g), vec(gmlp_norm_b), gmlp_w_s, bs,
            pool_w, vec(pool_scale), l)
        yb = _attention(q, k, vt, vec(lambda_q1), vec(lambda_k1), vec(lambda_q2), vec(lambda_k2),
                        sgb, lambda_init, l)
        xf = _merge(ya, yb, yc, gates, xf, w_branch, w_out, vec(norm_mix_post), vec(norm_ffn_pre),
                    w_ffn_in, w_ffn_out, vec(norm_ffn_post), l)
    return xf.reshape(bsz, s, d)
```

```python
import functools
import math

import jax
import jax.numpy as jnp
from jax import lax
from jax.experimental import pallas as pl
from jax.experimental.pallas import tpu as pltpu

F32 = jnp.float32
BF16 = jnp.bfloat16

D_MODEL = 1024
SEQ = 2048
CHUNK = 128
A_GROUPS = 4
A_WIDTH = 512
B_HEADS = 4
B_HEAD_DIM = 64
B_VDIM = 128
B_WIDTH = 512
C_WINDOWS = (2, 4, 8, 16)
C_GROUP_DIM = 128
C_WIDTH = 512
N_BRANCH = 3
BRANCH_WIDTH = 512
D_FF = 2816
EPS = 1e-6

COL_A = 0
COL_Q = 2 * A_WIDTH
COL_K = COL_Q + 512
COL_V = COL_K + 512
COL_C = COL_V + B_WIDTH
COL_G = COL_C + C_WIDTH
IN_TOTAL = COL_G + N_BRANCH * D_MODEL

HALO = 16
LANES = 128
VMEM_LIMIT = 56 * 1024 * 1024

TM_PROJ = 512
TM_MERGE = 512
TQ = 256
FF_CHUNK = 512


def _rms(x, g):
    ms = jnp.mean(x * x, axis=-1, keepdims=True)
    return x * lax.rsqrt(ms + EPS) * g


def _layer_spec(layer, shape):
    nd = len(shape)
    return pl.BlockSpec((None,) + tuple(shape), lambda *_: (layer,) + (0,) * nd,
                        pipeline_mode=pl.Buffered(1))


def _in_proj_kernel(x_ref, gpre_ref, w_ref, lng_ref, lnb_ref, ws_ref, bs_ref, pw_ref, psc_ref,
                    ya_ref, q_ref, k_ref, vt_ref, yc_ref, gate_ref, cbuf):
    tm = x_ref.shape[0]
    pos = (pl.program_id(0) * tm) % SEQ
    h = _rms(x_ref[...], gpre_ref[...]).astype(BF16)

    def proj(lo, width):
        return jnp.dot(h, w_ref[:, lo:lo + width].astype(BF16), preferred_element_type=F32)

    def gate(j):
        zg = proj(COL_G + j * D_MODEL, D_MODEL)
        gate_ref[:, j * D_MODEL:(j + 1) * D_MODEL] = jax.nn.sigmoid(zg)


    zc = proj(COL_C, C_WIDTH)
    gate(0)

    @pl.when(pos == 0)
    def _():
        cbuf[0:HALO, :] = jnp.zeros((HALO, C_WIDTH), F32)

    cbuf[HALO:, :] = zc
    p_all = cbuf[...]
    cbuf[0:HALO, :] = p_all[tm:, :]
    s2 = p_all + pltpu.roll(p_all, 1, 0)
    s4 = s2[:, 128:] + pltpu.roll(s2[:, 128:], 2, 0)
    s8 = s4[:, 128:] + pltpu.roll(s4[:, 128:], 4, 0)
    s16 = s8[:, 128:] + pltpu.roll(s8[:, 128:], 8, 0)
    sums = (s2[HALO:, :128], s4[HALO:, :128], s8[HALO:, :128], s16[HALO:, :])
    head_pos = 1 + lax.broadcasted_iota(jnp.int32, (HALO, C_GROUP_DIM), 0)
    for g, w in enumerate(C_WINDOWS):
        cs = slice(g * C_GROUP_DIM, (g + 1) * C_GROUP_DIM)
        head_cnt = jnp.where(pos == 0, jnp.minimum(head_pos, w), w).astype(F32)
        mean = jnp.concatenate([sums[g][:HALO] * (1.0 / head_cnt), sums[g][HALO:] * (1.0 / w)],
                               axis=0)
        pooled = (mean - zc[:, cs]).astype(BF16)
        yc = jnp.dot(pooled, pw_ref[g], preferred_element_type=F32) * psc_ref[:, cs]
        yc_ref[:, cs] = yc.astype(BF16)

    za = proj(COL_A, 2 * A_WIDTH)
    gate(1)
    ga =0.5 * za * (1.0 + lax.erf(za * math.sqrt(0.5)))
    u = ga[:, :A_WIDTH]
    vv = ga[:, A_WIDTH:]
    mu = jnp.mean(vv, axis=-1, keepdims=True)
    dv = vv - mu
    var = jnp.mean(dv * dv, axis=-1, keepdims=True)
    vn = (dv * lax.rsqrt(var + EPS) * lng_ref[...] + lnb_ref[...]).astype(BF16)
    trow = lax.broadcasted_iota(jnp.int32, (CHUNK, CHUNK), 0)
    tcol = lax.broadcasted_iota(jnp.int32, (CHUNK, CHUNK), 1)
    for g in range(A_GROUPS):
        wsg = jnp.where(tcol <= trow, ws_ref[g], 0.0).astype(BF16)
        cs = slice(g * CHUNK, (g + 1) * CHUNK)
        for c in range(tm // CHUNK):
            rs = slice(c * CHUNK, (c + 1) * CHUNK)
            mixed = jnp.dot(wsg, vn[rs, cs], preferred_element_type=F32) + bs_ref[g]
            ya_ref[rs, cs] = (u[rs, cs] * mixed).astype(BF16)

    gate(2)

    q_ref[...] = (proj(COL_Q, 512) * (B_HEAD_DIM ** -0.5 * math.log2(math.e))).astype(BF16)
    k_ref[...] = proj(COL_K, 512).astype(BF16)
    vt = proj(COL_V, B_WIDTH).T.astype(BF16)
    for j in range(tm // TQ):
        vt_ref[j] = vt[:, j * TQ:(j + 1) * TQ]


def _in_proj(x, gpre, w_in, lng, lnb, ws, bs, pw, psc, layer):
    n = x.shape[0]
    tm = TM_PROJ
    row = lambda width: pl.BlockSpec((tm, width), lambda i: (i, 0))
    out_shapes = (
        jax.ShapeDtypeStruct((n, A_WIDTH), BF16),
        jax.ShapeDtypeStruct((n, 512), BF16),
        jax.ShapeDtypeStruct((n, 512), BF16),
        jax.ShapeDtypeStruct((n // TQ, B_WIDTH, TQ), BF16),
        jax.ShapeDtypeStruct((n, C_WIDTH), BF16),
        jax.ShapeDtypeStruct((n, N_BRANCH * D_MODEL), F32),
    )
    return pl.pallas_call(
        _in_proj_kernel,
        grid=(n // tm,),
        in_specs=[
            row(D_MODEL),
            _layer_spec(layer, (1, D_MODEL)),
            _layer_spec(layer, (D_MODEL, IN_TOTAL)),
            _layer_spec(layer, (1, A_WIDTH)),
            _layer_spec(layer, (1, A_WIDTH)),
            _layer_spec(layer, (A_GROUPS, CHUNK, CHUNK)),
            _layer_spec(layer, (A_GROUPS, CHUNK, CHUNK)),
            _layer_spec(layer, (len(C_WINDOWS), C_GROUP_DIM, C_GROUP_DIM)),
            _layer_spec(layer, (1, C_WIDTH)),
        ],
        out_specs=[row(A_WIDTH), row(512), row(512),
                   pl.BlockSpec((tm // TQ, B_WIDTH, TQ), lambda i: (i, 0, 0)), row(C_WIDTH),
                   row(N_BRANCH * D_MODEL)],
        out_shape=out_shapes,
        scratch_shapes=[pltpu.VMEM((HALO + tm, C_WIDTH), F32)],
        compiler_params=pltpu.CompilerParams(
            dimension_semantics=("arbitrary",), vmem_limit_bytes=VMEM_LIMIT),
        name=f"in_proj_l{layer}",
    )(x, gpre, w_in, lng, lnb, ws, bs, pw, psc)


ONES_ROWS = 16
N_STREAMS = 2 * B_HEADS


def _attn_kernel(q_ref, qn_ref, k_ref, k0n_ref, vt_ref, lq1_ref, lk1_ref, lq2_ref, lk2_ref, sgb_ref,
                 o_ref, sa_scr, sb_scr, ma_scr, mb_scr, m_scr, acc_scr, *, lambda_init):
    b = pl.program_id(0)
    i = pl.program_id(1)
    nq = pl.num_programs(1)
    tq = q_ref.shape[0]
    heads = [slice(h * B_VDIM, (h + 1) * B_VDIM) for h in range(B_HEADS)]
    lane = lax.broadcasted_iota(jnp.int32, (tq, B_VDIM), 1)
    zero = jnp.zeros((tq, B_VDIM), BF16)
    key = lax.broadcasted_iota(jnp.int32, (tq, tq), 0)
    qry = lax.broadcasted_iota(jnp.int32, (tq, tq), 1)
    nt = (((1,), (1,)), ((), ()))

    def split_components(qr):
        out = []
        for hs in heads:
            qh = qr[:, hs]
            out += [jnp.where(lane < B_HEAD_DIM, qh, zero), jnp.where(lane >= B_HEAD_DIM, qh, zero)]
        return out

    def scores(qs, kb_of, s_scr, mx_scr, visible=None):
        for h, hs in enumerate(heads):
            kb = kb_of(hs)
            for c in range(2):
                st = 2 * h + c
                s = lax.dot_general(kb, qs[st], nt, preferred_element_type=F32)
                if visible is not None:
                    s = jnp.where(visible, s, -jnp.inf)
                s_scr[st] = s
                mx_scr[st] = jnp.max(s, axis=0, keepdims=True)

    ones_rows = jnp.ones((ONES_ROWS, tq), BF16)

    def consume(j, s_scr, mx_scr, after_head=None, visible=None):
        for h, hs in enumerate(heads):
            vtb = jnp.concatenate([vt_ref[j, hs, :], ones_rows], axis=0)
            for c in range(2):
                st = 2 * h + c
                s = s_scr[st]
                if visible is None:
                    mx = mx_scr[st]
                else:
                    s = jnp.where(visible, s, -jnp.inf)
                    mx = jnp.max(s, axis=0, keepdims=True)
                m_old = m_scr[st]
                m_new = jnp.maximum(m_old, mx)
                alpha = jnp.exp2(m_old - m_new)
                p = jnp.exp2(s - m_new)
                acc_scr[st] = alpha * acc_scr[st] + jnp.dot(vtb, p.astype(BF16),
                                                            preferred_element_type=F32)
                m_scr[st] = m_new
            if after_head is not None:
                after_head(h, hs)

    qs = split_components(q_ref)
    key_tile = lambda j: (lambda hs: k_ref[pl.ds(pl.multiple_of(j * tq, tq), tq), hs])
    causal = key <= qry

    m_scr[...] = jnp.full(m_scr.shape, -jnp.inf, F32)
    acc_scr[...] = jnp.zeros(acc_scr.shape, F32)

    @pl.when((b == 0) & (i == 0))
    def _():
        scores(qs, key_tile(0), sa_scr, ma_scr)

    def pair(t, carry):
        j = 2 * t
        scores(qs, key_tile(j + 1), sb_scr, mb_scr)
        consume(j, sa_scr, ma_scr)
        scores(qs, key_tile(j + 2), sa_scr, ma_scr)
        consume(j + 1, sb_scr, mb_scr)
        return carry

    n_pairs = lax.shift_right_logical(jnp.maximum(i - 1, 0), 1)
    lax.fori_loop(0, n_pairs, pair, 0)
    done = 2 * n_pairs

    lam = (jnp.exp(jnp.sum(lq1_ref[...] * lk1_ref[...], keepdims=True))
           - jnp.exp(jnp.sum(lq2_ref[...] * lk2_ref[...], keepdims=True)) + lambda_init)

    def finish_head(h, hs):
        a0, a1 = acc_scr[2 * h], acc_scr[2 * h + 1]
        l0, l1 = a0[B_VDIM:B_VDIM + 1], a1[B_VDIM:B_VDIM + 1]
        ot = a0[:B_VDIM] * (1.0 / l0) - a1[:B_VDIM] * (lam / l1)
        ms = jnp.mean(ot * ot, axis=0, keepdims=True)
        y = ot * lax.rsqrt(ms + EPS) * sgb_ref[...] * (1.0 - lambda_init)
        o_ref[:, hs] = y.T.astype(BF16)

    def stage_next():
        scores(split_components(qn_ref), lambda hs: k0n_ref[:, hs], sa_scr, ma_scr)

    @pl.when(i == 0)
    def _():
        consume(0, sa_scr, ma_scr, finish_head, causal)
        stage_next()

    @pl.when((i & 1) == 1)
    def _():
        scores(qs, key_tile(i), sb_scr, mb_scr, causal)
        consume(done, sa_scr, ma_scr)
        consume(i, sb_scr, mb_scr, finish_head)
        stage_next()

    @pl.when(((i & 1) == 0) & (i > 0))
    def _():
        scores(qs, key_tile(done + 1), sb_scr, mb_scr)
        consume(done, sa_scr, ma_scr)
        scores(qs, key_tile(i), sa_scr, ma_scr, causal)
        consume(done + 1, sb_scr, mb_scr)
        consume(i, sa_scr, ma_scr, finish_head)
        stage_next()


def _attention(q, k, vt, lq1, lk1, lq2, lk2, sgb, lambda_init, layer):
    n = q.shape[0]
    bsz = n // SEQ
    nq = SEQ // TQ
    last = bsz * nq - 1
    vec = lambda width: _layer_spec(layer, (1, width))
    stream = lambda *shape: pltpu.VMEM((N_STREAMS,) + shape, F32)
    return pl.pallas_call(
        functools.partial(_attn_kernel, lambda_init=lambda_init),
        grid=(bsz, nq),
        in_specs=[
            pl.BlockSpec((TQ, B_WIDTH), lambda b, i: (b * nq + i, 0)),
            pl.BlockSpec((TQ, B_WIDTH), lambda b, i: (jnp.minimum(b * nq + i + 1, last), 0)),
            pl.BlockSpec((SEQ, B_WIDTH), lambda b, i: (b, 0)),
            pl.BlockSpec((TQ, B_WIDTH),
                         lambda b, i: (jnp.minimum(b * nq + i + 1, last) // nq * nq, 0)),
            pl.BlockSpec((nq, B_WIDTH, TQ), lambda b, i: (b, 0, 0)),
            vec(B_HEAD_DIM), vec(B_HEAD_DIM), vec(B_HEAD_DIM), vec(B_HEAD_DIM),
            _layer_spec(layer, (B_VDIM, TQ)),
        ],
        out_specs=pl.BlockSpec((TQ, B_WIDTH), lambda b, i: (b * nq + i, 0)),
        out_shape=jax.ShapeDtypeStruct((n, B_WIDTH), BF16),
        scratch_shapes=[
            stream(TQ, TQ), stream(TQ, TQ),
            stream(1, TQ), stream(1, TQ),
            stream(1, TQ),
            stream(B_VDIM + ONES_ROWS, TQ),
        ],
        compiler_params=pltpu.CompilerParams(
            dimension_semantics=("arbitrary", "arbitrary"), vmem_limit_bytes=VMEM_LIMIT),
        name=f"attn_l{layer}",
    )(q, q, k, k, vt, lq1, lk1, lq2, lk2, sgb)


def _merge_kernel(ya_ref, yb_ref, yc_ref, gate_ref, x_ref, wb_ref, wo_ref, gpost_ref,
                  gfpre_ref, wfi_ref, wfo_ref, gfpost_ref, o_ref):
    merged = None
    for j, y_ref in enumerate((ya_ref, yb_ref, yc_ref)):
        up = jnp.dot(y_ref[...], wb_ref[j], preferred_element_type=F32)
        term = gate_ref[:, j * D_MODEL:(j + 1) * D_MODEL] * up
        merged = term if merged is None else merged + term
    mix = jnp.dot(merged.astype(BF16), wo_ref[...], preferred_element_type=F32)
    x1 = x_ref[...] + _rms(mix, gpost_ref[...])

    h = _rms(x1, gfpre_ref[...]).astype(BF16)
    f = None
    for lo in range(0, D_FF, FF_CHUNK):
        width = min(FF_CHUNK, D_FF - lo)
        g = jnp.dot(h, wfi_ref[:, lo:lo + width], preferred_element_type=F32)
        u = jnp.dot(h, wfi_ref[:, D_FF + lo:D_FF + lo + width], preferred_element_type=F32)
        a = (g * jax.nn.sigmoid(g) * u).astype(BF16)
        part = jnp.dot(a, wfo_ref[lo:lo + width, :], preferred_element_type=F32)
        f = part if f is None else f + part
    o_ref[...] = x1 + _rms(f, gfpost_ref[...])


def _merge(ya, yb, yc, gates, x, wb, wo, gpost, gfpre, wfi, wfo, gfpost, layer):
    n = x.shape[0]
    tm = TM_MERGE
    row = lambda width: pl.BlockSpec((tm, width), lambda i: (i, 0))
    return pl.pallas_call(
        _merge_kernel,
        grid=(n // tm,),
        in_specs=[
            row(BRANCH_WIDTH), row(BRANCH_WIDTH), row(BRANCH_WIDTH),
            row(N_BRANCH * D_MODEL), row(D_MODEL),
            _layer_spec(layer, (N_BRANCH, BRANCH_WIDTH, D_MODEL)),
            _layer_spec(layer, (D_MODEL, D_MODEL)),
            _layer_spec(layer, (1, D_MODEL)),
            _layer_spec(layer, (1, D_MODEL)),
            _layer_spec(layer, (D_MODEL, 2 * D_FF)),
            _layer_spec(layer, (D_FF, D_MODEL)),
            _layer_spec(layer, (1, D_MODEL)),
        ],
        out_specs=row(D_MODEL),
        out_shape=jax.ShapeDtypeStruct((n, D_MODEL), F32),
        compiler_params=pltpu.CompilerParams(
            dimension_semantics=("arbitrary",), vmem_limit_bytes=VMEM_LIMIT),
        name=f"merge_l{layer}",
    )(ya, yb, yc, gates, x, wb, wo, gpost, gfpre, wfi, wfo, gfpost)


def kernel(x, norm_mix_pre, w_in, gmlp_norm_g, gmlp_norm_b, gmlp_w_s, gmlp_b_s, lambda_q1, lambda_k1,
           lambda_q2, lambda_k2, diff_subln_g, pool_w, pool_scale, w_branch, w_out, norm_mix_post,
           norm_ffn_pre, w_ffn_in, w_ffn_out, norm_ffn_post):
    bsz, s, d = x.shape
    assert (s, d) == (SEQ, D_MODEL)
    depth = w_in.shape[0]
    xf = x.reshape(bsz * s, d)
    vec = lambda a: a.reshape(depth, 1, -1)
    pool_w, w_branch, w_out, w_ffn_in, w_ffn_out = (
        a.astype(BF16) for a in (pool_w, w_branch, w_out, w_ffn_in, w_ffn_out))
    bs = jnp.broadcast_to(gmlp_b_s[:, :, :, None], (depth, A_GROUPS, CHUNK, CHUNK))
    sgb = jnp.broadcast_to(diff_subln_g[:, :, None], (depth, B_VDIM, TQ))
    for l in range(depth):
        lambda_init = 0.8 - 0.6 * math.exp(-0.3 * l)
        ya, q, k, vt, yc, gates = _in_proj(
            xf, vec(norm_mix_pre), w_in, vec(gmlp_norm_g), vec(gmlp_norm_b), gmlp_w_s, bs,
            pool_w, vec(pool_scale), l)
        yb = _attention(q, k, vt, vec(lambda_q1), vec(lambda_k1), vec(lambda_q2), vec(lambda_k2),
                        sgb, lambda_init, l)
        xf = _merge(ya, yb, yc, gates, xf, w_branch, w_out, vec(norm_mix_post), vec(norm_ffn_pre),
                    w_ffn_in, w_ffn_out, vec(norm_ffn_post), l)
    return xf.reshape(bsz, s, d)
```

```python
import functools
import math

import jax
import jax.numpy as jnp
from jax import lax
from jax.experimental import pallas as pl
from jax.experimental.pallas import tpu as pltpu

F32 = jnp.float32
BF16 = jnp.bfloat16

D_MODEL = 1024
SEQ = 2048
CHUNK = 128
A_GROUPS = 4
A_WIDTH = 512
B_HEADS = 4
B_HEAD_DIM = 64
B_VDIM = 128
B_WIDTH = 512
C_WINDOWS = (2, 4, 8, 16)
C_GROUP_DIM = 128
C_WIDTH = 512
N_BRANCH = 3
BRANCH_WIDTH = 512
D_FF = 2816
EPS = 1e-6

COL_A = 0
COL_Q = 2 * A_WIDTH
COL_K = COL_Q + 512
COL_V = COL_K + 512
COL_C = COL_V + B_WIDTH
COL_G = COL_C + C_WIDTH
IN_TOTAL = COL_G + N_BRANCH * D_MODEL

HALO = 16
LANES = 128
VMEM_LIMIT = 56 * 1024 * 1024

TM_PROJ = 512
TM_MERGE = 512
TQ = 512
FF_CHUNK = 512


def _rms(x, g):
    ms = jnp.mean(x * x, axis=-1, keepdims=True)
    return x * lax.rsqrt(ms + EPS) * g


def _layer_spec(layer, shape):
    nd = len(shape)
    return pl.BlockSpec((None,) + tuple(shape), lambda *_: (layer,) + (0,) * nd,
                        pipeline_mode=pl.Buffered(1))


def _in_proj_kernel(x_ref, gpre_ref, w_ref, lng_ref, lnb_ref, ws_ref, bs_ref, pw_ref, psc_ref,
                    ya_ref, q_ref, k_ref, vt_ref, yc_ref, gate_ref, cbuf):
    tm = x_ref.shape[0]
    pos = (pl.program_id(0) * tm) % SEQ
    h = _rms(x_ref[...], gpre_ref[...]).astype(BF16)

    def proj(lo, width):
        return jnp.dot(h, w_ref[:, lo:lo + width].astype(BF16), preferred_element_type=F32)

    def gate(j):
        zg = proj(COL_G + j * D_MODEL, D_MODEL)
        gate_ref[:, j * D_MODEL:(j + 1) * D_MODEL] = jax.nn.sigmoid(zg)


    zc = proj(COL_C, C_WIDTH)
    gate(0)

    @pl.when(pos == 0)
    def _():
        cbuf[0:HALO, :] = jnp.zeros((HALO, C_WIDTH), F32)

    cbuf[HALO:, :] = zc
    p_all = cbuf[...]
    cbuf[0:HALO, :] = p_all[tm:, :]
    s2 = p_all + pltpu.roll(p_all, 1, 0)
    s4 = s2[:, 128:] + pltpu.roll(s2[:, 128:], 2, 0)
    s8 = s4[:, 128:] + pltpu.roll(s4[:, 128:], 4, 0)
    s16 = s8[:, 128:] + pltpu.roll(s8[:, 128:], 8, 0)
    sums = (s2[HALO:, :128], s4[HALO:, :128], s8[HALO:, :128], s16[HALO:, :])
    head_pos = 1 + lax.broadcasted_iota(jnp.int32, (HALO, C_GROUP_DIM), 0)
    for g, w in enumerate(C_WINDOWS):
        cs = slice(g * C_GROUP_DIM, (g + 1) * C_GROUP_DIM)
        head_cnt = jnp.where(pos == 0, jnp.minimum(head_pos, w), w).astype(F32)
        mean = jnp.concatenate([sums[g][:HALO] * (1.0 / head_cnt), sums[g][HALO:] * (1.0 / w)],
                               axis=0)
        pooled = (mean - zc[:, cs]).astype(BF16)
        yc = jnp.dot(pooled, pw_ref[g], preferred_element_type=F32) * psc_ref[:, cs]
        yc_ref[:, cs] = yc.astype(BF16)

    za = proj(COL_A, 2 * A_WIDTH)
    gate(1)
    ga =0.5 * za * (1.0 + lax.erf(za * math.sqrt(0.5)))
    u = ga[:, :A_WIDTH]
    vv = ga[:, A_WIDTH:]
    mu = jnp.mean(vv, axis=-1, keepdims=True)
    dv = vv - mu
    var = jnp.mean(dv * dv, axis=-1, keepdims=True)
    vn = (dv * lax.rsqrt(var + EPS) * lng_ref[...] + lnb_ref[...]).astype(BF16)
    trow = lax.broadcasted_iota(jnp.int32, (CHUNK, CHUNK), 0)
    tcol = lax.broadcasted_iota(jnp.int32, (CHUNK, CHUNK), 1)
    for g in range(A_GROUPS):
        wsg = jnp.where(tcol <= trow, ws_ref[g], 0.0).astype(BF16)
        cs = slice(g * CHUNK, (g + 1) * CHUNK)
        for c in range(tm // CHUNK):
            rs = slice(c * CHUNK, (c + 1) * CHUNK)
            mixed = jnp.dot(wsg, vn[rs, cs], preferred_element_type=F32) + bs_ref[g]
            ya_ref[rs, cs] = (u[rs, cs] * mixed).astype(BF16)

    gate(2)

    q_ref[...] = (proj(COL_Q, 512) * (B_HEAD_DIM ** -0.5 * math.log2(math.e))).astype(BF16)
    k_ref[...] = proj(COL_K, 512).astype(BF16)
    vt = proj(COL_V, B_WIDTH).T.astype(BF16)
    for j in range(tm // TQ):
        vt_ref[j] = vt[:, j * TQ:(j + 1) * TQ]


def _in_proj(x, gpre, w_in, lng, lnb, ws, bs, pw, psc, layer):
    n = x.shape[0]
    tm = TM_PROJ
    row = lambda width: pl.BlockSpec((tm, width), lambda i: (i, 0))
    out_shapes = (
        jax.ShapeDtypeStruct((n, A_WIDTH), BF16),
        jax.ShapeDtypeStruct((n, 512), BF16),
        jax.ShapeDtypeStruct((n, 512), BF16),
        jax.ShapeDtypeStruct((n // TQ, B_WIDTH, TQ), BF16),
        jax.ShapeDtypeStruct((n, C_WIDTH), BF16),
        jax.ShapeDtypeStruct((n, N_BRANCH * D_MODEL), F32),
    )
    return pl.pallas_call(
        _in_proj_kernel,
        grid=(n // tm,),
        in_specs=[
            row(D_MODEL),
            _layer_spec(layer, (1, D_MODEL)),
            _layer_spec(layer, (D_MODEL, IN_TOTAL)),
            _layer_spec(layer, (1, A_WIDTH)),
            _layer_spec(layer, (1, A_WIDTH)),
            _layer_spec(layer, (A_GROUPS, CHUNK, CHUNK)),
            _layer_spec(layer, (A_GROUPS, CHUNK, CHUNK)),
            _layer_spec(layer, (len(C_WINDOWS), C_GROUP_DIM, C_GROUP_DIM)),
            _layer_spec(layer, (1, C_WIDTH)),
        ],
        out_specs=[row(A_WIDTH), row(512), row(512),
                   pl.BlockSpec((tm // TQ, B_WIDTH, TQ), lambda i: (i, 0, 0)), row(C_WIDTH),
                   row(N_BRANCH * D_MODEL)],
        out_shape=out_shapes,
        scratch_shapes=[pltpu.VMEM((HALO + tm, C_WIDTH), F32)],
        compiler_params=pltpu.CompilerParams(
            dimension_semantics=("arbitrary",), vmem_limit_bytes=VMEM_LIMIT),
        name=f"in_proj_l{layer}",
    )(x, gpre, w_in, lng, lnb, ws, bs, pw, psc)


ONES_ROWS = 16
N_STREAMS = 2 * B_HEADS


def _attn_kernel(q_ref, qn_ref, k_ref, k0n_ref, vt_ref, lq1_ref, lk1_ref, lq2_ref, lk2_ref, sgb_ref,
                 o_ref, sa_scr, sb_scr, ma_scr, mb_scr, m_scr, acc_scr, *, lambda_init):
    b = pl.program_id(0)
    i = pl.program_id(1)
    nq = pl.num_programs(1)
    tq = q_ref.shape[0]
    heads = [slice(h * B_VDIM, (h + 1) * B_VDIM) for h in range(B_HEADS)]
    lane = lax.broadcasted_iota(jnp.int32, (tq, B_VDIM), 1)
    zero = jnp.zeros((tq, B_VDIM), BF16)
    key = lax.broadcasted_iota(jnp.int32, (tq, tq), 0)
    qry = lax.broadcasted_iota(jnp.int32, (tq, tq), 1)
    nt = (((1,), (1,)), ((), ()))

    def split_components(qr):
        out = []
        for hs in heads:
            qh = qr[:, hs]
            out += [jnp.where(lane < B_HEAD_DIM, qh, zero), jnp.where(lane >= B_HEAD_DIM, qh, zero)]
        return out

    def scores(qs, kb_of, s_scr, mx_scr, visible=None):
        for h, hs in enumerate(heads):
            kb = kb_of(hs)
            for c in range(2):
                st = 2 * h + c
                s = lax.dot_general(kb, qs[st], nt, preferred_element_type=F32)
                if visible is not None:
                    s = jnp.where(visible, s, -jnp.inf)
                s_scr[st] = s
                mx_scr[st] = jnp.max(s, axis=0, keepdims=True)

    ones_rows = jnp.ones((ONES_ROWS, tq), BF16)

    def consume(j, s_scr, mx_scr, after_head=None, visible=None):
        for h, hs in enumerate(heads):
            vtb = jnp.concatenate([vt_ref[j, hs, :], ones_rows], axis=0)
            for c in range(2):
                st = 2 * h + c
                s = s_scr[st]
                if visible is None:
                    mx = mx_scr[st]
                else:
                    s = jnp.where(visible, s, -jnp.inf)
                    mx = jnp.max(s, axis=0, keepdims=True)
                m_old = m_scr[st]
                m_new = jnp.maximum(m_old, mx)
                alpha = jnp.exp2(m_old - m_new)
                p = jnp.exp2(s - m_new)
                acc_scr[st] = alpha * acc_scr[st] + jnp.dot(vtb, p.astype(BF16),
                                                            preferred_element_type=F32)
                m_scr[st] = m_new
            if after_head is not None:
                after_head(h, hs)

    qs = split_components(q_ref)
    key_tile = lambda j: (lambda hs: k_ref[pl.ds(pl.multiple_of(j * tq, tq), tq), hs])
    causal = key <= qry

    m_scr[...] = jnp.full(m_scr.shape, -jnp.inf, F32)
    acc_scr[...] = jnp.zeros(acc_scr.shape, F32)

    @pl.when((b == 0) & (i == 0))
    def _():
        scores(qs, key_tile(0), sa_scr, ma_scr)

    def pair(t, carry):
        j = 2 * t
        scores(qs, key_tile(j + 1), sb_scr, mb_scr)
        consume(j, sa_scr, ma_scr)
        scores(qs, key_tile(j + 2), sa_scr, ma_scr)
        consume(j + 1, sb_scr, mb_scr)
        return carry

    n_pairs = lax.shift_right_logical(jnp.maximum(i - 1, 0), 1)
    lax.fori_loop(0, n_pairs, pair, 0)
    done = 2 * n_pairs

    lam = (jnp.exp(jnp.sum(lq1_ref[...] * lk1_ref[...], keepdims=True))
           - jnp.exp(jnp.sum(lq2_ref[...] * lk2_ref[...], keepdims=True)) + lambda_init)

    def finish_head(h, hs):
        a0, a1 = acc_scr[2 * h], acc_scr[2 * h + 1]
        l0, l1 = a0[B_VDIM:B_VDIM + 1], a1[B_VDIM:B_VDIM + 1]
        ot = a0[:B_VDIM] * (1.0 / l0) - a1[:B_VDIM] * (lam / l1)
        ms = jnp.mean(ot * ot, axis=0, keepdims=True)
        y = ot * lax.rsqrt(ms + EPS) * sgb_ref[...] * (1.0 - lambda_init)
        o_ref[:, hs] = y.T.astype(BF16)

    def stage_next():
        scores(split_components(qn_ref), lambda hs: k0n_ref[:, hs], sa_scr, ma_scr)

    @pl.when(i == 0)
    def _():
        consume(0, sa_scr, ma_scr, finish_head, causal)
        stage_next()

    @pl.when((i & 1) == 1)
    def _():
        scores(qs, key_tile(i), sb_scr, mb_scr, causal)
        consume(done, sa_scr, ma_scr)
        consume(i, sb_scr, mb_scr, finish_head)
        stage_next()

    @pl.when(((i & 1) == 0) & (i > 0))
    def _():
        scores(qs, key_tile(done + 1), sb_scr, mb_scr)
        consume(done, sa_scr, ma_scr)
        scores(qs, key_tile(i), sa_scr, ma_scr, causal)
        consume(done + 1, sb_scr, mb_scr)
        consume(i, sa_scr, ma_scr, finish_head)
        stage_next()


def _attention(q, k, vt, lq1, lk1, lq2, lk2, sgb, lambda_init, layer):
    n = q.shape[0]
    bsz = n // SEQ
    nq = SEQ // TQ
    last = bsz * nq - 1
    vec = lambda width: _layer_spec(layer, (1, width))
    stream = lambda *shape: pltpu.VMEM((N_STREAMS,) + shape, F32)
    return pl.pallas_call(
        functools.partial(_attn_kernel, lambda_init=lambda_init),
        grid=(bsz, nq),
        in_specs=[
            pl.BlockSpec((TQ, B_WIDTH), lambda b, i: (b * nq + i, 0)),
            pl.BlockSpec((TQ, B_WIDTH), lambda b, i: (jnp.minimum(b * nq + i + 1, last), 0)),
            pl.BlockSpec((SEQ, B_WIDTH), lambda b, i: (b, 0)),
            pl.BlockSpec((TQ, B_WIDTH),
                         lambda b, i: (jnp.minimum(b * nq + i + 1, last) // nq * nq, 0)),
            pl.BlockSpec((nq, B_WIDTH, TQ), lambda b, i: (b, 0, 0)),
            vec(B_HEAD_DIM), vec(B_HEAD_DIM), vec(B_HEAD_DIM), vec(B_HEAD_DIM),
            _layer_spec(layer, (B_VDIM, TQ)),
        ],
        out_specs=pl.BlockSpec((TQ, B_WIDTH), lambda b, i: (b * nq + i, 0)),
        out_shape=jax.ShapeDtypeStruct((n, B_WIDTH), BF16),
        scratch_shapes=[
            stream(TQ, TQ), stream(TQ, TQ),
            stream(1, TQ), stream(1, TQ),
            stream(1, TQ),
            stream(B_VDIM + ONES_ROWS, TQ),
        ],
        compiler_params=pltpu.CompilerParams(
            dimension_semantics=("arbitrary", "arbitrary"), vmem_limit_bytes=VMEM_LIMIT),
        name=f"attn_l{layer}",
    )(q, q, k, k, vt, lq1, lk1, lq2, lk2, sgb)


def _merge_kernel(ya_ref, yb_ref, yc_ref, gate_ref, x_ref, wb_ref, wo_ref, gpost_ref,
                  gfpre_ref, wfi_ref, wfo_ref, gfpost_ref, o_ref):
    merged = None
    for j, y_ref in enumerate((ya_ref, yb_ref, yc_ref)):
        up = jnp.dot(y_ref[...], wb_ref[j], preferred_element_type=F32)
        term = gate_ref[:, j * D_MODEL:(j + 1) * D_MODEL] * up
        merged = term if merged is None else merged + term
    mix = jnp.dot(merged.astype(BF16), wo_ref[...], preferred_element_type=F32)
    x1 = x_ref[...] + _rms(mix, gpost_ref[...])

    h = _rms(x1, gfpre_ref[...]).astype(BF16)
    f = None
    for lo in range(0, D_FF, FF_CHUNK):
        width = min(FF_CHUNK, D_FF - lo)
        g = jnp.dot(h, wfi_ref[:, lo:lo + width], preferred_element_type=F32)
        u = jnp.dot(h, wfi_ref[:, D_FF + lo:D_FF + lo + width], preferred_element_type=F32)
        a = (g * jax.nn.sigmoid(g) * u).astype(BF16)
        part = jnp.dot(a, wfo_ref[lo:lo + width, :], preferred_element_type=F32)
        f = part if f is None else f + part
    o_ref[...] = x1 + _rms(f, gfpost_ref[...])


def _merge(ya, yb, yc, gates, x, wb, wo, gpost, gfpre, wfi, wfo, gfpost, layer):
    n = x.shape[0]
    tm = TM_MERGE
    row = lambda width: pl.BlockSpec((tm, width), lambda i: (i, 0))
    return pl.pallas_call(
        _merge_kernel,
        grid=(n // tm,),
        in_specs=[
            row(BRANCH_WIDTH), row(BRANCH_WIDTH), row(BRANCH_WIDTH),
            row(N_BRANCH * D_MODEL), row(D_MODEL),
            _layer_spec(layer, (N_BRANCH, BRANCH_WIDTH, D_MODEL)),
            _layer_spec(layer, (D_MODEL, D_MODEL)),
            _layer_spec(layer, (1, D_MODEL)),
            _layer_spec(layer, (1, D_MODEL)),
            _layer_spec(layer, (D_MODEL, 2 * D_FF)),
            _layer_spec(layer, (D_FF, D_MODEL)),
            _layer_spec(layer, (1, D_MODEL)),
        ],
        out_specs=row(D_MODEL),
        out_shape=jax.ShapeDtypeStruct((n, D_MODEL), F32),
        compiler_params=pltpu.CompilerParams(
            dimension_semantics=("arbitrary",), vmem_limit_bytes=VMEM_LIMIT),
        name=f"merge_l{layer}",
    )(ya, yb, yc, gates, x, wb, wo, gpost, gfpre, wfi, wfo, gfpost)


def kernel(x, norm_mix_pre, w_in, gmlp_norm_g, gmlp_norm_b, gmlp_w_s, gmlp_b_s, lambda_q1, lambda_k1,
           lambda_q2, lambda_k2, diff_subln_g, pool_w, pool_scale, w_branch, w_out, norm_mix_post,
           norm_ffn_pre, w_ffn_in, w_ffn_out, norm_ffn_post):
    bsz, s, d = x.shape
    assert (s, d) == (SEQ, D_MODEL)
    depth = w_in.shape[0]
    xf = x.reshape(bsz * s, d)
    vec = lambda a: a.reshape(depth, 1, -1)
    pool_w, w_branch, w_out, w_ffn_in, w_ffn_out = (
        a.astype(BF16) for a in (pool_w, w_branch, w_out, w_ffn_in, w_ffn_out))
    bs = jnp.broadcast_to(gmlp_b_s[:, :, :, None], (depth, A_GROUPS, CHUNK, CHUNK))
    sgb = jnp.broadcast_to(diff_subln_g[:, :, None], (depth, B_VDIM, TQ))
    for l in range(depth):
        lambda_init = 0.8 - 0.6 * math.exp(-0.3 * l)
        ya, q, k, vt, yc, gates = _in_proj(
            xf, vec(norm_mix_pre), w_in, vec(gmlp_norm_g), vec(gmlp_norm_b), gmlp_w_s, bs,
            pool_w, vec(pool_scale), l)
        yb = _attention(q, k, vt, vec(lambda_q1), vec(lambda_k1), vec(lambda_q2), vec(lambda_k2),
                        sgb, lambda_init, l)
        xf = _merge(ya, yb, yc, gates, xf, w_branch, w_out, vec(norm_mix_post), vec(norm_ffn_pre),
                    w_ffn_in, w_ffn_out, vec(norm_ffn_post), l)
    return xf.reshape(bsz, s, d)
```

```python
import functools
import math

import jax
import jax.numpy as jnp
from jax import lax
from jax.experimental import pallas as pl
from jax.experimental.pallas import tpu as pltpu

F32 = jnp.float32
BF16 = jnp.bfloat16

D_MODEL = 1024
SEQ = 2048
CHUNK = 128
A_GROUPS = 4
A_WIDTH = 512
B_HEADS = 4
B_HEAD_DIM = 64
B_VDIM = 128
B_WIDTH = 512
C_WINDOWS = (2, 4, 8, 16)
C_GROUP_DIM = 128
C_WIDTH = 512
N_BRANCH = 3
BRANCH_WIDTH = 512
D_FF = 2816
EPS = 1e-6

COL_A = 0
COL_Q = 2 * A_WIDTH
COL_K = COL_Q + 512
COL_V = COL_K + 512
COL_C = COL_V + B_WIDTH
COL_G = COL_C + C_WIDTH
IN_TOTAL = COL_G + N_BRANCH * D_MODEL

HALO = 16
LANES = 128
VMEM_LIMIT = 56 * 1024 * 1024

TM_PROJ = 512
TM_MERGE = 512
TQ = 512
FF_CHUNK = 512


def _rms(x, g):
    ms = jnp.mean(x * x, axis=-1, keepdims=True)
    return x * lax.rsqrt(ms + EPS) * g


def _layer_spec(layer, shape):
    nd = len(shape)
    return pl.BlockSpec((None,) + tuple(shape), lambda *_: (layer,) + (0,) * nd,
                        pipeline_mode=pl.Buffered(1))


def _in_proj_kernel(x_ref, vec_ref, w_ref, ws_ref, bs_ref, pw_ref,
                    ya_ref, q_ref, k_ref, vt_ref, yc_ref, gate_ref, cbuf):
    tm = x_ref.shape[0]
    pos = (pl.program_id(0) * tm) % SEQ
    gpre = vec_ref[0:1, :]
    lng, lnb, psc = (vec_ref[r:r + 1, :A_WIDTH] for r in (1, 2, 3))
    h = _rms(x_ref[...], gpre).astype(BF16)

    def proj(lo, width):
        return jnp.dot(h, w_ref[:, lo:lo + width].astype(BF16), preferred_element_type=F32)

    def gate(j):
        zg = proj(COL_G + j * D_MODEL, D_MODEL)
        gate_ref[:, j * D_MODEL:(j + 1) * D_MODEL] = jax.nn.sigmoid(zg)


    zc = proj(COL_C, C_WIDTH)
    gate(0)

    @pl.when(pos == 0)
    def _():
        cbuf[0:HALO, :] = jnp.zeros((HALO, C_WIDTH), F32)

    cbuf[HALO:, :] = zc
    p_all = cbuf[...]
    cbuf[0:HALO, :] = p_all[tm:, :]
    s2 = p_all + pltpu.roll(p_all, 1, 0)
    s4 = s2[:, 128:] + pltpu.roll(s2[:, 128:], 2, 0)
    s8 = s4[:, 128:] + pltpu.roll(s4[:, 128:], 4, 0)
    s16 = s8[:, 128:] + pltpu.roll(s8[:, 128:], 8, 0)
    sums = (s2[HALO:, :128], s4[HALO:, :128], s8[HALO:, :128], s16[HALO:, :])
    head_pos = 1 + lax.broadcasted_iota(jnp.int32, (HALO, C_GROUP_DIM), 0)
    pooled = []
    for g, w in enumerate(C_WINDOWS):
        cs = slice(g * C_GROUP_DIM, (g + 1) * C_GROUP_DIM)
        head_cnt = jnp.where(pos == 0, jnp.minimum(head_pos, w), w).astype(F32)
        mean = jnp.concatenate([sums[g][:HALO] * (1.0 / head_cnt), sums[g][HALO:] * (1.0 / w)],
                               axis=0)
        pooled.append((mean - zc[:, cs]).astype(BF16))
    zero_w = jnp.zeros((C_GROUP_DIM, C_GROUP_DIM), BF16)
    for g in range(0, len(C_WINDOWS), 2):
        cs = slice(g * C_GROUP_DIM, (g + 2) * C_GROUP_DIM)
        w_pair = jnp.concatenate([jnp.concatenate([pw_ref[g], zero_w], axis=1),
                                  jnp.concatenate([zero_w, pw_ref[g + 1]], axis=1)], axis=0)
        yc = jnp.dot(jnp.concatenate(pooled[g:g + 2], axis=1), w_pair,
                     preferred_element_type=F32) * psc[:, cs]
        yc_ref[:, cs] = yc.astype(BF16)

    za = proj(COL_A, 2 * A_WIDTH)
    gate(1)
    ga = 0.5 * za * (1.0 + lax.erf(za * math.sqrt(0.5)))
    u = ga[:, :A_WIDTH]
    vv = ga[:, A_WIDTH:]
    mu = jnp.mean(vv, axis=-1, keepdims=True)
    dv = vv - mu
    var = jnp.mean(dv * dv, axis=-1, keepdims=True)
    vn = (dv * lax.rsqrt(var + EPS) * lng + lnb).astype(BF16)
    trow = lax.broadcasted_iota(jnp.int32, (CHUNK, CHUNK), 0)
    tcol = lax.broadcasted_iota(jnp.int32, (CHUNK, CHUNK), 1)
    chunks = [slice(c * CHUNK, (c + 1) * CHUNK) for c in range(tm // CHUNK)]
    for g in range(A_GROUPS):
        wsg = jnp.where(tcol <= trow, ws_ref[g], 0.0).astype(BF16)
        cs = slice(g * CHUNK, (g + 1) * CHUNK)
        v_side = jnp.concatenate([vn[rs, cs] for rs in chunks], axis=1)
        mixed = jnp.dot(wsg, v_side, preferred_element_type=F32)
        for rs, ls in zip(chunks, chunks):
            ya_ref[rs, cs] = (u[rs, cs] * (mixed[:, ls] + bs_ref[g])).astype(BF16)

    gate(2)

    q_ref[...] = (proj(COL_Q, 512) * (B_HEAD_DIM ** -0.5 * math.log2(math.e))).astype(BF16)
    k_ref[...] = proj(COL_K, 512).astype(BF16)
    vt = proj(COL_V, B_WIDTH).T.astype(BF16)
    for j in range(tm // TQ):
        vt_ref[j] = vt[:, j * TQ:(j + 1) * TQ]


def _in_proj(x, vecs, w_in, ws, bs, pw, layer):
    n = x.shape[0]
    tm = TM_PROJ
    row = lambda width: pl.BlockSpec((tm, width), lambda i: (i, 0))
    out_shapes = (
        jax.ShapeDtypeStruct((n, A_WIDTH), BF16),
        jax.ShapeDtypeStruct((n, 512), BF16),
        jax.ShapeDtypeStruct((n, 512), BF16),
        jax.ShapeDtypeStruct((n // TQ, B_WIDTH, TQ), BF16),
        jax.ShapeDtypeStruct((n, C_WIDTH), BF16),
        jax.ShapeDtypeStruct((n, N_BRANCH * D_MODEL), F32),
    )
    return pl.pallas_call(
        _in_proj_kernel,
        grid=(n // tm,),
        in_specs=[
            row(D_MODEL),
            _layer_spec(layer, (4, D_MODEL)),
            _layer_spec(layer, (D_MODEL, IN_TOTAL)),
            _layer_spec(layer, (A_GROUPS, CHUNK, CHUNK)),
            _layer_spec(layer, (A_GROUPS, CHUNK, CHUNK)),
            _layer_spec(layer, (len(C_WINDOWS), C_GROUP_DIM, C_GROUP_DIM)),
        ],
        out_specs=[row(A_WIDTH), row(512), row(512),
                   pl.BlockSpec((tm // TQ, B_WIDTH, TQ), lambda i: (i, 0, 0)), row(C_WIDTH),
                   row(N_BRANCH * D_MODEL)],
        out_shape=out_shapes,
        scratch_shapes=[pltpu.VMEM((HALO + tm, C_WIDTH), F32)],
        compiler_params=pltpu.CompilerParams(
            dimension_semantics=("arbitrary",), vmem_limit_bytes=VMEM_LIMIT),
        name=f"in_proj_l{layer}",
    )(x, vecs, w_in, ws, bs, pw)


ONES_ROWS = 16
N_STREAMS = 2 * B_HEADS


def _attn_kernel(q_ref, qn_ref, k_ref, k0n_ref, vt_ref, lam_ref, sgb_ref,
                 o_ref, sa_scr, sb_scr, ma_scr, mb_scr, m_scr, acc_scr, *, lambda_init):
    b = pl.program_id(0)
    i = pl.program_id(1)
    nq = pl.num_programs(1)
    tq = q_ref.shape[0]
    heads = [slice(h * B_VDIM, (h + 1) * B_VDIM) for h in range(B_HEADS)]
    lane = lax.broadcasted_iota(jnp.int32, (tq, B_VDIM), 1)
    zero = jnp.zeros((tq, B_VDIM), BF16)
    key = lax.broadcasted_iota(jnp.int32, (tq, tq), 0)
    qry = lax.broadcasted_iota(jnp.int32, (tq, tq), 1)
    nt = (((1,), (1,)), ((), ()))

    def split_components(qr):
        out = []
        for hs in heads:
            qh = qr[:, hs]
            out += [jnp.where(lane < B_HEAD_DIM, qh, zero), jnp.where(lane >= B_HEAD_DIM, qh, zero)]
        return out

    def scores(qs, kb_of, s_scr, mx_scr, visible=None):
        for h, hs in enumerate(heads):
            kb = kb_of(hs)
            for c in range(2):
                st = 2 * h + c
                s = lax.dot_general(kb, qs[st], nt, preferred_element_type=F32)
                if visible is not None:
                    s = jnp.where(visible, s, -jnp.inf)
                s_scr[st] = s
                mx_scr[st] = jnp.max(s, axis=0, keepdims=True)

    ones_rows = jnp.ones((ONES_ROWS, tq), BF16)

    def consume(j, s_scr, mx_scr, after_head=None, visible=None):
        for h, hs in enumerate(heads):
            vtb = jnp.concatenate([vt_ref[j, hs, :], ones_rows], axis=0)
            for c in range(2):
                st = 2 * h + c
                s = s_scr[st]
                if visible is None:
                    mx = mx_scr[st]
                else:
                    s = jnp.where(visible, s, -jnp.inf)
                    mx = jnp.max(s, axis=0, keepdims=True)
                m_old = m_scr[st]
                m_new = jnp.maximum(m_old, mx)
                alpha = jnp.exp2(m_old - m_new)
                p = jnp.exp2(s - m_new)
                acc_scr[st] = alpha * acc_scr[st] + jnp.dot(vtb, p.astype(BF16),
                                                            preferred_element_type=F32)
                m_scr[st] = m_new
            if after_head is not None:
                after_head(h, hs)

    qs = split_components(q_ref)
    key_tile = lambda j: (lambda hs: k_ref[pl.ds(pl.multiple_of(j * tq, tq), tq), hs])
    causal = key <= qry

    m_scr[...] = jnp.full(m_scr.shape, -jnp.inf, F32)
    acc_scr[...] = jnp.zeros(acc_scr.shape, F32)

    @pl.when((b == 0) & (i == 0))
    def _():
        scores(qs, key_tile(0), sa_scr, ma_scr)

    def pair(t, carry):
        j = 2 * t
        scores(qs, key_tile(j + 1), sb_scr, mb_scr)
        consume(j, sa_scr, ma_scr)
        scores(qs, key_tile(j + 2), sa_scr, ma_scr)
        consume(j + 1, sb_scr, mb_scr)
        return carry

    n_pairs = lax.shift_right_logical(jnp.maximum(i - 1, 0), 1)
    lax.fori_loop(0, n_pairs, pair, 0)
    done = 2 * n_pairs

    lam = (jnp.exp(jnp.sum(lam_ref[0:1, :] * lam_ref[1:2, :], keepdims=True))
           - jnp.exp(jnp.sum(lam_ref[2:3, :] * lam_ref[3:4, :], keepdims=True)) + lambda_init)

    def finish_head(h, hs):
        a0, a1 = acc_scr[2 * h], acc_scr[2 * h + 1]
        l0, l1 = a0[B_VDIM:B_VDIM + 1], a1[B_VDIM:B_VDIM + 1]
        ot = a0[:B_VDIM] * (1.0 / l0) - a1[:B_VDIM] * (lam / l1)
        ms = jnp.mean(ot * ot, axis=0, keepdims=True)
        y = ot * lax.rsqrt(ms + EPS) * sgb_ref[...] * (1.0 - lambda_init)
        o_ref[:, hs] = y.T.astype(BF16)

    def stage_next():
        scores(split_components(qn_ref), lambda hs: k0n_ref[:, hs], sa_scr, ma_scr)

    @pl.when(i == 0)
    def _():
        consume(0, sa_scr, ma_scr, finish_head, causal)
        stage_next()

    @pl.when((i & 1) == 1)
    def _():
        scores(qs, key_tile(i), sb_scr, mb_scr, causal)
        consume(done, sa_scr, ma_scr)
        consume(i, sb_scr, mb_scr, finish_head)
        stage_next()

    @pl.when(((i & 1) == 0) & (i > 0))
    def _():
        scores(qs, key_tile(done + 1), sb_scr, mb_scr)
        consume(done, sa_scr, ma_scr)
        scores(qs, key_tile(i), sa_scr, ma_scr, causal)
        consume(done + 1, sb_scr, mb_scr)
        consume(i, sa_scr, ma_scr, finish_head)
        stage_next()


def _attention(q, k, vt, lam_vecs, sgb, lambda_init, layer):
    n = q.shape[0]
    bsz = n // SEQ
    nq = SEQ // TQ
    last = bsz * nq - 1
    stream = lambda *shape: pltpu.VMEM((N_STREAMS,) + shape, F32)
    return pl.pallas_call(
        functools.partial(_attn_kernel, lambda_init=lambda_init),
        grid=(bsz, nq),
        in_specs=[
            pl.BlockSpec((TQ, B_WIDTH), lambda b, i: (b * nq + i, 0)),
            pl.BlockSpec((TQ, B_WIDTH), lambda b, i: (jnp.minimum(b * nq + i + 1, last), 0)),
            pl.BlockSpec((SEQ, B_WIDTH), lambda b, i: (b, 0)),
            pl.BlockSpec((TQ, B_WIDTH),
                         lambda b, i: (jnp.minimum(b * nq + i + 1, last) // nq * nq, 0)),
            pl.BlockSpec((nq, B_WIDTH, TQ), lambda b, i: (b, 0, 0)),
            _layer_spec(layer, (4, B_HEAD_DIM)),
            _layer_spec(layer, (B_VDIM, TQ)),
        ],
        out_specs=pl.BlockSpec((TQ, B_WIDTH), lambda b, i: (b * nq + i, 0)),
        out_shape=jax.ShapeDtypeStruct((n, B_WIDTH), BF16),
        scratch_shapes=[
            stream(TQ, TQ), stream(TQ, TQ),
            stream(1, TQ), stream(1, TQ),
            stream(1, TQ),
            stream(B_VDIM + ONES_ROWS, TQ),
        ],
        compiler_params=pltpu.CompilerParams(
            dimension_semantics=("arbitrary", "arbitrary"), vmem_limit_bytes=VMEM_LIMIT),
        name=f"attn_l{layer}",
    )(q, q, k, k, vt, lam_vecs, sgb)


def _merge_kernel(ya_ref, yb_ref, yc_ref, gate_ref, x_ref, wb_ref, wo_ref, vec_ref, wfi_ref, wfo_ref,
                  o_ref):
    gpost, gfpre, gfpost = (vec_ref[r:r + 1, :] for r in range(3))
    merged = None
    for j, y_ref in enumerate((ya_ref, yb_ref, yc_ref)):
        up = jnp.dot(y_ref[...], wb_ref[j], preferred_element_type=F32)
        term = gate_ref[:, j * D_MODEL:(j + 1) * D_MODEL] * up
        merged = term if merged is None else merged + term
    mix = jnp.dot(merged.astype(BF16), wo_ref[...], preferred_element_type=F32)
    x1 = x_ref[...] + _rms(mix, gpost)

    h = _rms(x1, gfpre).astype(BF16)
    f = None
    for lo in range(0, D_FF, FF_CHUNK):
        width = min(FF_CHUNK, D_FF - lo)
        g = jnp.dot(h, wfi_ref[:, lo:lo + width], preferred_element_type=F32)
        u = jnp.dot(h, wfi_ref[:, D_FF + lo:D_FF + lo + width], preferred_element_type=F32)
        a = (g * jax.nn.sigmoid(g) * u).astype(BF16)
        part = jnp.dot(a, wfo_ref[lo:lo + width, :], preferred_element_type=F32)
        f = part if f is None else f + part
    o_ref[...] = x1 + _rms(f, gfpost)


def _merge(ya, yb, yc, gates, x, wb, wo, vecs, wfi, wfo, layer):
    n = x.shape[0]
    tm = TM_MERGE
    row = lambda width: pl.BlockSpec((tm, width), lambda i: (i, 0))
    return pl.pallas_call(
        _merge_kernel,
        grid=(n // tm,),
        in_specs=[
            row(BRANCH_WIDTH), row(BRANCH_WIDTH), row(BRANCH_WIDTH),
            row(N_BRANCH * D_MODEL), row(D_MODEL),
            _layer_spec(layer, (N_BRANCH, BRANCH_WIDTH, D_MODEL)),
            _layer_spec(layer, (D_MODEL, D_MODEL)),
            _layer_spec(layer, (3, D_MODEL)),
            _layer_spec(layer, (D_MODEL, 2 * D_FF)),
            _layer_spec(layer, (D_FF, D_MODEL)),
        ],
        out_specs=row(D_MODEL),
        out_shape=jax.ShapeDtypeStruct((n, D_MODEL), F32),
        compiler_params=pltpu.CompilerParams(
            dimension_semantics=("arbitrary",), vmem_limit_bytes=VMEM_LIMIT),
        name=f"merge_l{layer}",
    )(ya, yb, yc, gates, x, wb, wo, vecs, wfi, wfo)


def kernel(x, norm_mix_pre, w_in, gmlp_norm_g, gmlp_norm_b, gmlp_w_s, gmlp_b_s, lambda_q1, lambda_k1,
           lambda_q2, lambda_k2, diff_subln_g, pool_w, pool_scale, w_branch, w_out, norm_mix_post,
           norm_ffn_pre, w_ffn_in, w_ffn_out, norm_ffn_post):
    bsz, s, d = x.shape
    assert (s, d) == (SEQ, D_MODEL)
    depth = w_in.shape[0]
    xf = x.reshape(bsz * s, d)
    pad = lambda a: jnp.pad(a, ((0, 0), (0, D_MODEL - a.shape[1])))
    proj_vecs = jnp.stack([norm_mix_pre, pad(gmlp_norm_g), pad(gmlp_norm_b), pad(pool_scale)], axis=1)
    lam_vecs = jnp.stack([lambda_q1, lambda_k1, lambda_q2, lambda_k2], axis=1)
    merge_vecs = jnp.stack([norm_mix_post, norm_ffn_pre, norm_ffn_post], axis=1)
    pool_w, w_branch, w_out, w_ffn_in, w_ffn_out = (
        a.astype(BF16) for a in (pool_w, w_branch, w_out, w_ffn_in, w_ffn_out))
    bs = jnp.broadcast_to(gmlp_b_s[:, :, :, None], (depth, A_GROUPS, CHUNK, CHUNK))
    sgb = jnp.broadcast_to(diff_subln_g[:, :, None], (depth, B_VDIM, TQ))
    for l in range(depth):
        lambda_init = 0.8 - 0.6 * math.exp(-0.3 * l)
        ya, q, k, vt, yc, gates = _in_proj(xf, proj_vecs, w_in, gmlp_w_s, bs, pool_w, l)
        yb = _attention(q, k, vt, lam_vecs, sgb, lambda_init, l)
        xf = _merge(ya, yb, yc, gates, xf, w_branch, w_out, merge_vecs, w_ffn_in, w_ffn_out, l)
    return xf.reshape(bsz, s, d)
```

```python
import functools
import math

import jax
import jax.numpy as jnp
from jax import lax
from jax.experimental import pallas as pl
from jax.experimental.pallas import tpu as pltpu

F32 = jnp.float32
BF16 = jnp.bfloat16

D_MODEL = 1024
SEQ = 2048
CHUNK = 128
A_GROUPS = 4
A_WIDTH = 512
B_HEADS = 4
B_HEAD_DIM = 64
B_VDIM = 128
B_WIDTH = 512
C_WINDOWS = (2, 4, 8, 16)
C_GROUP_DIM = 128
C_WIDTH = 512
N_BRANCH = 3
BRANCH_WIDTH = 512
D_FF = 2816
EPS = 1e-6

COL_A = 0
COL_Q = 2 * A_WIDTH
COL_K = COL_Q + 512
COL_V = COL_K + 512
COL_C = COL_V + B_WIDTH
COL_G = COL_C + C_WIDTH
IN_TOTAL = COL_G + N_BRANCH * D_MODEL

HALO = 16
LANES = 128
VMEM_LIMIT = 56 * 1024 * 1024

TM_PROJ = 512
TM_MERGE = 512
TQ = 512
FF_CHUNK = 512


def _rms(x, g):
    ms = jnp.mean(x * x, axis=-1, keepdims=True)
    return x * lax.rsqrt(ms + EPS) * g


def _sigmoid(x):
    return 0.5 * (jnp.tanh(0.5 * x) + 1.0)


def _layer_spec(layer, shape):
    nd = len(shape)
    return pl.BlockSpec((None,) + tuple(shape), lambda *_: (layer,) + (0,) * nd,
                        pipeline_mode=pl.Buffered(1))


def _in_proj_kernel(x_ref, vec_ref, w_ref, ws_ref, bs_ref, pw_ref,
                    ya_ref, q_ref, k_ref, vt_ref, yc_ref, gate_ref, cbuf):
    tm = x_ref.shape[0]
    pos = (pl.program_id(0) * tm) % SEQ
    gpre = vec_ref[0:1, :]
    lng, lnb, psc = (vec_ref[r:r + 1, :A_WIDTH] for r in (1, 2, 3))
    h = _rms(x_ref[...], gpre).astype(BF16)

    def proj(lo, width):
        return jnp.dot(h, w_ref[:, lo:lo + width].astype(BF16), preferred_element_type=F32)

    def gate(j):
        zg = proj(COL_G + j * D_MODEL, D_MODEL)
        gate_ref[:, j * D_MODEL:(j + 1) * D_MODEL] = _sigmoid(zg)


    zc = proj(COL_C, C_WIDTH)
    gate(0)

    @pl.when(pos == 0)
    def _():
        cbuf[0:HALO, :] = jnp.zeros((HALO, C_WIDTH), F32)

    cbuf[HALO:, :] = zc
    p_all = cbuf[...]
    cbuf[0:HALO, :] = p_all[tm:, :]
    s2 = p_all + pltpu.roll(p_all, 1, 0)
    s4 = s2[:, 128:] + pltpu.roll(s2[:, 128:], 2, 0)
    s8 = s4[:, 128:] + pltpu.roll(s4[:, 128:], 4, 0)
    s16 = s8[:, 128:] + pltpu.roll(s8[:, 128:], 8, 0)
    sums = (s2[HALO:, :128], s4[HALO:, :128], s8[HALO:, :128], s16[HALO:, :])
    head_pos = 1 + lax.broadcasted_iota(jnp.int32, (HALO, C_GROUP_DIM), 0)
    pooled = []
    for g, w in enumerate(C_WINDOWS):
        cs = slice(g * C_GROUP_DIM, (g + 1) * C_GROUP_DIM)
        head_cnt = jnp.where(pos == 0, jnp.minimum(head_pos, w), w).astype(F32)
        mean = jnp.concatenate([sums[g][:HALO] * (1.0 / head_cnt), sums[g][HALO:] * (1.0 / w)],
                               axis=0)
        pooled.append((mean - zc[:, cs]).astype(BF16))
    zero_w = jnp.zeros((C_GROUP_DIM, C_GROUP_DIM), BF16)
    for g in range(0, len(C_WINDOWS), 2):
        cs = slice(g * C_GROUP_DIM, (g + 2) * C_GROUP_DIM)
        w_pair = jnp.concatenate([jnp.concatenate([pw_ref[g], zero_w], axis=1),
                                  jnp.concatenate([zero_w, pw_ref[g + 1]], axis=1)], axis=0)
        yc = jnp.dot(jnp.concatenate(pooled[g:g + 2], axis=1), w_pair,
                     preferred_element_type=F32) * psc[:, cs]
        yc_ref[:, cs] = yc.astype(BF16)

    za = proj(COL_A, 2 * A_WIDTH)
    gate(1)
    ga = 0.5 * za * (1.0 + lax.erf(za * math.sqrt(0.5)))
    u = ga[:, :A_WIDTH]
    vv = ga[:, A_WIDTH:]
    mu = jnp.mean(vv, axis=-1, keepdims=True)
    dv = vv - mu
    var = jnp.mean(dv * dv, axis=-1, keepdims=True)
    vn = (dv * lax.rsqrt(var + EPS) * lng + lnb).astype(BF16)
    trow = lax.broadcasted_iota(jnp.int32, (CHUNK, CHUNK), 0)
    tcol = lax.broadcasted_iota(jnp.int32, (CHUNK, CHUNK), 1)
    chunks = [slice(c * CHUNK, (c + 1) * CHUNK) for c in range(tm // CHUNK)]
    for g in range(A_GROUPS):
        wsg = jnp.where(tcol <= trow, ws_ref[g], 0.0).astype(BF16)
        cs = slice(g * CHUNK, (g + 1) * CHUNK)
        v_side = jnp.concatenate([vn[rs, cs] for rs in chunks], axis=1)
        mixed = jnp.dot(wsg, v_side, preferred_element_type=F32)
        for rs, ls in zip(chunks, chunks):
            ya_ref[rs, cs] = (u[rs, cs] * (mixed[:, ls] + bs_ref[g])).astype(BF16)

    gate(2)

    q_ref[...] = (proj(COL_Q, 512) * (B_HEAD_DIM ** -0.5 * math.log2(math.e))).astype(BF16)
    k_ref[...] = proj(COL_K, 512).astype(BF16)
    vt = proj(COL_V, B_WIDTH).T.astype(BF16)
    for j in range(tm // TQ):
        vt_ref[j] = vt[:, j * TQ:(j + 1) * TQ]


def _in_proj(x, vecs, w_in, ws, bs, pw, layer):
    n = x.shape[0]
    tm = TM_PROJ
    row = lambda width: pl.BlockSpec((tm, width), lambda i: (i, 0))
    out_shapes = (
        jax.ShapeDtypeStruct((n, A_WIDTH), BF16),
        jax.ShapeDtypeStruct((n, 512), BF16),
        jax.ShapeDtypeStruct((n, 512), BF16),
        jax.ShapeDtypeStruct((n // TQ, B_WIDTH, TQ), BF16),
        jax.ShapeDtypeStruct((n, C_WIDTH), BF16),
        jax.ShapeDtypeStruct((n, N_BRANCH * D_MODEL), F32),
    )
    return pl.pallas_call(
        _in_proj_kernel,
        grid=(n // tm,),
        in_specs=[
            row(D_MODEL),
            _layer_spec(layer, (4, D_MODEL)),
            _layer_spec(layer, (D_MODEL, IN_TOTAL)),
            _layer_spec(layer, (A_GROUPS, CHUNK, CHUNK)),
            _layer_spec(layer, (A_GROUPS, CHUNK, CHUNK)),
            _layer_spec(layer, (len(C_WINDOWS), C_GROUP_DIM, C_GROUP_DIM)),
        ],
        out_specs=[row(A_WIDTH), row(512), row(512),
                   pl.BlockSpec((tm // TQ, B_WIDTH, TQ), lambda i: (i, 0, 0)), row(C_WIDTH),
                   row(N_BRANCH * D_MODEL)],
        out_shape=out_shapes,
        scratch_shapes=[pltpu.VMEM((HALO + tm, C_WIDTH), F32)],
        compiler_params=pltpu.CompilerParams(
            dimension_semantics=("arbitrary",), vmem_limit_bytes=VMEM_LIMIT),
        name=f"in_proj_l{layer}",
    )(x, vecs, w_in, ws, bs, pw)


ONES_ROWS = 16
N_STREAMS = 2 * B_HEADS


def _attn_kernel(q_ref, qn_ref, k_ref, k0n_ref, vt_ref, lam_ref, sgb_ref,
                 o_ref, sa_scr, sb_scr, ma_scr, mb_scr, m_scr, acc_scr, *, lambda_init):
    b = pl.program_id(0)
    i = pl.program_id(1)
    nq = pl.num_programs(1)
    tq = q_ref.shape[0]
    heads = [slice(h * B_VDIM, (h + 1) * B_VDIM) for h in range(B_HEADS)]
    lane = lax.broadcasted_iota(jnp.int32, (tq, B_VDIM), 1)
    zero = jnp.zeros((tq, B_VDIM), BF16)
    key = lax.broadcasted_iota(jnp.int32, (tq, tq), 0)
    qry = lax.broadcasted_iota(jnp.int32, (tq, tq), 1)
    nt = (((1,), (1,)), ((), ()))

    def split_components(qr):
        out = []
        for hs in heads:
            qh = qr[:, hs]
            out += [jnp.where(lane < B_HEAD_DIM, qh, zero), jnp.where(lane >= B_HEAD_DIM, qh, zero)]
        return out

    def scores(qs, kb_of, s_scr, mx_scr, visible=None):
        for h, hs in enumerate(heads):
            kb = kb_of(hs)
            for c in range(2):
                st = 2 * h + c
                s = lax.dot_general(kb, qs[st], nt, preferred_element_type=F32)
                if visible is not None:
                    s = jnp.where(visible, s, -jnp.inf)
                s_scr[st] = s
                mx_scr[st] = jnp.max(s, axis=0, keepdims=True)

    ones_rows = jnp.ones((ONES_ROWS, tq), BF16)

    def consume(j, s_scr, mx_scr, after_head=None, visible=None):
        for h, hs in enumerate(heads):
            vtb = jnp.concatenate([vt_ref[j, hs, :], ones_rows], axis=0)
            for c in range(2):
                st = 2 * h + c
                s = s_scr[st]
                if visible is None:
                    mx = mx_scr[st]
                else:
                    s = jnp.where(visible, s, -jnp.inf)
                    mx = jnp.max(s, axis=0, keepdims=True)
                m_old = m_scr[st]
                m_new = jnp.maximum(m_old, mx)
                alpha = jnp.exp2(m_old - m_new)
                p = jnp.exp2(s - m_new)
                acc_scr[st] = alpha * acc_scr[st] + jnp.dot(vtb, p.astype(BF16),
                                                            preferred_element_type=F32)
                m_scr[st] = m_new
            if after_head is not None:
                after_head(h, hs)

    qs = split_components(q_ref)
    key_tile = lambda j: (lambda hs: k_ref[pl.ds(pl.multiple_of(j * tq, tq), tq), hs])
    causal = key <= qry

    m_scr[...] = jnp.full(m_scr.shape, -jnp.inf, F32)
    acc_scr[...] = jnp.zeros(acc_scr.shape, F32)

    @pl.when((b == 0) & (i == 0))
    def _():
        scores(qs, key_tile(0), sa_scr, ma_scr)

    def pair(t, carry):
        j = 2 * t
        scores(qs, key_tile(j + 1), sb_scr, mb_scr)
        consume(j, sa_scr, ma_scr)
        scores(qs, key_tile(j + 2), sa_scr, ma_scr)
        consume(j + 1, sb_scr, mb_scr)
        return carry

    n_pairs = lax.shift_right_logical(jnp.maximum(i - 1, 0), 1)
    lax.fori_loop(0, n_pairs, pair, 0)
    done = 2 * n_pairs

    lam = (jnp.exp(jnp.sum(lam_ref[0:1, :] * lam_ref[1:2, :], keepdims=True))
           - jnp.exp(jnp.sum(lam_ref[2:3, :] * lam_ref[3:4, :], keepdims=True)) + lambda_init)

    def finish_head(h, hs):
        a0, a1 = acc_scr[2 * h], acc_scr[2 * h + 1]
        l0, l1 = a0[B_VDIM:B_VDIM + 1], a1[B_VDIM:B_VDIM + 1]
        ot = a0[:B_VDIM] * (1.0 / l0) - a1[:B_VDIM] * (lam / l1)
        ms = jnp.mean(ot * ot, axis=0, keepdims=True)
        y = ot * lax.rsqrt(ms + EPS) * sgb_ref[...] * (1.0 - lambda_init)
        o_ref[:, hs] = y.T.astype(BF16)

    def stage_next():
        scores(split_components(qn_ref), lambda hs: k0n_ref[:, hs], sa_scr, ma_scr)

    @pl.when(i == 0)
    def _():
        consume(0, sa_scr, ma_scr, finish_head, causal)
        stage_next()

    @pl.when((i & 1) == 1)
    def _():
        scores(qs, key_tile(i), sb_scr, mb_scr, causal)
        consume(done, sa_scr, ma_scr)
        consume(i, sb_scr, mb_scr, finish_head)
        stage_next()

    @pl.when(((i & 1) == 0) & (i > 0))
    def _():
        scores(qs, key_tile(done + 1), sb_scr, mb_scr)
        consume(done, sa_scr, ma_scr)
        scores(qs, key_tile(i), sa_scr, ma_scr, causal)
        consume(done + 1, sb_scr, mb_scr)
        consume(i, sa_scr, ma_scr, finish_head)
        stage_next()


def _attention(q, k, vt, lam_vecs, sgb, lambda_init, layer):
    n = q.shape[0]
    bsz = n // SEQ
    nq = SEQ // TQ
    last = bsz * nq - 1
    stream = lambda *shape: pltpu.VMEM((N_STREAMS,) + shape, F32)
    return pl.pallas_call(
        functools.partial(_attn_kernel, lambda_init=lambda_init),
        grid=(bsz, nq),
        in_specs=[
            pl.BlockSpec((TQ, B_WIDTH), lambda b, i: (b * nq + i, 0)),
            pl.BlockSpec((TQ, B_WIDTH), lambda b, i: (jnp.minimum(b * nq + i + 1, last), 0)),
            pl.BlockSpec((SEQ, B_WIDTH), lambda b, i: (b, 0)),
            pl.BlockSpec((TQ, B_WIDTH),
                         lambda b, i: (jnp.minimum(b * nq + i + 1, last) // nq * nq, 0)),
            pl.BlockSpec((nq, B_WIDTH, TQ), lambda b, i: (b, 0, 0)),
            _layer_spec(layer, (4, B_HEAD_DIM)),
            _layer_spec(layer, (B_VDIM, TQ)),
        ],
        out_specs=pl.BlockSpec((TQ, B_WIDTH), lambda b, i: (b * nq + i, 0)),
        out_shape=jax.ShapeDtypeStruct((n, B_WIDTH), BF16),
        scratch_shapes=[
            stream(TQ, TQ), stream(TQ, TQ),
            stream(1, TQ), stream(1, TQ),
            stream(1, TQ),
            stream(B_VDIM + ONES_ROWS, TQ),
        ],
        compiler_params=pltpu.CompilerParams(
            dimension_semantics=("arbitrary", "arbitrary"), vmem_limit_bytes=VMEM_LIMIT),
        name=f"attn_l{layer}",
    )(q, q, k, k, vt, lam_vecs, sgb)


def _merge_kernel(ya_ref, yb_ref, yc_ref, gate_ref, x_ref, wb_ref, wo_ref, vec_ref, wfi_ref, wfo_ref,
                  o_ref):
    gpost, gfpre, gfpost = (vec_ref[r:r + 1, :] for r in range(3))
    merged = None
    for j, y_ref in enumerate((ya_ref, yb_ref, yc_ref)):
        up = jnp.dot(y_ref[...], wb_ref[j], preferred_element_type=F32)
        term = gate_ref[:, j * D_MODEL:(j + 1) * D_MODEL] * up
        merged = term if merged is None else merged + term
    mix = jnp.dot(merged.astype(BF16), wo_ref[...], preferred_element_type=F32)
    x1 = x_ref[...] + _rms(mix, gpost)

    h = _rms(x1, gfpre).astype(BF16)
    f = None
    for lo in range(0, D_FF, FF_CHUNK):
        width = min(FF_CHUNK, D_FF - lo)
        g = jnp.dot(h, wfi_ref[:, lo:lo + width], preferred_element_type=F32)
        u = jnp.dot(h, wfi_ref[:, D_FF + lo:D_FF + lo + width], preferred_element_type=F32)
        a = (g * _sigmoid(g) * u).astype(BF16)
        part = jnp.dot(a, wfo_ref[lo:lo + width, :], preferred_element_type=F32)
        f = part if f is None else f + part
    o_ref[...] = x1 + _rms(f, gfpost)


def _merge(ya, yb, yc, gates, x, wb, wo, vecs, wfi, wfo, layer):
    n = x.shape[0]
    tm = TM_MERGE
    row = lambda width: pl.BlockSpec((tm, width), lambda i: (i, 0))
    return pl.pallas_call(
        _merge_kernel,
        grid=(n // tm,),
        in_specs=[
            row(BRANCH_WIDTH), row(BRANCH_WIDTH), row(BRANCH_WIDTH),
            row(N_BRANCH * D_MODEL), row(D_MODEL),
            _layer_spec(layer, (N_BRANCH, BRANCH_WIDTH, D_MODEL)),
            _layer_spec(layer, (D_MODEL, D_MODEL)),
            _layer_spec(layer, (3, D_MODEL)),
            _layer_spec(layer, (D_MODEL, 2 * D_FF)),
            _layer_spec(layer, (D_FF, D_MODEL)),
        ],
        out_specs=row(D_MODEL),
        out_shape=jax.ShapeDtypeStruct((n, D_MODEL), F32),
        compiler_params=pltpu.CompilerParams(
            dimension_semantics=("arbitrary",), vmem_limit_bytes=VMEM_LIMIT),
        name=f"merge_l{layer}",
    )(ya, yb, yc, gates, x, wb, wo, vecs, wfi, wfo)


def kernel(x, norm_mix_pre, w_in, gmlp_norm_g, gmlp_norm_b, gmlp_w_s, gmlp_b_s, lambda_q1, lambda_k1,
           lambda_q2, lambda_k2, diff_subln_g, pool_w, pool_scale, w_branch, w_out, norm_mix_post,
           norm_ffn_pre, w_ffn_in, w_ffn_out, norm_ffn_post):
    bsz, s, d = x.shape
    assert (s, d) == (SEQ, D_MODEL)
    depth = w_in.shape[0]
    xf = x.reshape(bsz * s, d)
    pad = lambda a: jnp.pad(a, ((0, 0), (0, D_MODEL - a.shape[1])))
    proj_vecs = jnp.stack([norm_mix_pre, pad(gmlp_norm_g), pad(gmlp_norm_b), pad(pool_scale)], axis=1)
    lam_vecs = jnp.stack([lambda_q1, lambda_k1, lambda_q2, lambda_k2], axis=1)
    merge_vecs = jnp.stack([norm_mix_post, norm_ffn_pre, norm_ffn_post], axis=1)
    pool_w, w_branch, w_out, w_ffn_in, w_ffn_out = (
        a.astype(BF16) for a in (pool_w, w_branch, w_out, w_ffn_in, w_ffn_out))
    bs = jnp.broadcast_to(gmlp_b_s[:, :, :, None], (depth, A_GROUPS, CHUNK, CHUNK))
    sgb = jnp.broadcast_to(diff_subln_g[:, :, None], (depth, B_VDIM, TQ))
    for l in range(depth):
        lambda_init = 0.8 - 0.6 * math.exp(-0.3 * l)
        ya, q, k, vt, yc, gates = _in_proj(xf, proj_vecs, w_in, gmlp_w_s, bs, pool_w, l)
        yb = _attention(q, k, vt, lam_vecs, sgb, lambda_init, l)
        xf = _merge(ya, yb, yc, gates, xf, w_branch, w_out, merge_vecs, w_ffn_in, w_ffn_out, l)
    return xf.reshape(bsz, s, d)
```

```python
import functools
import math

import jax
import jax.numpy as jnp
from jax import lax
from jax.experimental import pallas as pl
from jax.experimental.pallas import tpu as pltpu

F32 = jnp.float32
BF16 = jnp.bfloat16

D_MODEL = 1024
SEQ = 2048
CHUNK = 128
A_GROUPS = 4
A_WIDTH = 512
B_HEADS = 4
B_HEAD_DIM = 64
B_VDIM = 128
B_WIDTH = 512
C_WINDOWS = (2, 4, 8, 16)
C_GROUP_DIM = 128
C_WIDTH = 512
N_BRANCH = 3
BRANCH_WIDTH = 512
D_FF = 2816
EPS = 1e-6

COL_A = 0
COL_Q = 2 * A_WIDTH
COL_K = COL_Q + 512
COL_V = COL_K + 512
COL_C = COL_V + B_WIDTH
COL_G = COL_C + C_WIDTH
IN_TOTAL = COL_G + N_BRANCH * D_MODEL

HALO = 16
LANES = 128
VMEM_LIMIT = 56 * 1024 * 1024

TM_PROJ = 512
TM_MERGE = 512
TQ = 512
FF_CHUNK = 512


def _rms(x, g):
    ms = jnp.mean(x * x, axis=-1, keepdims=True)
    return x * lax.rsqrt(ms + EPS) * g


def _sigmoid(x):
    return 0.5 * (jnp.tanh(0.5 * x) + 1.0)


def _layer_spec(layer, shape):
    nd = len(shape)
    return pl.BlockSpec((None,) + tuple(shape), lambda *_: (layer,) + (0,) * nd,
                        pipeline_mode=pl.Buffered(1))


def _in_proj_kernel(x_ref, vec_ref, w_ref, ws_ref, bs_ref, pw_ref,
                    ya_ref, q_ref, k_ref, vt_ref, yc_ref, gate_ref, cbuf):
    tm = x_ref.shape[0]
    pos = (pl.program_id(0) * tm) % SEQ
    gpre = vec_ref[0:1, :]
    lng, lnb, psc = (vec_ref[r:r + 1, :A_WIDTH] for r in (1, 2, 3))
    h = _rms(x_ref[...], gpre).astype(BF16)

    def proj(lo, width):
        return jnp.dot(h, w_ref[:, lo:lo + width].astype(BF16), preferred_element_type=F32)

    def gate(j):
        zg = proj(COL_G + j * D_MODEL, D_MODEL)
        gate_ref[:, j * D_MODEL:(j + 1) * D_MODEL] = _sigmoid(zg)


    zc = proj(COL_C, C_WIDTH)
    gate(0)

    @pl.when(pos == 0)
    def _():
        cbuf[0:HALO, :] = jnp.zeros((HALO, C_WIDTH), F32)

    cbuf[HALO:, :] = zc
    p_all = cbuf[...]
    cbuf[0:HALO, :] = p_all[tm:, :]
    s2 = p_all + pltpu.roll(p_all, 1, 0)
    s4 = s2[:, 128:] + pltpu.roll(s2[:, 128:], 2, 0)
    s8 = s4[:, 128:] + pltpu.roll(s4[:, 128:], 4, 0)
    s16 = s8[:, 128:] + pltpu.roll(s8[:, 128:], 8, 0)
    sums = (s2[HALO:, :128], s4[HALO:, :128], s8[HALO:, :128], s16[HALO:, :])
    head_pos = 1 + lax.broadcasted_iota(jnp.int32, (HALO, C_GROUP_DIM), 0)
    pooled = []
    for g, w in enumerate(C_WINDOWS):
        cs = slice(g * C_GROUP_DIM, (g + 1) * C_GROUP_DIM)
        head_cnt = jnp.where(pos == 0, jnp.minimum(head_pos, w), w).astype(F32)
        mean = jnp.concatenate([sums[g][:HALO] * (1.0 / head_cnt), sums[g][HALO:] * (1.0 / w)],
                               axis=0)
        pooled.append((mean - zc[:, cs]).astype(BF16))
    zero_w = jnp.zeros((C_GROUP_DIM, C_GROUP_DIM), BF16)
    for g in range(0, len(C_WINDOWS), 2):
        cs = slice(g * C_GROUP_DIM, (g + 2) * C_GROUP_DIM)
        w_pair = jnp.concatenate([jnp.concatenate([pw_ref[g], zero_w], axis=1),
                                  jnp.concatenate([zero_w, pw_ref[g + 1]], axis=1)], axis=0)
        yc = jnp.dot(jnp.concatenate(pooled[g:g + 2], axis=1), w_pair,
                     preferred_element_type=F32) * psc[:, cs]
        yc_ref[:, cs] = yc.astype(BF16)

    za = proj(COL_A, 2 * A_WIDTH)
    gate(1)
    ga = 0.5 * za * (1.0 + lax.erf(za * math.sqrt(0.5)))
    u = ga[:, :A_WIDTH]
    vv = ga[:, A_WIDTH:]
    mu = jnp.mean(vv, axis=-1, keepdims=True)
    dv = vv - mu
    var = jnp.mean(dv * dv, axis=-1, keepdims=True)
    vn = (dv * lax.rsqrt(var + EPS) * lng + lnb).astype(BF16)
    trow = lax.broadcasted_iota(jnp.int32, (CHUNK, CHUNK), 0)
    tcol = lax.broadcasted_iota(jnp.int32, (CHUNK, CHUNK), 1)
    chunks = [slice(c * CHUNK, (c + 1) * CHUNK) for c in range(tm // CHUNK)]
    for g in range(A_GROUPS):
        wsg = jnp.where(tcol <= trow, ws_ref[g], 0.0).astype(BF16)
        cs = slice(g * CHUNK, (g + 1) * CHUNK)
        v_side = jnp.concatenate([vn[rs, cs] for rs in chunks], axis=1)
        mixed = jnp.dot(wsg, v_side, preferred_element_type=F32)
        for rs, ls in zip(chunks, chunks):
            ya_ref[rs, cs] = (u[rs, cs] * (mixed[:, ls] + bs_ref[g])).astype(BF16)

    gate(2)

    q_ref[...] = (proj(COL_Q, 512) * (B_HEAD_DIM ** -0.5 * math.log2(math.e))).astype(BF16)
    k_ref[...] = proj(COL_K, 512).astype(BF16)
    vt = proj(COL_V, B_WIDTH).T.astype(BF16)
    for j in range(tm // TQ):
        vt_ref[j] = vt[:, j * TQ:(j + 1) * TQ]


def _in_proj(x, vecs, w_in, ws, bs, pw, layer):
    n = x.shape[0]
    tm = TM_PROJ
    row = lambda width: pl.BlockSpec((tm, width), lambda i: (i, 0))
    out_shapes = (
        jax.ShapeDtypeStruct((n, A_WIDTH), BF16),
        jax.ShapeDtypeStruct((n, 512), BF16),
        jax.ShapeDtypeStruct((n, 512), BF16),
        jax.ShapeDtypeStruct((n // TQ, B_WIDTH, TQ), BF16),
        jax.ShapeDtypeStruct((n, C_WIDTH), BF16),
        jax.ShapeDtypeStruct((n, N_BRANCH * D_MODEL), F32),
    )
    return pl.pallas_call(
        _in_proj_kernel,
        grid=(n // tm,),
        in_specs=[
            row(D_MODEL),
            _layer_spec(layer, (4, D_MODEL)),
            _layer_spec(layer, (D_MODEL, IN_TOTAL)),
            _layer_spec(layer, (A_GROUPS, CHUNK, CHUNK)),
            _layer_spec(layer, (A_GROUPS, CHUNK, CHUNK)),
            _layer_spec(layer, (len(C_WINDOWS), C_GROUP_DIM, C_GROUP_DIM)),
        ],
        out_specs=[row(A_WIDTH), row(512), row(512),
                   pl.BlockSpec((tm // TQ, B_WIDTH, TQ), lambda i: (i, 0, 0)), row(C_WIDTH),
                   row(N_BRANCH * D_MODEL)],
        out_shape=out_shapes,
        scratch_shapes=[pltpu.VMEM((HALO + tm, C_WIDTH), F32)],
        compiler_params=pltpu.CompilerParams(
            dimension_semantics=("arbitrary",), vmem_limit_bytes=VMEM_LIMIT),
        name=f"in_proj_l{layer}",
    )(x, vecs, w_in, ws, bs, pw)


ONES_ROWS = 16
N_STREAMS = 2 * B_HEADS


def _attn_kernel(q_ref, qn_ref, k_ref, k0n_ref, vt_ref, lam_ref, sgb_ref,
                 o_ref, sa_scr, sb_scr, ma_scr, mb_scr, m_scr, acc_scr, *, lambda_init):
    b = pl.program_id(0)
    i = pl.program_id(1)
    nq = pl.num_programs(1)
    tq = q_ref.shape[0]
    heads = [slice(h * B_VDIM, (h + 1) * B_VDIM) for h in range(B_HEADS)]
    lane = lax.broadcasted_iota(jnp.int32, (tq, B_VDIM), 1)
    zero = jnp.zeros((tq, B_VDIM), BF16)
    nt = (((1,), (1,)), ((), ()))

    def split_components(qr):
        out = []
        for hs in heads:
            qh = qr[:, hs]
            out += [jnp.where(lane < B_HEAD_DIM, qh, zero), jnp.where(lane >= B_HEAD_DIM, qh, zero)]
        return out

    def scores(qs, kb_of, s_scr, mx_scr):
        for h, hs in enumerate(heads):
            kb = kb_of(hs)
            for c in range(2):
                st = 2 * h + c
                s = lax.dot_general(kb, qs[st], nt, preferred_element_type=F32)
                s_scr[st] = s
                mx_scr[st] = jnp.max(s, axis=0, keepdims=True)

    ones_rows = jnp.ones((ONES_ROWS, tq), BF16)

    def consume(j, s_scr, mx_scr):
        for h, hs in enumerate(heads):
            vtb = jnp.concatenate([vt_ref[j, hs, :], ones_rows], axis=0)
            for c in range(2):
                st = 2 * h + c
                m_old = m_scr[st]
                m_new = jnp.maximum(m_old, mx_scr[st])
                alpha = jnp.exp2(m_old - m_new)
                p = jnp.exp2(s_scr[st] - m_new)
                acc_scr[st] = alpha * acc_scr[st] + jnp.dot(vtb, p.astype(BF16),
                                                            preferred_element_type=F32)
                m_scr[st] = m_new

    half = tq // 2
    causal_edge = lambda shape: (lax.broadcasted_iota(jnp.int32, shape, 0)
                                 <= lax.broadcasted_iota(jnp.int32, shape, 1))
    vis_top = causal_edge((half, tq))
    vis_bot = causal_edge((half, half))

    future = jnp.full((half, half), -jnp.inf, F32)
    ones_half = jnp.ones((ONES_ROWS, half), BF16)

    def diag_max(s_top, s_bot):
        s_bot_wide = jnp.concatenate([future, s_bot], axis=1)
        return jnp.maximum(jnp.max(s_top, axis=0, keepdims=True),
                           jnp.max(s_bot_wide, axis=0, keepdims=True))

    def scores_diag(qs, s_scr, mx_scr):
        top = pl.multiple_of(i * tq, tq)
        bot = pl.multiple_of(i * tq + half, half)
        for h, hs in enumerate(heads):
            k_top, k_bot = k_ref[pl.ds(top, half), hs], k_ref[pl.ds(bot, half), hs]
            for c in range(2):
                st = 2 * h + c
                s_top = lax.dot_general(k_top, qs[st], nt, preferred_element_type=F32)
                s_bot = lax.dot_general(k_bot, qs[st][half:], nt, preferred_element_type=F32)
                s_top = jnp.where(vis_top, s_top, -jnp.inf)
                s_bot = jnp.where(vis_bot, s_bot, -jnp.inf)
                s_scr[st, 0:half, :] = s_top
                s_scr[st, half:, half:] = s_bot
                mx_scr[st] = diag_max(s_top, s_bot)

    def consume_diag(s_scr, mx_scr, mask_on_use=False):
        for h, hs in enumerate(heads):
            vt_top = jnp.concatenate([vt_ref[i, hs, 0:half], ones_half], axis=0)
            vt_bot = jnp.concatenate([vt_ref[i, hs, half:], ones_half], axis=0)
            for c in range(2):
                st = 2 * h + c
                s_top, s_bot = s_scr[st, 0:half, :], s_scr[st, half:, half:]
                if mask_on_use:
                    s_top = jnp.where(vis_top, s_top, -jnp.inf)
                    s_bot = jnp.where(vis_bot, s_bot, -jnp.inf)
                    mx = diag_max(s_top, s_bot)
                else:
                    mx = mx_scr[st]
                m_old = m_scr[st]
                m_new = jnp.maximum(m_old, mx)
                alpha = jnp.exp2(m_old - m_new)
                p_top = jnp.exp2(s_top - m_new).astype(BF16)
                m_scr[st] = m_new
                p_bot = jnp.exp2(s_bot - m_scr[st, :, half:]).astype(BF16)
                a = alpha * acc_scr[st] + jnp.dot(vt_top, p_top, preferred_element_type=F32)
                a_right = a[:, half:] + jnp.dot(vt_bot, p_bot, preferred_element_type=F32)
                acc_scr[st] = jnp.concatenate([a[:, :half], a_right], axis=1)
            finish_head(h, hs)

    qs = split_components(q_ref)
    key_tile = lambda j: (lambda hs: k_ref[pl.ds(pl.multiple_of(j * tq, tq), tq), hs])

    m_scr[...] = jnp.full(m_scr.shape, -jnp.inf, F32)
    acc_scr[...] = jnp.zeros(acc_scr.shape, F32)

    @pl.when((b == 0) & (i == 0))
    def _():
        scores(qs, key_tile(0), sa_scr, ma_scr)

    def pair(t, carry):
        j = 2 * t
        scores(qs, key_tile(j + 1), sb_scr, mb_scr)
        consume(j, sa_scr, ma_scr)
        scores(qs, key_tile(j + 2), sa_scr, ma_scr)
        consume(j + 1, sb_scr, mb_scr)
        return carry

    n_pairs = lax.shift_right_logical(jnp.maximum(i - 1, 0), 1)
    lax.fori_loop(0, n_pairs, pair, 0)
    done = 2 * n_pairs

    lam = (jnp.exp(jnp.sum(lam_ref[0:1, :] * lam_ref[1:2, :], keepdims=True))
           - jnp.exp(jnp.sum(lam_ref[2:3, :] * lam_ref[3:4, :], keepdims=True)) + lambda_init)

    def finish_head(h, hs):
        a0, a1 = acc_scr[2 * h], acc_scr[2 * h + 1]
        l0, l1 = a0[B_VDIM:B_VDIM + 1], a1[B_VDIM:B_VDIM + 1]
        ot = a0[:B_VDIM] * (1.0 / l0) - a1[:B_VDIM] * (lam / l1)
        ms = jnp.mean(ot * ot, axis=0, keepdims=True)
        y = ot * lax.rsqrt(ms + EPS) * sgb_ref[...] * (1.0 - lambda_init)
        o_ref[:, hs] = y.T.astype(BF16)

    def stage_next():
        scores(split_components(qn_ref), lambda hs: k0n_ref[:, hs], sa_scr, ma_scr)

    @pl.when(i == 0)
    def _():
        consume_diag(sa_scr, ma_scr, mask_on_use=True)
        stage_next()

    @pl.when((i & 1) == 1)
    def _():
        scores_diag(qs, sb_scr, mb_scr)
        consume(done, sa_scr, ma_scr)
        consume_diag(sb_scr, mb_scr)
        stage_next()

    @pl.when(((i & 1) == 0) & (i > 0))
    def _():
        scores(qs, key_tile(done + 1), sb_scr, mb_scr)
        consume(done, sa_scr, ma_scr)
        scores_diag(qs, sa_scr, ma_scr)
        consume(done + 1, sb_scr, mb_scr)
        consume_diag(sa_scr, ma_scr)
        stage_next()


def _attention(q, k, vt, lam_vecs, sgb, lambda_init, layer):
    n = q.shape[0]
    bsz = n // SEQ
    nq = SEQ // TQ
    last = bsz * nq - 1
    stream = lambda *shape: pltpu.VMEM((N_STREAMS,) + shape, F32)
    return pl.pallas_call(
        functools.partial(_attn_kernel, lambda_init=lambda_init),
        grid=(bsz, nq),
        in_specs=[
            pl.BlockSpec((TQ, B_WIDTH), lambda b, i: (b * nq + i, 0)),
            pl.BlockSpec((TQ, B_WIDTH), lambda b, i: (jnp.minimum(b * nq + i + 1, last), 0)),
            pl.BlockSpec((SEQ, B_WIDTH), lambda b, i: (b, 0)),
            pl.BlockSpec((TQ, B_WIDTH),
                         lambda b, i: (jnp.minimum(b * nq + i + 1, last) // nq * nq, 0)),
            pl.BlockSpec((nq, B_WIDTH, TQ), lambda b, i: (b, 0, 0)),
            _layer_spec(layer, (4, B_HEAD_DIM)),
            _layer_spec(layer, (B_VDIM, TQ)),
        ],
        out_specs=pl.BlockSpec((TQ, B_WIDTH), lambda b, i: (b * nq + i, 0)),
        out_shape=jax.ShapeDtypeStruct((n, B_WIDTH), BF16),
        scratch_shapes=[
            stream(TQ, TQ), stream(TQ, TQ),
            stream(1, TQ), stream(1, TQ),
            stream(1, TQ),
            stream(B_VDIM + ONES_ROWS, TQ),
        ],
        compiler_params=pltpu.CompilerParams(
            dimension_semantics=("arbitrary", "arbitrary"), vmem_limit_bytes=VMEM_LIMIT),
        name=f"attn_l{layer}",
    )(q, q, k, k, vt, lam_vecs, sgb)


def _merge_kernel(ya_ref, yb_ref, yc_ref, gate_ref, x_ref, wb_ref, wo_ref, vec_ref, wfi_ref, wfo_ref,
                  o_ref):
    gpost, gfpre, gfpost = (vec_ref[r:r + 1, :] for r in range(3))
    merged = None
    for j, y_ref in enumerate((ya_ref, yb_ref, yc_ref)):
        up = jnp.dot(y_ref[...], wb_ref[j], preferred_element_type=F32)
        term = gate_ref[:, j * D_MODEL:(j + 1) * D_MODEL] * up
        merged = term if merged is None else merged + term
    mix = jnp.dot(merged.astype(BF16), wo_ref[...], preferred_element_type=F32)
    x1 = x_ref[...] + _rms(mix, gpost)

    h = _rms(x1, gfpre).astype(BF16)
    f = None
    for lo in range(0, D_FF, FF_CHUNK):
        width = min(FF_CHUNK, D_FF - lo)
        g = jnp.dot(h, wfi_ref[:, lo:lo + width], preferred_element_type=F32)
        u = jnp.dot(h, wfi_ref[:, D_FF + lo:D_FF + lo + width], preferred_element_type=F32)
        a = (g * _sigmoid(g) * u).astype(BF16)
        part = jnp.dot(a, wfo_ref[lo:lo + width, :], preferred_element_type=F32)
        f = part if f is None else f + part
    o_ref[...] = x1 + _rms(f, gfpost)


def _merge(ya, yb, yc, gates, x, wb, wo, vecs, wfi, wfo, layer):
    n = x.shape[0]
    tm = TM_MERGE
    row = lambda width: pl.BlockSpec((tm, width), lambda i: (i, 0))
    return pl.pallas_call(
        _merge_kernel,
        grid=(n // tm,),
        in_specs=[
            row(BRANCH_WIDTH), row(BRANCH_WIDTH), row(BRANCH_WIDTH),
            row(N_BRANCH * D_MODEL), row(D_MODEL),
            _layer_spec(layer, (N_BRANCH, BRANCH_WIDTH, D_MODEL)),
            _layer_spec(layer, (D_MODEL, D_MODEL)),
            _layer_spec(layer, (3, D_MODEL)),
            _layer_spec(layer, (D_MODEL, 2 * D_FF)),
            _layer_spec(layer, (D_FF, D_MODEL)),
        ],
        out_specs=row(D_MODEL),
        out_shape=jax.ShapeDtypeStruct((n, D_MODEL), F32),
        compiler_params=pltpu.CompilerParams(
            dimension_semantics=("arbitrary",), vmem_limit_bytes=VMEM_LIMIT),
        name=f"merge_l{layer}",
    )(ya, yb, yc, gates, x, wb, wo, vecs, wfi, wfo)


def kernel(x, norm_mix_pre, w_in, gmlp_norm_g, gmlp_norm_b, gmlp_w_s, gmlp_b_s, lambda_q1, lambda_k1,
           lambda_q2, lambda_k2, diff_subln_g, pool_w, pool_scale, w_branch, w_out, norm_mix_post,
           norm_ffn_pre, w_ffn_in, w_ffn_out, norm_ffn_post):
    bsz, s, d = x.shape
    assert (s, d) == (SEQ, D_MODEL)
    depth = w_in.shape[0]
    xf = x.reshape(bsz * s, d)
    pad = lambda a: jnp.pad(a, ((0, 0), (0, D_MODEL - a.shape[1])))
    proj_vecs = jnp.stack([norm_mix_pre, pad(gmlp_norm_g), pad(gmlp_norm_b), pad(pool_scale)], axis=1)
    lam_vecs = jnp.stack([lambda_q1, lambda_k1, lambda_q2, lambda_k2], axis=1)
    merge_vecs = jnp.stack([norm_mix_post, norm_ffn_pre, norm_ffn_post], axis=1)
    pool_w, w_branch, w_out, w_ffn_in, w_ffn_out = (
        a.astype(BF16) for a in (pool_w, w_branch, w_out, w_ffn_in, w_ffn_out))
    bs = jnp.broadcast_to(gmlp_b_s[:, :, :, None], (depth, A_GROUPS, CHUNK, CHUNK))
    sgb = jnp.broadcast_to(diff_subln_g[:, :, None], (depth, B_VDIM, TQ))
    for l in range(depth):
        lambda_init = 0.8 - 0.6 * math.exp(-0.3 * l)
        ya, q, k, vt, yc, gates = _in_proj(xf, proj_vecs, w_in, gmlp_w_s, bs, pool_w, l)
        yb = _attention(q, k, vt, lam_vecs, sgb, lambda_init, l)
        xf = _merge(ya, yb, yc, gates, xf, w_branch, w_out, merge_vecs, w_ffn_in, w_ffn_out, l)
    return xf.reshape(bsz, s, d)
```

```python
import functools
import math

import jax
import jax.numpy as jnp
from jax import lax
from jax.experimental import pallas as pl
from jax.experimental.pallas import tpu as pltpu

F32 = jnp.float32
BF16 = jnp.bfloat16

D_MODEL = 1024
SEQ = 2048
CHUNK = 128
A_GROUPS = 4
A_WIDTH = 512
B_HEADS = 4
B_HEAD_DIM = 64
B_VDIM = 128
B_WIDTH = 512
C_WINDOWS = (2, 4, 8, 16)
C_GROUP_DIM = 128
C_WIDTH = 512
N_BRANCH = 3
BRANCH_WIDTH = 512
D_FF = 2816
EPS = 1e-6

COL_A = 0
COL_Q = 2 * A_WIDTH
COL_K = COL_Q + 512
COL_V = COL_K + 512
COL_C = COL_V + B_WIDTH
COL_G = COL_C + C_WIDTH
IN_TOTAL = COL_G + N_BRANCH * D_MODEL

HALO = 16
LANES = 128
VMEM_LIMIT = 58 * 1024 * 1024

TM_PROJ = 512
TM_MERGE = 512
TQ = 512
FF_CHUNK = 512


def _rms(x, g):
    ms = jnp.mean(x * x, axis=-1, keepdims=True)
    return x * lax.rsqrt(ms + EPS) * g


def _sigmoid(x):
    return 0.5 * (jnp.tanh(0.5 * x) + 1.0)


def _resident_spec(shape):
    nd = len(shape)
    return pl.BlockSpec(tuple(shape), lambda *_: (0,) * nd, pipeline_mode=pl.Buffered(1))


def _layer_spec(layer, shape):
    nd = len(shape)
    return pl.BlockSpec((None,) + tuple(shape), lambda *_: (layer,) + (0,) * nd,
                        pipeline_mode=pl.Buffered(1))


def _in_proj_kernel(x_ref, vec_ref, w_ref, ws_ref, bs_ref, pw_ref, wb32_ref, wo32_ref, wfi32_ref,
                    wfo32_ref, ya_ref, q_ref, k_ref, vt_ref, yc_ref, gate_ref, wb_ref, wo_ref,
                    wfi_ref, wfo_ref, cbuf):
    for src, dst in ((wb32_ref, wb_ref), (wo32_ref, wo_ref), (wfi32_ref, wfi_ref),
                     (wfo32_ref, wfo_ref)):
        dst[...] = src[...].astype(BF16)

    tm = x_ref.shape[0]
    pos = (pl.program_id(0) * tm) % SEQ
    gpre = vec_ref[0:1, :]
    lng, lnb, psc = (vec_ref[r:r + 1, :A_WIDTH] for r in (1, 2, 3))
    h = _rms(x_ref[...], gpre).astype(BF16)

    def proj(lo, width):
        return jnp.dot(h, w_ref[:, lo:lo + width].astype(BF16), preferred_element_type=F32)

    def gate(j):
        zg = proj(COL_G + j * D_MODEL, D_MODEL)
        gate_ref[:, j * D_MODEL:(j + 1) * D_MODEL] = _sigmoid(zg)


    zc = proj(COL_C, C_WIDTH)
    gate(0)

    @pl.when(pos == 0)
    def _():
        cbuf[0:HALO, :] = jnp.zeros((HALO, C_WIDTH), F32)

    cbuf[HALO:, :] = zc
    p_all = cbuf[...]
    cbuf[0:HALO, :] = p_all[tm:, :]
    s2 = p_all + pltpu.roll(p_all, 1, 0)
    s4 = s2[:, 128:] + pltpu.roll(s2[:, 128:], 2, 0)
    s8 = s4[:, 128:] + pltpu.roll(s4[:, 128:], 4, 0)
    s16 = s8[:, 128:] + pltpu.roll(s8[:, 128:], 8, 0)
    sums = (s2[HALO:, :128], s4[HALO:, :128], s8[HALO:, :128], s16[HALO:, :])
    head_pos = 1 + lax.broadcasted_iota(jnp.int32, (HALO, C_GROUP_DIM), 0)
    pooled = []
    for g, w in enumerate(C_WINDOWS):
        cs = slice(g * C_GROUP_DIM, (g + 1) * C_GROUP_DIM)
        head_cnt = jnp.where(pos == 0, jnp.minimum(head_pos, w), w).astype(F32)
        mean = jnp.concatenate([sums[g][:HALO] * (1.0 / head_cnt), sums[g][HALO:] * (1.0 / w)],
                               axis=0)
        pooled.append((mean - zc[:, cs]).astype(BF16))
    zero_w = jnp.zeros((C_GROUP_DIM, C_GROUP_DIM), BF16)
    for g in range(0, len(C_WINDOWS), 2):
        cs = slice(g * C_GROUP_DIM, (g + 2) * C_GROUP_DIM)
        w_pair = jnp.concatenate([jnp.concatenate([pw_ref[g], zero_w], axis=1),
                                  jnp.concatenate([zero_w, pw_ref[g + 1]], axis=1)], axis=0)
        yc = jnp.dot(jnp.concatenate(pooled[g:g + 2], axis=1), w_pair,
                     preferred_element_type=F32) * psc[:, cs]
        yc_ref[:, cs] = yc.astype(BF16)

    za = proj(COL_A, 2 * A_WIDTH)
    gate(1)
    ga = 0.5 * za * (1.0 + lax.erf(za * math.sqrt(0.5)))
    u = ga[:, :A_WIDTH]
    vv = ga[:, A_WIDTH:]
    mu = jnp.mean(vv, axis=-1, keepdims=True)
    dv = vv - mu
    var = jnp.mean(dv * dv, axis=-1, keepdims=True)
    vn = (dv * lax.rsqrt(var + EPS) * lng + lnb).astype(BF16)
    trow = lax.broadcasted_iota(jnp.int32, (CHUNK, CHUNK), 0)
    tcol = lax.broadcasted_iota(jnp.int32, (CHUNK, CHUNK), 1)
    chunks = [slice(c * CHUNK, (c + 1) * CHUNK) for c in range(tm // CHUNK)]
    for g in range(A_GROUPS):
        wsg = jnp.where(tcol <= trow, ws_ref[g], 0.0).astype(BF16)
        cs = slice(g * CHUNK, (g + 1) * CHUNK)
        v_side = jnp.concatenate([vn[rs, cs] for rs in chunks], axis=1)
        mixed = jnp.dot(wsg, v_side, preferred_element_type=F32)
        for rs, ls in zip(chunks, chunks):
            ya_ref[rs, cs] = (u[rs, cs] * (mixed[:, ls] + bs_ref[g])).astype(BF16)

    gate(2)

    q_ref[...] = (proj(COL_Q, 512) * (B_HEAD_DIM ** -0.5 * math.log2(math.e))).astype(BF16)
    k_ref[...] = proj(COL_K, 512).astype(BF16)
    vt = proj(COL_V, B_WIDTH).T.astype(BF16)
    for j in range(tm // TQ):
        vt_ref[j] = vt[:, j * TQ:(j + 1) * TQ]


BF16_SUBLANES = 16


def _in_proj(x, vecs, w_in, ws, bs, pw, merge_weights, layer):
    n = x.shape[0]
    tm = TM_PROJ
    steps = n // tm
    row = lambda width: pl.BlockSpec((tm, width), lambda i: (i, 0))
    out_shapes = [
        jax.ShapeDtypeStruct((n, A_WIDTH), BF16),
        jax.ShapeDtypeStruct((n, 512), BF16),
        jax.ShapeDtypeStruct((n, 512), BF16),
        jax.ShapeDtypeStruct((n // TQ, B_WIDTH, TQ), BF16),
        jax.ShapeDtypeStruct((n, C_WIDTH), BF16),
        jax.ShapeDtypeStruct((n, N_BRANCH * D_MODEL), F32),
    ]
    cast_in, cast_out = [], []
    for w in merge_weights:
        rows, cols = w.shape[1:]
        every = 1
        while rows * every % (steps * BF16_SUBLANES):
            every *= 2
        slab = rows * every // steps
        cast_in.append(pl.BlockSpec((None, slab, cols), lambda i, e=every: (layer, i // e, 0)))
        cast_out.append(pl.BlockSpec((slab, cols), lambda i, e=every: (i // e, 0)))
        out_shapes.append(jax.ShapeDtypeStruct((rows, cols), BF16))
    return pl.pallas_call(
        _in_proj_kernel,
        grid=(steps,),
        in_specs=[
            row(D_MODEL),
            _layer_spec(layer, (4, D_MODEL)),
            _layer_spec(layer, (D_MODEL, IN_TOTAL)),
            _layer_spec(layer, (A_GROUPS, CHUNK, CHUNK)),
            _layer_spec(layer, (A_GROUPS, CHUNK, CHUNK)),
            _layer_spec(layer, (len(C_WINDOWS), C_GROUP_DIM, C_GROUP_DIM)),
        ] + cast_in,
        out_specs=[row(A_WIDTH), row(512), row(512),
                   pl.BlockSpec((tm // TQ, B_WIDTH, TQ), lambda i: (i, 0, 0)), row(C_WIDTH),
                   row(N_BRANCH * D_MODEL)] + cast_out,
        out_shape=out_shapes,
        scratch_shapes=[pltpu.VMEM((HALO + tm, C_WIDTH), F32)],
        compiler_params=pltpu.CompilerParams(
            dimension_semantics=("arbitrary",), vmem_limit_bytes=VMEM_LIMIT),
        name=f"in_proj_l{layer}",
    )(x, vecs, w_in, ws, bs, pw, *merge_weights)


ONES_ROWS = 16
N_STREAMS = 2 * B_HEADS


def _attn_kernel(q_ref, qn_ref, k_ref, k0n_ref, vt_ref, lam_ref, sgb_ref,
                 o_ref, sa_scr, sb_scr, ma_scr, mb_scr, m_scr, acc_scr, *, lambda_init):
    b = pl.program_id(0)
    i = pl.program_id(1)
    nq = pl.num_programs(1)
    tq = q_ref.shape[0]
    heads = [slice(h * B_VDIM, (h + 1) * B_VDIM) for h in range(B_HEADS)]
    lane = lax.broadcasted_iota(jnp.int32, (tq, B_VDIM), 1)
    zero = jnp.zeros((tq, B_VDIM), BF16)
    nt = (((1,), (1,)), ((), ()))

    def split_components(qr):
        out = []
        for hs in heads:
            qh = qr[:, hs]
            out += [jnp.where(lane < B_HEAD_DIM, qh, zero), jnp.where(lane >= B_HEAD_DIM, qh, zero)]
        return out

    def scores(qs, kb_of, s_scr, mx_scr):
        for h, hs in enumerate(heads):
            kb = kb_of(hs)
            for c in range(2):
                st = 2 * h + c
                s = lax.dot_general(kb, qs[st], nt, preferred_element_type=F32)
                s_scr[st] = s
                mx_scr[st] = jnp.max(s, axis=0, keepdims=True)

    ones_rows = jnp.ones((ONES_ROWS, tq), BF16)

    def consume(j, s_scr, mx_scr):
        for h, hs in enumerate(heads):
            vtb = jnp.concatenate([vt_ref[j, hs, :], ones_rows], axis=0)
            for c in range(2):
                st = 2 * h + c
                m_old = m_scr[st]
                m_new = jnp.maximum(m_old, mx_scr[st])
                alpha = jnp.exp2(m_old - m_new)
                p = jnp.exp2(s_scr[st] - m_new)
                acc_scr[st] = alpha * acc_scr[st] + jnp.dot(vtb, p.astype(BF16),
                                                            preferred_element_type=F32)
                m_scr[st] = m_new

    half = tq // 2
    causal_edge = lambda shape: (lax.broadcasted_iota(jnp.int32, shape, 0)
                                 <= lax.broadcasted_iota(jnp.int32, shape, 1))
    vis_top = causal_edge((half, tq))
    vis_bot = causal_edge((half, half))

    future = jnp.full((half, half), -jnp.inf, F32)
    ones_half = jnp.ones((ONES_ROWS, half), BF16)

    def diag_max(s_top, s_bot):
        s_bot_wide = jnp.concatenate([future, s_bot], axis=1)
        return jnp.maximum(jnp.max(s_top, axis=0, keepdims=True),
                           jnp.max(s_bot_wide, axis=0, keepdims=True))

    def scores_diag(qs, s_scr, mx_scr):
        top = pl.multiple_of(i * tq, tq)
        bot = pl.multiple_of(i * tq + half, half)
        for h, hs in enumerate(heads):
            k_top, k_bot = k_ref[pl.ds(top, half), hs], k_ref[pl.ds(bot, half), hs]
            for c in range(2):
                st = 2 * h + c
                s_top = lax.dot_general(k_top, qs[st], nt, preferred_element_type=F32)
                s_bot = lax.dot_general(k_bot, qs[st][half:], nt, preferred_element_type=F32)
                s_top = jnp.where(vis_top, s_top, -jnp.inf)
                s_bot = jnp.where(vis_bot, s_bot, -jnp.inf)
                s_scr[st, 0:half, :] = s_top
                s_scr[st, half:, half:] = s_bot
                mx_scr[st] = diag_max(s_top, s_bot)

    def consume_diag(s_scr, mx_scr, mask_on_use=False):
        for h, hs in enumerate(heads):
            vt_top = jnp.concatenate([vt_ref[i, hs, 0:half], ones_half], axis=0)
            vt_bot = jnp.concatenate([vt_ref[i, hs, half:], ones_half], axis=0)
            for c in range(2):
                st = 2 * h + c
                s_top, s_bot = s_scr[st, 0:half, :], s_scr[st, half:, half:]
                if mask_on_use:
                    s_top = jnp.where(vis_top, s_top, -jnp.inf)
                    s_bot = jnp.where(vis_bot, s_bot, -jnp.inf)
                    mx = diag_max(s_top, s_bot)
                else:
                    mx = mx_scr[st]
                m_old = m_scr[st]
                m_new = jnp.maximum(m_old, mx)
                alpha = jnp.exp2(m_old - m_new)
                p_top = jnp.exp2(s_top - m_new).astype(BF16)
                m_scr[st] = m_new
                p_bot = jnp.exp2(s_bot - m_scr[st, :, half:]).astype(BF16)
                a = alpha * acc_scr[st] + jnp.dot(vt_top, p_top, preferred_element_type=F32)
                a_right = a[:, half:] + jnp.dot(vt_bot, p_bot, preferred_element_type=F32)
                acc_scr[st] = jnp.concatenate([a[:, :half], a_right], axis=1)
            finish_head(h, hs)

    qs = split_components(q_ref)
    key_tile = lambda j: (lambda hs: k_ref[pl.ds(pl.multiple_of(j * tq, tq), tq), hs])

    m_scr[...] = jnp.full(m_scr.shape, -jnp.inf, F32)
    acc_scr[...] = jnp.zeros(acc_scr.shape, F32)

    @pl.when((b == 0) & (i == 0))
    def _():
        scores(qs, key_tile(0), sa_scr, ma_scr)

    def pair(t, carry):
        j = 2 * t
        scores(qs, key_tile(j + 1), sb_scr, mb_scr)
        consume(j, sa_scr, ma_scr)
        scores(qs, key_tile(j + 2), sa_scr, ma_scr)
        consume(j + 1, sb_scr, mb_scr)
        return carry

    n_pairs = lax.shift_right_logical(jnp.maximum(i - 1, 0), 1)
    lax.fori_loop(0, n_pairs, pair, 0)
    done = 2 * n_pairs

    lam = (jnp.exp(jnp.sum(lam_ref[0:1, :] * lam_ref[1:2, :], keepdims=True))
           - jnp.exp(jnp.sum(lam_ref[2:3, :] * lam_ref[3:4, :], keepdims=True)) + lambda_init)

    def finish_head(h, hs):
        a0, a1 = acc_scr[2 * h], acc_scr[2 * h + 1]
        l0, l1 = a0[B_VDIM:B_VDIM + 1], a1[B_VDIM:B_VDIM + 1]
        ot = a0[:B_VDIM] * (1.0 / l0) - a1[:B_VDIM] * (lam / l1)
        ms = jnp.mean(ot * ot, axis=0, keepdims=True)
        y = ot * lax.rsqrt(ms + EPS) * sgb_ref[...] * (1.0 - lambda_init)
        o_ref[:, hs] = y.T.astype(BF16)

    def stage_next():
        scores(split_components(qn_ref), lambda hs: k0n_ref[:, hs], sa_scr, ma_scr)

    @pl.when(i == 0)
    def _():
        consume_diag(sa_scr, ma_scr, mask_on_use=True)
        stage_next()

    @pl.when((i & 1) == 1)
    def _():
        scores_diag(qs, sb_scr, mb_scr)
        consume(done, sa_scr, ma_scr)
        consume_diag(sb_scr, mb_scr)
        stage_next()

    @pl.when(((i & 1) == 0) & (i > 0))
    def _():
        scores(qs, key_tile(done + 1), sb_scr, mb_scr)
        consume(done, sa_scr, ma_scr)
        scores_diag(qs, sa_scr, ma_scr)
        consume(done + 1, sb_scr, mb_scr)
        consume_diag(sa_scr, ma_scr)
        stage_next()


def _attention(q, k, vt, lam_vecs, sgb, lambda_init, layer):
    n = q.shape[0]
    bsz = n // SEQ
    nq = SEQ // TQ
    last = bsz * nq - 1
    stream = lambda *shape: pltpu.VMEM((N_STREAMS,) + shape, F32)
    return pl.pallas_call(
        functools.partial(_attn_kernel, lambda_init=lambda_init),
        grid=(bsz, nq),
        in_specs=[
            pl.BlockSpec((TQ, B_WIDTH), lambda b, i: (b * nq + i, 0)),
            pl.BlockSpec((TQ, B_WIDTH), lambda b, i: (jnp.minimum(b * nq + i + 1, last), 0)),
            pl.BlockSpec((SEQ, B_WIDTH), lambda b, i: (b, 0)),
            pl.BlockSpec((TQ, B_WIDTH),
                         lambda b, i: (jnp.minimum(b * nq + i + 1, last) // nq * nq, 0)),
            pl.BlockSpec((nq, B_WIDTH, TQ), lambda b, i: (b, 0, 0)),
            _layer_spec(layer, (4, B_HEAD_DIM)),
            _layer_spec(layer, (B_VDIM, TQ)),
        ],
        out_specs=pl.BlockSpec((TQ, B_WIDTH), lambda b, i: (b * nq + i, 0)),
        out_shape=jax.ShapeDtypeStruct((n, B_WIDTH), BF16),
        scratch_shapes=[
            stream(TQ, TQ), stream(TQ, TQ),
            stream(1, TQ), stream(1, TQ),
            stream(1, TQ),
            stream(B_VDIM + ONES_ROWS, TQ),
        ],
        compiler_params=pltpu.CompilerParams(
            dimension_semantics=("arbitrary", "arbitrary"), vmem_limit_bytes=VMEM_LIMIT),
        name=f"attn_l{layer}",
    )(q, q, k, k, vt, lam_vecs, sgb)


def _merge_kernel(ya_ref, yb_ref, yc_ref, gate_ref, x_ref, wb_ref, wo_ref, vec_ref, wfi_ref, wfo_ref,
                  o_ref):
    gpost, gfpre, gfpost = (vec_ref[r:r + 1, :] for r in range(3))
    merged = None
    for j, y_ref in enumerate((ya_ref, yb_ref, yc_ref)):
        up = jnp.dot(y_ref[...], wb_ref[j * BRANCH_WIDTH:(j + 1) * BRANCH_WIDTH, :],
                     preferred_element_type=F32)
        term = gate_ref[:, j * D_MODEL:(j + 1) * D_MODEL] * up
        merged = term if merged is None else merged + term
    mix = jnp.dot(merged.astype(BF16), wo_ref[...], preferred_element_type=F32)
    x1 = x_ref[...] + _rms(mix, gpost)

    h = _rms(x1, gfpre).astype(BF16)
    f = None
    for lo in range(0, D_FF, FF_CHUNK):
        width = min(FF_CHUNK, D_FF - lo)
        g = jnp.dot(h, wfi_ref[:, lo:lo + width], preferred_element_type=F32)
        u = jnp.dot(h, wfi_ref[:, D_FF + lo:D_FF + lo + width], preferred_element_type=F32)
        a = (g * _sigmoid(g) * u).astype(BF16)
        part = jnp.dot(a, wfo_ref[lo:lo + width, :], preferred_element_type=F32)
        f = part if f is None else f + part
    o_ref[...] = x1 + _rms(f, gfpost)


def _merge(ya, yb, yc, gates, x, wb, wo, vecs, wfi, wfo, layer):
    n = x.shape[0]
    tm = TM_MERGE
    row = lambda width: pl.BlockSpec((tm, width), lambda i: (i, 0))
    return pl.pallas_call(
        _merge_kernel,
        grid=(n // tm,),
        in_specs=[
            row(BRANCH_WIDTH), row(BRANCH_WIDTH), row(BRANCH_WIDTH),
            row(N_BRANCH * D_MODEL), row(D_MODEL),
            _resident_spec((N_BRANCH * BRANCH_WIDTH, D_MODEL)),
            _resident_spec((D_MODEL, D_MODEL)),
            _layer_spec(layer, (3, D_MODEL)),
            _resident_spec((D_MODEL, 2 * D_FF)),
            _resident_spec((D_FF, D_MODEL)),
        ],
        out_specs=row(D_MODEL),
        out_shape=jax.ShapeDtypeStruct((n, D_MODEL), F32),
        compiler_params=pltpu.CompilerParams(
            dimension_semantics=("arbitrary",), vmem_limit_bytes=VMEM_LIMIT),
        name=f"merge_l{layer}",
    )(ya, yb, yc, gates, x, wb, wo, vecs, wfi, wfo)


def kernel(x, norm_mix_pre, w_in, gmlp_norm_g, gmlp_norm_b, gmlp_w_s, gmlp_b_s, lambda_q1, lambda_k1,
           lambda_q2, lambda_k2, diff_subln_g, pool_w, pool_scale, w_branch, w_out, norm_mix_post,
           norm_ffn_pre, w_ffn_in, w_ffn_out, norm_ffn_post):
    bsz, s, d = x.shape
    assert (s, d) == (SEQ, D_MODEL)
    depth = w_in.shape[0]
    xf = x.reshape(bsz * s, d)
    pad = lambda a: jnp.pad(a, ((0, 0), (0, D_MODEL - a.shape[1])))
    proj_vecs = jnp.stack([norm_mix_pre, pad(gmlp_norm_g), pad(gmlp_norm_b), pad(pool_scale)], axis=1)
    lam_vecs = jnp.stack([lambda_q1, lambda_k1, lambda_q2, lambda_k2], axis=1)
    merge_vecs = jnp.stack([norm_mix_post, norm_ffn_pre, norm_ffn_post], axis=1)
    pool_w = pool_w.astype(BF16)
    merge_weights = (w_branch.reshape(depth, N_BRANCH * BRANCH_WIDTH, D_MODEL), w_out, w_ffn_in,
                     w_ffn_out)
    bs = jnp.broadcast_to(gmlp_b_s[:, :, :, None], (depth, A_GROUPS, CHUNK, CHUNK))
    sgb = jnp.broadcast_to(diff_subln_g[:, :, None], (depth, B_VDIM, TQ))
    for l in range(depth):
        lambda_init = 0.8 - 0.6 * math.exp(-0.3 * l)
        ya, q, k, vt, yc, gates, wb, wo, wfi, wfo = _in_proj(
            xf, proj_vecs, w_in, gmlp_w_s, bs, pool_w, merge_weights, l)
        yb = _attention(q, k, vt, lam_vecs, sgb, lambda_init, l)
        xf = _merge(ya, yb, yc, gates, xf, wb, wo, merge_vecs, wfi, wfo, l)
    return xf.reshape(bsz, s, d)
```

```python
import functools
import math

import jax
import jax.numpy as jnp
from jax import lax
from jax.experimental import pallas as pl
from jax.experimental.pallas import tpu as pltpu

F32 = jnp.float32
BF16 = jnp.bfloat16

D_MODEL = 1024
SEQ = 2048
CHUNK = 128
A_GROUPS = 4
A_WIDTH = 512
B_HEADS = 4
B_HEAD_DIM = 64
B_VDIM = 128
B_WIDTH = 512
QK_WIDTH = B_HEADS * 2 * B_HEAD_DIM
C_WINDOWS = (2, 4, 8, 16)
C_GROUP_DIM = 128
C_WIDTH = 512
N_BRANCH = 3
BRANCH_WIDTH = 512
D_FF = 2816
EPS = 1e-6

COL_A = 0
COL_Q = 2 * A_WIDTH
COL_K = COL_Q + QK_WIDTH
COL_V = COL_K + QK_WIDTH
COL_C = COL_V + B_WIDTH
COL_G = COL_C + C_WIDTH
IN_TOTAL = COL_G + N_BRANCH * D_MODEL

HALO = 16
VMEM_LIMIT = 58 * 1024 * 1024

TM_PROJ = 512
TM_MERGE = 512
TQ = 512
FF_CHUNK = 512


def _rms(x, g):
    ms = jnp.mean(x * x, axis=-1, keepdims=True)
    return x * lax.rsqrt(ms + EPS) * g


def _sigmoid(x):
    return 0.5 * (jnp.tanh(0.5 * x) + 1.0)


def _resident_spec(shape):
    nd = len(shape)
    return pl.BlockSpec(tuple(shape), lambda *_: (0,) * nd, pipeline_mode=pl.Buffered(1))


def _layer_spec(layer, shape):
    nd = len(shape)
    return pl.BlockSpec((None,) + tuple(shape), lambda *_: (layer,) + (0,) * nd,
                        pipeline_mode=pl.Buffered(1))


def _in_proj_kernel(x_ref, vec_ref, w_ref, ws_ref, bs_ref, pw_ref, wb32_ref, wo32_ref, wfi32_ref,
                    wfo32_ref, ya_ref, q_ref, k_ref, vt_ref, yc_ref, gate_ref, wb_ref, wo_ref,
                    wfi_ref, wfo_ref, cbuf):
    for src, dst in ((wb32_ref, wb_ref), (wo32_ref, wo_ref), (wfi32_ref, wfi_ref),
                     (wfo32_ref, wfo_ref)):
        dst[...] = src[...].astype(BF16)

    tm = x_ref.shape[0]
    pos = (pl.program_id(0) * tm) % SEQ
    gpre = vec_ref[0:1, :]
    lng, lnb, psc = (vec_ref[r:r + 1, :A_WIDTH] for r in (1, 2, 3))
    h = _rms(x_ref[...], gpre).astype(BF16)

    def proj(lo, width):
        return jnp.dot(h, w_ref[:, lo:lo + width].astype(BF16), preferred_element_type=F32)

    def gate(j):
        zg = proj(COL_G + j * D_MODEL, D_MODEL)
        gate_ref[:, j * D_MODEL:(j + 1) * D_MODEL] = _sigmoid(zg)


    zc = proj(COL_C, C_WIDTH)
    gate(0)

    @pl.when(pos == 0)
    def _():
        cbuf[0:HALO, :] = jnp.zeros((HALO, C_WIDTH), F32)

    cbuf[HALO:, :] = zc
    p_all = cbuf[...]
    cbuf[0:HALO, :] = p_all[tm:, :]
    s2 = p_all + pltpu.roll(p_all, 1, 0)
    rest = lambda a: a[:, C_GROUP_DIM:]
    s4 = rest(s2) + pltpu.roll(rest(s2), 2, 0)
    s8 = rest(s4) + pltpu.roll(rest(s4), 4, 0)
    s16 = rest(s8) + pltpu.roll(rest(s8), 8, 0)
    sums = [a[HALO:, :C_GROUP_DIM] for a in (s2, s4, s8, s16)]
    head_pos = 1 + lax.broadcasted_iota(jnp.int32, (HALO, C_GROUP_DIM), 0)
    pooled = []
    for g, w in enumerate(C_WINDOWS):
        cs = slice(g * C_GROUP_DIM, (g + 1) * C_GROUP_DIM)
        head_cnt = jnp.where(pos == 0, jnp.minimum(head_pos, w), w).astype(F32)
        mean = jnp.concatenate([sums[g][:HALO] * (1.0 / head_cnt), sums[g][HALO:] * (1.0 / w)],
                               axis=0)
        pooled.append((mean - zc[:, cs]).astype(BF16))
    zero_w = jnp.zeros((C_GROUP_DIM, C_GROUP_DIM), BF16)
    for g in range(0, len(C_WINDOWS), 2):
        cs = slice(g * C_GROUP_DIM, (g + 2) * C_GROUP_DIM)
        w_pair = jnp.concatenate([jnp.concatenate([pw_ref[g], zero_w], axis=1),
                                  jnp.concatenate([zero_w, pw_ref[g + 1]], axis=1)], axis=0)
        yc = jnp.dot(jnp.concatenate(pooled[g:g + 2], axis=1), w_pair,
                     preferred_element_type=F32) * psc[:, cs]
        yc_ref[:, cs] = yc.astype(BF16)

    za = proj(COL_A, 2 * A_WIDTH)
    gate(1)
    ga = 0.5 * za * (1.0 + lax.erf(za * math.sqrt(0.5)))
    u = ga[:, :A_WIDTH]
    vv = ga[:, A_WIDTH:]
    mu = jnp.mean(vv, axis=-1, keepdims=True)
    dv = vv - mu
    var = jnp.mean(dv * dv, axis=-1, keepdims=True)
    vn = (dv * lax.rsqrt(var + EPS) * lng + lnb).astype(BF16)
    trow = lax.broadcasted_iota(jnp.int32, (CHUNK, CHUNK), 0)
    tcol = lax.broadcasted_iota(jnp.int32, (CHUNK, CHUNK), 1)
    chunks = [slice(c * CHUNK, (c + 1) * CHUNK) for c in range(tm // CHUNK)]
    for g in range(A_GROUPS):
        wsg = jnp.where(tcol <= trow, ws_ref[g], 0.0).astype(BF16)
        cs = slice(g * CHUNK, (g + 1) * CHUNK)
        v_side = jnp.concatenate([vn[rs, cs] for rs in chunks], axis=1)
        mixed = jnp.dot(wsg, v_side, preferred_element_type=F32)
        for rs, ls in zip(chunks, chunks):
            ya_ref[rs, cs] = (u[rs, cs] * (mixed[:, ls] + bs_ref[g])).astype(BF16)

    gate(2)

    q_ref[...] = (proj(COL_Q, QK_WIDTH) * (B_HEAD_DIM ** -0.5 * math.log2(math.e))).astype(BF16)
    k_ref[...] = proj(COL_K, QK_WIDTH).astype(BF16)
    vt = proj(COL_V, B_WIDTH).T.astype(BF16)
    for j in range(tm // TQ):
        vt_ref[j] = vt[:, j * TQ:(j + 1) * TQ]


BF16_SUBLANES = 16


def _in_proj(x, vecs, w_in, ws, bs, pw, merge_weights, layer):
    n = x.shape[0]
    tm = TM_PROJ
    assert n % SEQ == 0 and SEQ % tm == 0 and tm % CHUNK == 0 and tm % TQ == 0
    steps = n // tm
    row = lambda width: pl.BlockSpec((tm, width), lambda i: (i, 0))
    out_shapes = [
        jax.ShapeDtypeStruct((n, A_WIDTH), BF16),
        jax.ShapeDtypeStruct((n, QK_WIDTH), BF16),
        jax.ShapeDtypeStruct((n, QK_WIDTH), BF16),
        jax.ShapeDtypeStruct((n // TQ, B_WIDTH, TQ), BF16),
        jax.ShapeDtypeStruct((n, C_WIDTH), BF16),
        jax.ShapeDtypeStruct((n, N_BRANCH * D_MODEL), F32),
    ]
    cast_in, cast_out = [], []
    for w in merge_weights:
        rows, cols = w.shape[1:]
        every = 1
        while rows * every % (steps * BF16_SUBLANES):
            every *= 2
        slab = rows * every // steps
        cast_in.append(pl.BlockSpec((None, slab, cols), lambda i, e=every: (layer, i // e, 0)))
        cast_out.append(pl.BlockSpec((slab, cols), lambda i, e=every: (i // e, 0)))
        out_shapes.append(jax.ShapeDtypeStruct((rows, cols), BF16))
    return pl.pallas_call(
        _in_proj_kernel,
        grid=(steps,),
        in_specs=[
            row(D_MODEL),
            _layer_spec(layer, (4, D_MODEL)),
            _layer_spec(layer, (D_MODEL, IN_TOTAL)),
            _layer_spec(layer, (A_GROUPS, CHUNK, CHUNK)),
            _layer_spec(layer, (A_GROUPS, CHUNK, CHUNK)),
            _layer_spec(layer, (len(C_WINDOWS), C_GROUP_DIM, C_GROUP_DIM)),
        ] + cast_in,
        out_specs=[row(A_WIDTH), row(QK_WIDTH), row(QK_WIDTH),
                   pl.BlockSpec((tm // TQ, B_WIDTH, TQ), lambda i: (i, 0, 0)), row(C_WIDTH),
                   row(N_BRANCH * D_MODEL)] + cast_out,
        out_shape=out_shapes,
        scratch_shapes=[pltpu.VMEM((HALO + tm, C_WIDTH), F32)],
        compiler_params=pltpu.CompilerParams(
            dimension_semantics=("arbitrary",), vmem_limit_bytes=VMEM_LIMIT),
        name=f"in_proj_l{layer}",
    )(x, vecs, w_in, ws, bs, pw, *merge_weights)


ONES_ROWS = 16
N_STREAMS = 2 * B_HEADS


def _attn_kernel(q_ref, qn_ref, k_ref, k0n_ref, vt_ref, lam_ref, sgb_ref,
                 o_ref, sa_scr, sb_scr, ma_scr, mb_scr, m_scr, acc_scr, *, lambda_init):
    b = pl.program_id(0)
    i = pl.program_id(1)
    tq = q_ref.shape[0]
    heads = [slice(h * B_VDIM, (h + 1) * B_VDIM) for h in range(B_HEADS)]
    lane = lax.broadcasted_iota(jnp.int32, (tq, B_VDIM), 1)
    zero = jnp.zeros((tq, B_VDIM), BF16)
    nt = (((1,), (1,)), ((), ()))

    def split_components(qr):
        out = []
        for hs in heads:
            qh = qr[:, hs]
            out += [jnp.where(lane < B_HEAD_DIM, qh, zero), jnp.where(lane >= B_HEAD_DIM, qh, zero)]
        return out

    def scores(qs, kb_of, s_scr, mx_scr):
        for h, hs in enumerate(heads):
            kb = kb_of(hs)
            for c in range(2):
                st = 2 * h + c
                s = lax.dot_general(kb, qs[st], nt, preferred_element_type=F32)
                s_scr[st] = s
                mx_scr[st] = jnp.max(s, axis=0, keepdims=True)

    ones_rows = jnp.ones((ONES_ROWS, tq), BF16)

    def consume(j, s_scr, mx_scr):
        for h, hs in enumerate(heads):
            vtb = jnp.concatenate([vt_ref[j, hs, :], ones_rows], axis=0)
            for c in range(2):
                st = 2 * h + c
                m_old = m_scr[st]
                m_new = jnp.maximum(m_old, mx_scr[st])
                alpha = jnp.exp2(m_old - m_new)
                p = jnp.exp2(s_scr[st] - m_new)
                acc_scr[st] = alpha * acc_scr[st] + jnp.dot(vtb, p.astype(BF16),
                                                            preferred_element_type=F32)
                m_scr[st] = m_new

    half = tq // 2
    causal_edge = lambda shape: (lax.broadcasted_iota(jnp.int32, shape, 0)
                                 <= lax.broadcasted_iota(jnp.int32, shape, 1))
    vis_top = causal_edge((half, tq))
    vis_bot = causal_edge((half, half))

    future = jnp.full((half, half), -jnp.inf, F32)
    ones_half = jnp.ones((ONES_ROWS, half), BF16)

    def diag_max(s_top, s_bot):
        s_bot_wide = jnp.concatenate([future, s_bot], axis=1)
        return jnp.maximum(jnp.max(s_top, axis=0, keepdims=True),
                           jnp.max(s_bot_wide, axis=0, keepdims=True))

    def scores_diag(qs, s_scr, mx_scr):
        top = pl.multiple_of(i * tq, tq)
        bot = pl.multiple_of(i * tq + half, half)
        for h, hs in enumerate(heads):
            k_top, k_bot = k_ref[pl.ds(top, half), hs], k_ref[pl.ds(bot, half), hs]
            for c in range(2):
                st = 2 * h + c
                s_top = lax.dot_general(k_top, qs[st], nt, preferred_element_type=F32)
                s_bot = lax.dot_general(k_bot, qs[st][half:], nt, preferred_element_type=F32)
                s_top = jnp.where(vis_top, s_top, -jnp.inf)
                s_bot = jnp.where(vis_bot, s_bot, -jnp.inf)
                s_scr[st, 0:half, :] = s_top
                s_scr[st, half:, half:] = s_bot
                mx_scr[st] = diag_max(s_top, s_bot)

    def consume_diag(s_scr, mx_scr, mask_on_use=False):
        for h, hs in enumerate(heads):
            vt_top = jnp.concatenate([vt_ref[i, hs, 0:half], ones_half], axis=0)
            vt_bot = jnp.concatenate([vt_ref[i, hs, half:], ones_half], axis=0)
            for c in range(2):
                st = 2 * h + c
                s_top, s_bot = s_scr[st, 0:half, :], s_scr[st, half:, half:]
                if mask_on_use:
                    s_top = jnp.where(vis_top, s_top, -jnp.inf)
                    s_bot = jnp.where(vis_bot, s_bot, -jnp.inf)
                    mx = diag_max(s_top, s_bot)
                else:
                    mx = mx_scr[st]
                m_old = m_scr[st]
                m_new = jnp.maximum(m_old, mx)
                alpha = jnp.exp2(m_old - m_new)
                p_top = jnp.exp2(s_top - m_new).astype(BF16)
                m_scr[st] = m_new
                p_bot = jnp.exp2(s_bot - m_scr[st, :, half:]).astype(BF16)
                a = alpha * acc_scr[st] + jnp.dot(vt_top, p_top, preferred_element_type=F32)
                a_right = a[:, half:] + jnp.dot(vt_bot, p_bot, preferred_element_type=F32)
                acc_scr[st] = jnp.concatenate([a[:, :half], a_right], axis=1)
            finish_head(h, hs)

    qs = split_components(q_ref)
    key_tile = lambda j: (lambda hs: k_ref[pl.ds(pl.multiple_of(j * tq, tq), tq), hs])

    m_scr[...] = jnp.full(m_scr.shape, -jnp.inf, F32)
    acc_scr[...] = jnp.zeros(acc_scr.shape, F32)

    @pl.when((b == 0) & (i == 0))
    def _():
        scores(qs, key_tile(0), sa_scr, ma_scr)

    def pair(t, carry):
        j = 2 * t
        scores(qs, key_tile(j + 1), sb_scr, mb_scr)
        consume(j, sa_scr, ma_scr)
        scores(qs, key_tile(j + 2), sa_scr, ma_scr)
        consume(j + 1, sb_scr, mb_scr)
        return carry

    n_pairs = lax.shift_right_logical(jnp.maximum(i - 1, 0), 1)
    lax.fori_loop(0, n_pairs, pair, 0)
    done = 2 * n_pairs

    lam = (jnp.exp(jnp.sum(lam_ref[0:1, :] * lam_ref[1:2, :], keepdims=True))
           - jnp.exp(jnp.sum(lam_ref[2:3, :] * lam_ref[3:4, :], keepdims=True)) + lambda_init)

    def finish_head(h, hs):
        a0, a1 = acc_scr[2 * h], acc_scr[2 * h + 1]
        l0, l1 = a0[B_VDIM:B_VDIM + 1], a1[B_VDIM:B_VDIM + 1]
        ot = a0[:B_VDIM] * (1.0 / l0) - a1[:B_VDIM] * (lam / l1)
        ms = jnp.mean(ot * ot, axis=0, keepdims=True)
        y = ot * lax.rsqrt(ms + EPS) * sgb_ref[...] * (1.0 - lambda_init)
        o_ref[:, hs] = y.T.astype(BF16)

    def stage_next():
        scores(split_components(qn_ref), lambda hs: k0n_ref[:, hs], sa_scr, ma_scr)

    @pl.when(i == 0)
    def _():
        consume_diag(sa_scr, ma_scr, mask_on_use=True)
        stage_next()

    @pl.when((i & 1) == 1)
    def _():
        scores_diag(qs, sb_scr, mb_scr)
        consume(done, sa_scr, ma_scr)
        consume_diag(sb_scr, mb_scr)
        stage_next()

    @pl.when(((i & 1) == 0) & (i > 0))
    def _():
        scores(qs, key_tile(done + 1), sb_scr, mb_scr)
        consume(done, sa_scr, ma_scr)
        scores_diag(qs, sa_scr, ma_scr)
        consume(done + 1, sb_scr, mb_scr)
        consume_diag(sa_scr, ma_scr)
        stage_next()


def _attention(q, k, vt, lam_vecs, sgb, lambda_init, layer):
    n = q.shape[0]
    assert n % SEQ == 0 and SEQ % TQ == 0 and TQ % (2 * BF16_SUBLANES) == 0
    bsz = n // SEQ
    nq = SEQ // TQ
    last = bsz * nq - 1
    stream = lambda *shape: pltpu.VMEM((N_STREAMS,) + shape, F32)
    return pl.pallas_call(
        functools.partial(_attn_kernel, lambda_init=lambda_init),
        grid=(bsz, nq),
        in_specs=[
            pl.BlockSpec((TQ, B_WIDTH), lambda b, i: (b * nq + i, 0)),
            pl.BlockSpec((TQ, B_WIDTH), lambda b, i: (jnp.minimum(b * nq + i + 1, last), 0)),
            pl.BlockSpec((SEQ, B_WIDTH), lambda b, i: (b, 0)),
            pl.BlockSpec((TQ, B_WIDTH),
                         lambda b, i: (jnp.minimum(b * nq + i + 1, last) // nq * nq, 0)),
            pl.BlockSpec((nq, B_WIDTH, TQ), lambda b, i: (b, 0, 0)),
            _layer_spec(layer, (4, B_HEAD_DIM)),
            _layer_spec(layer, (B_VDIM, TQ)),
        ],
        out_specs=pl.BlockSpec((TQ, B_WIDTH), lambda b, i: (b * nq + i, 0)),
        out_shape=jax.ShapeDtypeStruct((n, B_WIDTH), BF16),
        scratch_shapes=[
            stream(TQ, TQ), stream(TQ, TQ),
            stream(1, TQ), stream(1, TQ),
            stream(1, TQ),
            stream(B_VDIM + ONES_ROWS, TQ),
        ],
        compiler_params=pltpu.CompilerParams(
            dimension_semantics=("arbitrary", "arbitrary"), vmem_limit_bytes=VMEM_LIMIT),
        name=f"attn_l{layer}",
    )(q, q, k, k, vt, lam_vecs, sgb)


def _merge_kernel(ya_ref, yb_ref, yc_ref, gate_ref, x_ref, wb_ref, wo_ref, vec_ref, wfi_ref, wfo_ref,
                  o_ref):
    gpost, gfpre, gfpost = (vec_ref[r:r + 1, :] for r in range(3))
    merged = None
    for j, y_ref in enumerate((ya_ref, yb_ref, yc_ref)):
        up = jnp.dot(y_ref[...], wb_ref[j * BRANCH_WIDTH:(j + 1) * BRANCH_WIDTH, :],
                     preferred_element_type=F32)
        term = gate_ref[:, j * D_MODEL:(j + 1) * D_MODEL] * up
        merged = term if merged is None else merged + term
    mix = jnp.dot(merged.astype(BF16), wo_ref[...], preferred_element_type=F32)
    x1 = x_ref[...] + _rms(mix, gpost)

    h = _rms(x1, gfpre).astype(BF16)
    f = None
    for lo in range(0, D_FF, FF_CHUNK):
        width = min(FF_CHUNK, D_FF - lo)
        g = jnp.dot(h, wfi_ref[:, lo:lo + width], preferred_element_type=F32)
        u = jnp.dot(h, wfi_ref[:, D_FF + lo:D_FF + lo + width], preferred_element_type=F32)
        a = (g * _sigmoid(g) * u).astype(BF16)
        part = jnp.dot(a, wfo_ref[lo:lo + width, :], preferred_element_type=F32)
        f = part if f is None else f + part
    o_ref[...] = x1 + _rms(f, gfpost)


def _merge(ya, yb, yc, gates, x, wb, wo, vecs, wfi, wfo, layer):
    n = x.shape[0]
    tm = TM_MERGE
    assert n % tm == 0
    row = lambda width: pl.BlockSpec((tm, width), lambda i: (i, 0))
    return pl.pallas_call(
        _merge_kernel,
        grid=(n // tm,),
        in_specs=[
            row(BRANCH_WIDTH), row(BRANCH_WIDTH), row(BRANCH_WIDTH),
            row(N_BRANCH * D_MODEL), row(D_MODEL),
            _resident_spec((N_BRANCH * BRANCH_WIDTH, D_MODEL)),
            _resident_spec((D_MODEL, D_MODEL)),
            _layer_spec(layer, (3, D_MODEL)),
            _resident_spec((D_MODEL, 2 * D_FF)),
            _resident_spec((D_FF, D_MODEL)),
        ],
        out_specs=row(D_MODEL),
        out_shape=jax.ShapeDtypeStruct((n, D_MODEL), F32),
        compiler_params=pltpu.CompilerParams(
            dimension_semantics=("arbitrary",), vmem_limit_bytes=VMEM_LIMIT),
        name=f"merge_l{layer}",
    )(ya, yb, yc, gates, x, wb, wo, vecs, wfi, wfo)


def kernel(x, norm_mix_pre, w_in, gmlp_norm_g, gmlp_norm_b, gmlp_w_s, gmlp_b_s, lambda_q1, lambda_k1,
           lambda_q2, lambda_k2, diff_subln_g, pool_w, pool_scale, w_branch, w_out, norm_mix_post,
           norm_ffn_pre, w_ffn_in, w_ffn_out, norm_ffn_post):
    bsz, s, d = x.shape
    assert (s, d) == (SEQ, D_MODEL)
    depth = w_in.shape[0]
    xf = x.reshape(bsz * s, d)
    pad = lambda a: jnp.pad(a, ((0, 0), (0, D_MODEL - a.shape[1])))
    proj_vecs = jnp.stack([norm_mix_pre, pad(gmlp_norm_g), pad(gmlp_norm_b), pad(pool_scale)], axis=1)
    lam_vecs = jnp.stack([lambda_q1, lambda_k1, lambda_q2, lambda_k2], axis=1)
    merge_vecs = jnp.stack([norm_mix_post, norm_ffn_pre, norm_ffn_post], axis=1)
    pool_w = pool_w.astype(BF16)
    merge_weights = (w_branch.reshape(depth, N_BRANCH * BRANCH_WIDTH, D_MODEL), w_out, w_ffn_in,
                     w_ffn_out)
    bs = jnp.broadcast_to(gmlp_b_s[:, :, :, None], (depth, A_GROUPS, CHUNK, CHUNK))
    sgb = jnp.broadcast_to(diff_subln_g[:, :, None], (depth, B_VDIM, TQ))
    for l in range(depth):
        lambda_init = 0.8 - 0.6 * math.exp(-0.3 * l)
        ya, q, k, vt, yc, gates, wb, wo, wfi, wfo = _in_proj(
            xf, proj_vecs, w_in, gmlp_w_s, bs, pool_w, merge_weights, l)
        yb = _attention(q, k, vt, lam_vecs, sgb, lambda_init, l)
        xf = _merge(ya, yb, yc, gates, xf, wb, wo, merge_vecs, wfi, wfo, l)
    return xf.reshape(bsz, s, d)
```

```python
import functools
import math

import jax
import jax.numpy as jnp
from jax import lax
from jax.experimental import pallas as pl
from jax.experimental.pallas import tpu as pltpu

F32 = jnp.float32
BF16 = jnp.bfloat16

D_MODEL = 1024
SEQ = 2048
CHUNK = 128
A_GROUPS = 4
A_WIDTH = 512
B_HEADS = 4
B_HEAD_DIM = 64
B_VDIM = 128
B_WIDTH = 512
QK_WIDTH = B_HEADS * 2 * B_HEAD_DIM
C_WINDOWS = (2, 4, 8, 16)
C_GROUP_DIM = 128
C_WIDTH = 512
N_BRANCH = 3
BRANCH_WIDTH = 512
D_FF = 2816
EPS = 1e-6

COL_A = 0
COL_Q = 2 * A_WIDTH
COL_K = COL_Q + QK_WIDTH
COL_V = COL_K + QK_WIDTH
COL_C = COL_V + B_WIDTH
COL_G = COL_C + C_WIDTH
IN_TOTAL = COL_G + N_BRANCH * D_MODEL

HALO = 16
VMEM_LIMIT = 58 * 1024 * 1024

TM_PROJ = 512
TM_MERGE = 512
TQ = 512
FF_CHUNK = 512


def _rms(x, g):
    ms = jnp.mean(x * x, axis=-1, keepdims=True)
    return x * lax.rsqrt(ms + EPS) * g


def _sigmoid(x):
    return 0.5 * (jnp.tanh(0.5 * x) + 1.0)


def _resident_spec(shape):
    nd = len(shape)
    return pl.BlockSpec(tuple(shape), lambda *_: (0,) * nd, pipeline_mode=pl.Buffered(1))


def _layer_spec(layer, shape):
    nd = len(shape)
    return pl.BlockSpec((None,) + tuple(shape), lambda *_: (layer,) + (0,) * nd,
                        pipeline_mode=pl.Buffered(1))


def _in_proj_kernel(x_ref, vec_ref, w_ref, ws_ref, bs_ref, pw_ref, wb32_ref, wo32_ref, wfi32_ref,
                    wfo32_ref, ya_ref, q_ref, k_ref, vt_ref, yc_ref, gate_ref, wb_ref, wo_ref,
                    wfi_ref, wfo_ref, cbuf):
    for src, dst in ((wb32_ref, wb_ref), (wo32_ref, wo_ref), (wfi32_ref, wfi_ref),
                     (wfo32_ref, wfo_ref)):
        dst[...] = src[...].astype(BF16)

    tm = x_ref.shape[0]
    pos = (pl.program_id(0) * tm) % SEQ
    gpre = vec_ref[0:1, :]
    lng, lnb, psc = (vec_ref[r:r + 1, :A_WIDTH] for r in (1, 2, 3))
    h = _rms(x_ref[...], gpre).astype(BF16)

    def proj(lo, width):
        return jnp.dot(h, w_ref[:, lo:lo + width].astype(BF16), preferred_element_type=F32)

    def gate(j):
        zg = proj(COL_G + j * D_MODEL, D_MODEL)
        gate_ref[:, j * D_MODEL:(j + 1) * D_MODEL] = _sigmoid(zg).astype(gate_ref.dtype)


    zc = proj(COL_C, C_WIDTH)
    gate(0)

    @pl.when(pos == 0)
    def _():
        cbuf[0:HALO, :] = jnp.zeros((HALO, C_WIDTH), F32)

    cbuf[HALO:, :] = zc
    p_all = cbuf[...]
    cbuf[0:HALO, :] = p_all[tm:, :]
    s2 = p_all + pltpu.roll(p_all, 1, 0)
    rest = lambda a: a[:, C_GROUP_DIM:]
    s4 = rest(s2) + pltpu.roll(rest(s2), 2, 0)
    s8 = rest(s4) + pltpu.roll(rest(s4), 4, 0)
    s16 = rest(s8) + pltpu.roll(rest(s8), 8, 0)
    sums = [a[HALO:, :C_GROUP_DIM] for a in (s2, s4, s8, s16)]
    head_pos = 1 + lax.broadcasted_iota(jnp.int32, (HALO, C_GROUP_DIM), 0)
    pooled = []
    for g, w in enumerate(C_WINDOWS):
        cs = slice(g * C_GROUP_DIM, (g + 1) * C_GROUP_DIM)
        head_cnt = jnp.where(pos == 0, jnp.minimum(head_pos, w), w).astype(F32)
        mean = jnp.concatenate([sums[g][:HALO] * (1.0 / head_cnt), sums[g][HALO:] * (1.0 / w)],
                               axis=0)
        pooled.append((mean - zc[:, cs]).astype(BF16))
    zero_w = jnp.zeros((C_GROUP_DIM, C_GROUP_DIM), BF16)
    for g in range(0, len(C_WINDOWS), 2):
        cs = slice(g * C_GROUP_DIM, (g + 2) * C_GROUP_DIM)
        w_pair = jnp.concatenate([jnp.concatenate([pw_ref[g], zero_w], axis=1),
                                  jnp.concatenate([zero_w, pw_ref[g + 1]], axis=1)], axis=0)
        yc = jnp.dot(jnp.concatenate(pooled[g:g + 2], axis=1), w_pair,
                     preferred_element_type=F32) * psc[:, cs]
        yc_ref[:, cs] = yc.astype(BF16)

    za = proj(COL_A, 2 * A_WIDTH)
    gate(1)
    ga = 0.5 * za * (1.0 + lax.erf(za * math.sqrt(0.5)))
    u = ga[:, :A_WIDTH]
    vv = ga[:, A_WIDTH:]
    mu = jnp.mean(vv, axis=-1, keepdims=True)
    dv = vv - mu
    var = jnp.mean(dv * dv, axis=-1, keepdims=True)
    vn = (dv * lax.rsqrt(var + EPS) * lng + lnb).astype(BF16)
    trow = lax.broadcasted_iota(jnp.int32, (CHUNK, CHUNK), 0)
    tcol = lax.broadcasted_iota(jnp.int32, (CHUNK, CHUNK), 1)
    chunks = [slice(c * CHUNK, (c + 1) * CHUNK) for c in range(tm // CHUNK)]
    for g in range(A_GROUPS):
        wsg = jnp.where(tcol <= trow, ws_ref[g], 0.0).astype(BF16)
        cs = slice(g * CHUNK, (g + 1) * CHUNK)
        v_side = jnp.concatenate([vn[rs, cs] for rs in chunks], axis=1)
        mixed = jnp.dot(wsg, v_side, preferred_element_type=F32)
        for rs, ls in zip(chunks, chunks):
            ya_ref[rs, cs] = (u[rs, cs] * (mixed[:, ls] + bs_ref[g])).astype(BF16)

    gate(2)

    q_ref[...] = (proj(COL_Q, QK_WIDTH) * (B_HEAD_DIM ** -0.5 * math.log2(math.e))).astype(BF16)
    k_ref[...] = proj(COL_K, QK_WIDTH).astype(BF16)
    vt = proj(COL_V, B_WIDTH).T.astype(BF16)
    for j in range(tm // TQ):
        vt_ref[j] = vt[:, j * TQ:(j + 1) * TQ]


BF16_SUBLANES = 16


def _in_proj(x, vecs, w_in, ws, bs, pw, merge_weights, layer):
    n = x.shape[0]
    tm = TM_PROJ
    assert n % SEQ == 0 and SEQ % tm == 0 and tm % CHUNK == 0 and tm % TQ == 0
    steps = n // tm
    row = lambda width: pl.BlockSpec((tm, width), lambda i: (i, 0))
    out_shapes = [
        jax.ShapeDtypeStruct((n, A_WIDTH), BF16),
        jax.ShapeDtypeStruct((n, QK_WIDTH), BF16),
        jax.ShapeDtypeStruct((n, QK_WIDTH), BF16),
        jax.ShapeDtypeStruct((n // TQ, B_WIDTH, TQ), BF16),
        jax.ShapeDtypeStruct((n, C_WIDTH), BF16),
        jax.ShapeDtypeStruct((n, N_BRANCH * D_MODEL), BF16),
    ]
    cast_in, cast_out = [], []
    for w in merge_weights:
        rows, cols = w.shape[1:]
        every = 1
        while rows * every % (steps * BF16_SUBLANES):
            every *= 2
        slab = rows * every // steps
        cast_in.append(pl.BlockSpec((None, slab, cols), lambda i, e=every: (layer, i // e, 0)))
        cast_out.append(pl.BlockSpec((slab, cols), lambda i, e=every: (i // e, 0)))
        out_shapes.append(jax.ShapeDtypeStruct((rows, cols), BF16))
    return pl.pallas_call(
        _in_proj_kernel,
        grid=(steps,),
        in_specs=[
            row(D_MODEL),
            _layer_spec(layer, (4, D_MODEL)),
            _layer_spec(layer, (D_MODEL, IN_TOTAL)),
            _layer_spec(layer, (A_GROUPS, CHUNK, CHUNK)),
            _layer_spec(layer, (A_GROUPS, CHUNK, CHUNK)),
            _layer_spec(layer, (len(C_WINDOWS), C_GROUP_DIM, C_GROUP_DIM)),
        ] + cast_in,
        out_specs=[row(A_WIDTH), row(QK_WIDTH), row(QK_WIDTH),
                   pl.BlockSpec((tm // TQ, B_WIDTH, TQ), lambda i: (i, 0, 0)), row(C_WIDTH),
                   row(N_BRANCH * D_MODEL)] + cast_out,
        out_shape=out_shapes,
        scratch_shapes=[pltpu.VMEM((HALO + tm, C_WIDTH), F32)],
        compiler_params=pltpu.CompilerParams(
            dimension_semantics=("arbitrary",), vmem_limit_bytes=VMEM_LIMIT),
        name=f"in_proj_l{layer}",
    )(x, vecs, w_in, ws, bs, pw, *merge_weights)


ONES_ROWS = 16
N_STREAMS = 2 * B_HEADS


def _attn_kernel(q_ref, qn_ref, k_ref, k0n_ref, vt_ref, lam_ref, sgb_ref,
                 o_ref, sa_scr, sb_scr, ma_scr, mb_scr, m_scr, acc_scr, *, lambda_init):
    b = pl.program_id(0)
    i = pl.program_id(1)
    tq = q_ref.shape[0]
    heads = [slice(h * B_VDIM, (h + 1) * B_VDIM) for h in range(B_HEADS)]
    lane = lax.broadcasted_iota(jnp.int32, (tq, B_VDIM), 1)
    zero = jnp.zeros((tq, B_VDIM), BF16)
    nt = (((1,), (1,)), ((), ()))

    def split_components(qr):
        out = []
        for hs in heads:
            qh = qr[:, hs]
            out += [jnp.where(lane < B_HEAD_DIM, qh, zero), jnp.where(lane >= B_HEAD_DIM, qh, zero)]
        return out

    def scores(qs, kb_of, s_scr, mx_scr):
        for h, hs in enumerate(heads):
            kb = kb_of(hs)
            for c in range(2):
                st = 2 * h + c
                s = lax.dot_general(kb, qs[st], nt, preferred_element_type=F32)
                s_scr[st] = s
                mx_scr[st] = jnp.max(s, axis=0, keepdims=True)

    ones_rows = jnp.ones((ONES_ROWS, tq), BF16)

    def consume(j, s_scr, mx_scr):
        for h, hs in enumerate(heads):
            vtb = jnp.concatenate([vt_ref[j, hs, :], ones_rows], axis=0)
            for c in range(2):
                st = 2 * h + c
                m_old = m_scr[st]
                m_new = jnp.maximum(m_old, mx_scr[st])
                alpha = jnp.exp2(m_old - m_new)
                p = jnp.exp2(s_scr[st] - m_new)
                acc_scr[st] = alpha * acc_scr[st] + jnp.dot(vtb, p.astype(BF16),
                                                            preferred_element_type=F32)
                m_scr[st] = m_new

    half = tq // 2
    causal_edge = lambda shape: (lax.broadcasted_iota(jnp.int32, shape, 0)
                                 <= lax.broadcasted_iota(jnp.int32, shape, 1))
    vis_top = causal_edge((half, tq))
    vis_bot = causal_edge((half, half))

    future = jnp.full((half, half), -jnp.inf, F32)
    ones_half = jnp.ones((ONES_ROWS, half), BF16)

    def diag_max(s_top, s_bot):
        s_bot_wide = jnp.concatenate([future, s_bot], axis=1)
        return jnp.maximum(jnp.max(s_top, axis=0, keepdims=True),
                           jnp.max(s_bot_wide, axis=0, keepdims=True))

    def scores_diag(qs, s_scr, mx_scr):
        top = pl.multiple_of(i * tq, tq)
        bot = pl.multiple_of(i * tq + half, half)
        for h, hs in enumerate(heads):
            k_top, k_bot = k_ref[pl.ds(top, half), hs], k_ref[pl.ds(bot, half), hs]
            for c in range(2):
                st = 2 * h + c
                s_top = lax.dot_general(k_top, qs[st], nt, preferred_element_type=F32)
                s_bot = lax.dot_general(k_bot, qs[st][half:], nt, preferred_element_type=F32)
                s_top = jnp.where(vis_top, s_top, -jnp.inf)
                s_bot = jnp.where(vis_bot, s_bot, -jnp.inf)
                s_scr[st, 0:half, :] = s_top
                s_scr[st, half:, half:] = s_bot
                mx_scr[st] = diag_max(s_top, s_bot)

    def consume_diag(s_scr, mx_scr, mask_on_use=False):
        for h, hs in enumerate(heads):
            vt_top = jnp.concatenate([vt_ref[i, hs, 0:half], ones_half], axis=0)
            vt_bot = jnp.concatenate([vt_ref[i, hs, half:], ones_half], axis=0)
            for c in range(2):
                st = 2 * h + c
                s_top, s_bot = s_scr[st, 0:half, :], s_scr[st, half:, half:]
                if mask_on_use:
                    s_top = jnp.where(vis_top, s_top, -jnp.inf)
                    s_bot = jnp.where(vis_bot, s_bot, -jnp.inf)
                    mx = diag_max(s_top, s_bot)
                else:
                    mx = mx_scr[st]
                m_old = m_scr[st]
                m_new = jnp.maximum(m_old, mx)
                alpha = jnp.exp2(m_old - m_new)
                p_top = jnp.exp2(s_top - m_new).astype(BF16)
                m_scr[st] = m_new
                p_bot = jnp.exp2(s_bot - m_scr[st, :, half:]).astype(BF16)
                a = alpha * acc_scr[st] + jnp.dot(vt_top, p_top, preferred_element_type=F32)
                a_right = a[:, half:] + jnp.dot(vt_bot, p_bot, preferred_element_type=F32)
                acc_scr[st] = jnp.concatenate([a[:, :half], a_right], axis=1)
            finish_head(h, hs)

    qs = split_components(q_ref)
    key_tile = lambda j: (lambda hs: k_ref[pl.ds(pl.multiple_of(j * tq, tq), tq), hs])

    m_scr[...] = jnp.full(m_scr.shape, -jnp.inf, F32)
    acc_scr[...] = jnp.zeros(acc_scr.shape, F32)

    @pl.when((b == 0) & (i == 0))
    def _():
        scores(qs, key_tile(0), sa_scr, ma_scr)

    def pair(t, carry):
        j = 2 * t
        scores(qs, key_tile(j + 1), sb_scr, mb_scr)
        consume(j, sa_scr, ma_scr)
        scores(qs, key_tile(j + 2), sa_scr, ma_scr)
        consume(j + 1, sb_scr, mb_scr)
        return carry

    n_pairs = lax.shift_right_logical(jnp.maximum(i - 1, 0), 1)
    lax.fori_loop(0, n_pairs, pair, 0)
    done = 2 * n_pairs

    lam = (jnp.exp(jnp.sum(lam_ref[0:1, :] * lam_ref[1:2, :], keepdims=True))
           - jnp.exp(jnp.sum(lam_ref[2:3, :] * lam_ref[3:4, :], keepdims=True)) + lambda_init)

    def finish_head(h, hs):
        a0, a1 = acc_scr[2 * h], acc_scr[2 * h + 1]
        l0, l1 = a0[B_VDIM:B_VDIM + 1], a1[B_VDIM:B_VDIM + 1]
        ot = a0[:B_VDIM] * (1.0 / l0) - a1[:B_VDIM] * (lam / l1)
        ms = jnp.mean(ot * ot, axis=0, keepdims=True)
        y = ot * lax.rsqrt(ms + EPS) * sgb_ref[...] * (1.0 - lambda_init)
        o_ref[:, hs] = y.T.astype(BF16)

    def stage_next():
        scores(split_components(qn_ref), lambda hs: k0n_ref[:, hs], sa_scr, ma_scr)

    @pl.when(i == 0)
    def _():
        consume_diag(sa_scr, ma_scr, mask_on_use=True)
        stage_next()

    @pl.when((i & 1) == 1)
    def _():
        scores_diag(qs, sb_scr, mb_scr)
        consume(done, sa_scr, ma_scr)
        consume_diag(sb_scr, mb_scr)
        stage_next()

    @pl.when(((i & 1) == 0) & (i > 0))
    def _():
        scores(qs, key_tile(done + 1), sb_scr, mb_scr)
        consume(done, sa_scr, ma_scr)
        scores_diag(qs, sa_scr, ma_scr)
        consume(done + 1, sb_scr, mb_scr)
        consume_diag(sa_scr, ma_scr)
        stage_next()


def _attention(q, k, vt, lam_vecs, sgb, lambda_init, layer):
    n = q.shape[0]
    assert n % SEQ == 0 and SEQ % TQ == 0 and TQ % (2 * BF16_SUBLANES) == 0
    bsz = n // SEQ
    nq = SEQ // TQ
    last = bsz * nq - 1
    stream = lambda *shape: pltpu.VMEM((N_STREAMS,) + shape, F32)
    return pl.pallas_call(
        functools.partial(_attn_kernel, lambda_init=lambda_init),
        grid=(bsz, nq),
        in_specs=[
            pl.BlockSpec((TQ, B_WIDTH), lambda b, i: (b * nq + i, 0)),
            pl.BlockSpec((TQ, B_WIDTH), lambda b, i: (jnp.minimum(b * nq + i + 1, last), 0)),
            pl.BlockSpec((SEQ, B_WIDTH), lambda b, i: (b, 0)),
            pl.BlockSpec((TQ, B_WIDTH),
                         lambda b, i: (jnp.minimum(b * nq + i + 1, last) // nq * nq, 0)),
            pl.BlockSpec((nq, B_WIDTH, TQ), lambda b, i: (b, 0, 0)),
            _layer_spec(layer, (4, B_HEAD_DIM)),
            _layer_spec(layer, (B_VDIM, TQ)),
        ],
        out_specs=pl.BlockSpec((TQ, B_WIDTH), lambda b, i: (b * nq + i, 0)),
        out_shape=jax.ShapeDtypeStruct((n, B_WIDTH), BF16),
        scratch_shapes=[
            stream(TQ, TQ), stream(TQ, TQ),
            stream(1, TQ), stream(1, TQ),
            stream(1, TQ),
            stream(B_VDIM + ONES_ROWS, TQ),
        ],
        compiler_params=pltpu.CompilerParams(
            dimension_semantics=("arbitrary", "arbitrary"), vmem_limit_bytes=VMEM_LIMIT),
        name=f"attn_l{layer}",
    )(q, q, k, k, vt, lam_vecs, sgb)


def _merge_kernel(ya_ref, yb_ref, yc_ref, gate_ref, x_ref, wb_ref, wo_ref, vec_ref, wfi_ref, wfo_ref,
                  o_ref):
    gpost, gfpre, gfpost = (vec_ref[r:r + 1, :] for r in range(3))
    merged = None
    for j, y_ref in enumerate((ya_ref, yb_ref, yc_ref)):
        up = jnp.dot(y_ref[...], wb_ref[j * BRANCH_WIDTH:(j + 1) * BRANCH_WIDTH, :],
                     preferred_element_type=F32)
        term = gate_ref[:, j * D_MODEL:(j + 1) * D_MODEL].astype(F32) * up
        merged = term if merged is None else merged + term
    mix = jnp.dot(merged.astype(BF16), wo_ref[...], preferred_element_type=F32)
    x1 = x_ref[...] + _rms(mix, gpost)

    h = _rms(x1, gfpre).astype(BF16)
    f = None
    for lo in range(0, D_FF, FF_CHUNK):
        width = min(FF_CHUNK, D_FF - lo)
        g = jnp.dot(h, wfi_ref[:, lo:lo + width], preferred_element_type=F32)
        u = jnp.dot(h, wfi_ref[:, D_FF + lo:D_FF + lo + width], preferred_element_type=F32)
        a = (g * _sigmoid(g) * u).astype(BF16)
        part = jnp.dot(a, wfo_ref[lo:lo + width, :], preferred_element_type=F32)
        f = part if f is None else f + part
    o_ref[...] = x1 + _rms(f, gfpost)


def _merge(ya, yb, yc, gates, x, wb, wo, vecs, wfi, wfo, layer):
    n = x.shape[0]
    tm = TM_MERGE
    assert n % tm == 0
    row = lambda width: pl.BlockSpec((tm, width), lambda i: (i, 0))
    return pl.pallas_call(
        _merge_kernel,
        grid=(n // tm,),
        in_specs=[
            row(BRANCH_WIDTH), row(BRANCH_WIDTH), row(BRANCH_WIDTH),
            row(N_BRANCH * D_MODEL), row(D_MODEL),
            _resident_spec((N_BRANCH * BRANCH_WIDTH, D_MODEL)),
            _resident_spec((D_MODEL, D_MODEL)),
            _layer_spec(layer, (3, D_MODEL)),
            _resident_spec((D_MODEL, 2 * D_FF)),
            _resident_spec((D_FF, D_MODEL)),
        ],
        out_specs=row(D_MODEL),
        out_shape=jax.ShapeDtypeStruct((n, D_MODEL), F32),
        compiler_params=pltpu.CompilerParams(
            dimension_semantics=("arbitrary",), vmem_limit_bytes=VMEM_LIMIT),
        name=f"merge_l{layer}",
    )(ya, yb, yc, gates, x, wb, wo, vecs, wfi, wfo)


def kernel(x, norm_mix_pre, w_in, gmlp_norm_g, gmlp_norm_b, gmlp_w_s, gmlp_b_s, lambda_q1, lambda_k1,
           lambda_q2, lambda_k2, diff_subln_g, pool_w, pool_scale, w_branch, w_out, norm_mix_post,
           norm_ffn_pre, w_ffn_in, w_ffn_out, norm_ffn_post):
    bsz, s, d = x.shape
    assert (s, d) == (SEQ, D_MODEL)
    depth = w_in.shape[0]
    xf = x.reshape(bsz * s, d)
    pad = lambda a: jnp.pad(a, ((0, 0), (0, D_MODEL - a.shape[1])))
    proj_vecs = jnp.stack([norm_mix_pre, pad(gmlp_norm_g), pad(gmlp_norm_b), pad(pool_scale)], axis=1)
    lam_vecs = jnp.stack([lambda_q1, lambda_k1, lambda_q2, lambda_k2], axis=1)
    merge_vecs = jnp.stack([norm_mix_post, norm_ffn_pre, norm_ffn_post], axis=1)
    pool_w = pool_w.astype(BF16)
    merge_weights = (w_branch.reshape(depth, N_BRANCH * BRANCH_WIDTH, D_MODEL), w_out, w_ffn_in,
                     w_ffn_out)
    bs = jnp.broadcast_to(gmlp_b_s[:, :, :, None], (depth, A_GROUPS, CHUNK, CHUNK))
    sgb = jnp.broadcast_to(diff_subln_g[:, :, None], (depth, B_VDIM, TQ))
    for l in range(depth):
        lambda_init = 0.8 - 0.6 * math.exp(-0.3 * l)
        ya, q, k, vt, yc, gates, wb, wo, wfi, wfo = _in_proj(
            xf, proj_vecs, w_in, gmlp_w_s, bs, pool_w, merge_weights, l)
        yb = _attention(q, k, vt, lam_vecs, sgb, lambda_init, l)
        xf = _merge(ya, yb, yc, gates, xf, wb, wo, merge_vecs, wfi, wfo, l)
    return xf.reshape(bsz, s, d)
```

```python
import functools
import math

import jax
import jax.numpy as jnp
from jax import lax
from jax.experimental import pallas as pl
from jax.experimental.pallas import tpu as pltpu

F32 = jnp.float32
BF16 = jnp.bfloat16

D_MODEL = 1024
SEQ = 2048
CHUNK = 128
A_GROUPS = 4
A_WIDTH = 512
B_HEADS = 4
B_HEAD_DIM = 64
B_VDIM = 128
B_WIDTH = 512
QK_WIDTH = B_HEADS * 2 * B_HEAD_DIM
C_WINDOWS = (2, 4, 8, 16)
C_GROUP_DIM = 128
C_WIDTH = 512
N_BRANCH = 3
BRANCH_WIDTH = 512
D_FF = 2816
EPS = 1e-6

COL_A = 0
COL_Q = 2 * A_WIDTH
COL_K = COL_Q + QK_WIDTH
COL_V = COL_K + QK_WIDTH
COL_C = COL_V + B_WIDTH
COL_G = COL_C + C_WIDTH
IN_TOTAL = COL_G + N_BRANCH * D_MODEL

HALO = 16
VMEM_LIMIT = 58 * 1024 * 1024

TM_PROJ = 512
TM_MERGE = 512
TQ = 512
FF_CHUNK = 512


def _rms(x, g):
    ms = jnp.mean(x * x, axis=-1, keepdims=True)
    return x * lax.rsqrt(ms + EPS) * g


def _sigmoid(x):
    return 0.5 * (jnp.tanh(0.5 * x) + 1.0)


def _resident_spec(shape):
    nd = len(shape)
    return pl.BlockSpec(tuple(shape), lambda *_: (0,) * nd, pipeline_mode=pl.Buffered(1))


def _layer_spec(layer, shape):
    nd = len(shape)
    return pl.BlockSpec((None,) + tuple(shape), lambda *_: (layer,) + (0,) * nd,
                        pipeline_mode=pl.Buffered(1))


def _in_proj_kernel(x_ref, vec_ref, w_ref, ws_ref, bs_ref, pw_ref, wb32_ref, wo32_ref, wfi32_ref,
                    wfo32_ref, ya_ref, q_ref, k_ref, vt_ref, yc_ref, gate_ref, wb_ref, wo_ref,
                    wfi_ref, wfo_ref, cbuf):
    for src, dst in ((wb32_ref, wb_ref), (wo32_ref, wo_ref), (wfi32_ref, wfi_ref),
                     (wfo32_ref, wfo_ref)):
        dst[...] = src[...].astype(BF16)

    tm = x_ref.shape[0]
    pos = (pl.program_id(0) * tm) % SEQ
    gpre = vec_ref[0:1, :]
    lng, lnb, psc = (vec_ref[r:r + 1, :A_WIDTH] for r in (1, 2, 3))
    h = _rms(x_ref[...], gpre).astype(BF16)

    def proj(lo, width):
        return jnp.dot(h, w_ref[:, lo:lo + width].astype(BF16), preferred_element_type=F32)

    def gate(j):
        zg = proj(COL_G + j * D_MODEL, D_MODEL)
        gate_ref[:, j * D_MODEL:(j + 1) * D_MODEL] = _sigmoid(zg)


    zc = proj(COL_C, C_WIDTH)
    gate(0)

    @pl.when(pos == 0)
    def _():
        cbuf[0:HALO, :] = jnp.zeros((HALO, C_WIDTH), F32)

    cbuf[HALO:, :] = zc
    p_all = cbuf[...]
    cbuf[0:HALO, :] = p_all[tm:, :]
    s2 = p_all + pltpu.roll(p_all, 1, 0)
    rest = lambda a: a[:, C_GROUP_DIM:]
    s4 = rest(s2) + pltpu.roll(rest(s2), 2, 0)
    s8 = rest(s4) + pltpu.roll(rest(s4), 4, 0)
    s16 = rest(s8) + pltpu.roll(rest(s8), 8, 0)
    sums = [a[HALO:, :C_GROUP_DIM] for a in (s2, s4, s8, s16)]
    head_pos = 1 + lax.broadcasted_iota(jnp.int32, (HALO, C_GROUP_DIM), 0)
    pooled = []
    for g, w in enumerate(C_WINDOWS):
        cs = slice(g * C_GROUP_DIM, (g + 1) * C_GROUP_DIM)
        head_cnt = jnp.where(pos == 0, jnp.minimum(head_pos, w), w).astype(F32)
        mean = jnp.concatenate([sums[g][:HALO] * (1.0 / head_cnt), sums[g][HALO:] * (1.0 / w)],
                               axis=0)
        pooled.append((mean - zc[:, cs]).astype(BF16))
    zero_w = jnp.zeros((C_GROUP_DIM, C_GROUP_DIM), BF16)
    for g in range(0, len(C_WINDOWS), 2):
        cs = slice(g * C_GROUP_DIM, (g + 2) * C_GROUP_DIM)
        w_pair = jnp.concatenate([jnp.concatenate([pw_ref[g], zero_w], axis=1),
                                  jnp.concatenate([zero_w, pw_ref[g + 1]], axis=1)], axis=0)
        yc = jnp.dot(jnp.concatenate(pooled[g:g + 2], axis=1), w_pair,
                     preferred_element_type=F32) * psc[:, cs]
        yc_ref[:, cs] = yc.astype(BF16)

    za = proj(COL_A, 2 * A_WIDTH)
    gate(1)
    ga = 0.5 * za * (1.0 + lax.erf(za * math.sqrt(0.5)))
    u = ga[:, :A_WIDTH]
    vv = ga[:, A_WIDTH:]
    mu = jnp.mean(vv, axis=-1, keepdims=True)
    dv = vv - mu
    var = jnp.mean(dv * dv, axis=-1, keepdims=True)
    vn = (dv * lax.rsqrt(var + EPS) * lng + lnb).astype(BF16)
    trow = lax.broadcasted_iota(jnp.int32, (CHUNK, CHUNK), 0)
    tcol = lax.broadcasted_iota(jnp.int32, (CHUNK, CHUNK), 1)
    chunks = [slice(c * CHUNK, (c + 1) * CHUNK) for c in range(tm // CHUNK)]
    for g in range(A_GROUPS):
        wsg = jnp.where(tcol <= trow, ws_ref[g], 0.0).astype(BF16)
        cs = slice(g * CHUNK, (g + 1) * CHUNK)
        v_side = jnp.concatenate([vn[rs, cs] for rs in chunks], axis=1)
        mixed = jnp.dot(wsg, v_side, preferred_element_type=F32)
        for rs, ls in zip(chunks, chunks):
            ya_ref[rs, cs] = (u[rs, cs] * (mixed[:, ls] + bs_ref[g])).astype(BF16)

    gate(2)

    q_ref[...] = (proj(COL_Q, QK_WIDTH) * (B_HEAD_DIM ** -0.5 * math.log2(math.e))).astype(BF16)
    k_ref[...] = proj(COL_K, QK_WIDTH).astype(BF16)
    vt = proj(COL_V, B_WIDTH).T.astype(BF16)
    for j in range(tm // TQ):
        vt_ref[j] = vt[:, j * TQ:(j + 1) * TQ]


BF16_SUBLANES = 16


def _in_proj(x, vecs, w_in, ws, bs, pw, merge_weights, layer):
    n = x.shape[0]
    tm = TM_PROJ
    assert n % SEQ == 0 and SEQ % tm == 0 and tm % CHUNK == 0 and tm % TQ == 0
    steps = n // tm
    row = lambda width: pl.BlockSpec((tm, width), lambda i: (i, 0))
    out_shapes = [
        jax.ShapeDtypeStruct((n, A_WIDTH), BF16),
        jax.ShapeDtypeStruct((n, QK_WIDTH), BF16),
        jax.ShapeDtypeStruct((n, QK_WIDTH), BF16),
        jax.ShapeDtypeStruct((n // TQ, B_WIDTH, TQ), BF16),
        jax.ShapeDtypeStruct((n, C_WIDTH), BF16),
        jax.ShapeDtypeStruct((n, N_BRANCH * D_MODEL), F32),
    ]
    cast_in, cast_out = [], []
    for w in merge_weights:
        rows, cols = w.shape[1:]
        every = 1
        while rows * every % (steps * BF16_SUBLANES):
            every *= 2
        slab = rows * every // steps
        cast_in.append(pl.BlockSpec((None, slab, cols), lambda i, e=every: (layer, i // e, 0)))
        cast_out.append(pl.BlockSpec((slab, cols), lambda i, e=every: (i // e, 0)))
        out_shapes.append(jax.ShapeDtypeStruct((rows, cols), BF16))
    return pl.pallas_call(
        _in_proj_kernel,
        grid=(steps,),
        in_specs=[
            row(D_MODEL),
            _layer_spec(layer, (4, D_MODEL)),
            _layer_spec(layer, (D_MODEL, IN_TOTAL)),
            _layer_spec(layer, (A_GROUPS, CHUNK, CHUNK)),
            _layer_spec(layer, (A_GROUPS, CHUNK, CHUNK)),
            _layer_spec(layer, (len(C_WINDOWS), C_GROUP_DIM, C_GROUP_DIM)),
        ] + cast_in,
        out_specs=[row(A_WIDTH), row(QK_WIDTH), row(QK_WIDTH),
                   pl.BlockSpec((tm // TQ, B_WIDTH, TQ), lambda i: (i, 0, 0)), row(C_WIDTH),
                   row(N_BRANCH * D_MODEL)] + cast_out,
        out_shape=out_shapes,
        scratch_shapes=[pltpu.VMEM((HALO + tm, C_WIDTH), F32)],
        compiler_params=pltpu.CompilerParams(
            dimension_semantics=("arbitrary",), vmem_limit_bytes=VMEM_LIMIT),
        name=f"in_proj_l{layer}",
    )(x, vecs, w_in, ws, bs, pw, *merge_weights)


ONES_ROWS = 16
N_STREAMS = 2 * B_HEADS


def _attn_kernel(q_ref, qn_ref, k_ref, k0n_ref, vt_ref, lam_ref, sgb_ref,
                 o_ref, sa_scr, sb_scr, ma_scr, mb_scr, m_scr, acc_scr, *, lambda_init):
    b = pl.program_id(0)
    i = pl.program_id(1)
    tq = q_ref.shape[0]
    heads = [slice(h * B_VDIM, (h + 1) * B_VDIM) for h in range(B_HEADS)]
    lane = lax.broadcasted_iota(jnp.int32, (tq, B_VDIM), 1)
    zero = jnp.zeros((tq, B_VDIM), BF16)
    nt = (((1,), (1,)), ((), ()))

    def split_components(qr):
        out = []
        for hs in heads:
            qh = qr[:, hs]
            out += [jnp.where(lane < B_HEAD_DIM, qh, zero), jnp.where(lane >= B_HEAD_DIM, qh, zero)]
        return out

    def scores(qs, kb_of, s_scr, mx_scr):
        for h, hs in enumerate(heads):
            kb = kb_of(hs)
            for c in range(2):
                st = 2 * h + c
                s = lax.dot_general(kb, qs[st], nt, preferred_element_type=F32)
                s_scr[st] = s
                mx_scr[st] = jnp.max(s, axis=0, keepdims=True)

    def rescaled(st, m_new, first):
        if first:
            return None
        return jnp.exp2(m_scr[st] - m_new) * acc_scr[st]

    def plus(a, b):
        return b if a is None else a + b

    ones_rows = jnp.ones((ONES_ROWS, tq), BF16)

    def consume(j, s_scr, mx_scr, first):
        for h, hs in enumerate(heads):
            vtb = jnp.concatenate([vt_ref[j, hs, :], ones_rows], axis=0)
            for c in range(2):
                st = 2 * h + c
                mx = mx_scr[st]
                m_new = mx if first else jnp.maximum(m_scr[st], mx)
                p = jnp.exp2(s_scr[st] - m_new)
                acc_scr[st] = plus(rescaled(st, m_new, first),
                                   jnp.dot(vtb, p.astype(BF16), preferred_element_type=F32))
                m_scr[st] = m_new

    half = tq // 2
    causal_edge = lambda shape: (lax.broadcasted_iota(jnp.int32, shape, 0)
                                 <= lax.broadcasted_iota(jnp.int32, shape, 1))
    vis_top = causal_edge((half, tq))
    vis_bot = causal_edge((half, half))

    future = jnp.full((half, half), -jnp.inf, F32)
    ones_half = jnp.ones((ONES_ROWS, half), BF16)

    def diag_max(s_top, s_bot):
        s_bot_wide = jnp.concatenate([future, s_bot], axis=1)
        return jnp.maximum(jnp.max(s_top, axis=0, keepdims=True),
                           jnp.max(s_bot_wide, axis=0, keepdims=True))

    def scores_diag(n, qs, s_scr, mx_scr):
        top, bot = n * tq, n * tq + half
        for h, hs in enumerate(heads):
            k_top, k_bot = k_ref[top:top + half, hs], k_ref[bot:bot + half, hs]
            for c in range(2):
                st = 2 * h + c
                s_top = lax.dot_general(k_top, qs[st], nt, preferred_element_type=F32)
                s_bot = lax.dot_general(k_bot, qs[st][half:], nt, preferred_element_type=F32)
                s_top = jnp.where(vis_top, s_top, -jnp.inf)
                s_bot = jnp.where(vis_bot, s_bot, -jnp.inf)
                s_scr[st, 0:half, :] = s_top
                s_scr[st, half:, half:] = s_bot
                mx_scr[st] = diag_max(s_top, s_bot)

    def consume_diag(n, s_scr, mx_scr, first):
        for h, hs in enumerate(heads):
            vt_top = jnp.concatenate([vt_ref[n, hs, 0:half], ones_half], axis=0)
            vt_bot = jnp.concatenate([vt_ref[n, hs, half:], ones_half], axis=0)
            for c in range(2):
                st = 2 * h + c
                s_top, s_bot = s_scr[st, 0:half, :], s_scr[st, half:, half:]
                if first:
                    s_top = jnp.where(vis_top, s_top, -jnp.inf)
                    s_bot = jnp.where(vis_bot, s_bot, -jnp.inf)
                    m_new = diag_max(s_top, s_bot)
                else:
                    m_new = jnp.maximum(m_scr[st], mx_scr[st])
                a = rescaled(st, m_new, first)
                m_scr[st] = m_new
                p_top = jnp.exp2(s_top - m_new).astype(BF16)
                p_bot = jnp.exp2(s_bot - m_scr[st, :, half:]).astype(BF16)
                a = plus(a, jnp.dot(vt_top, p_top, preferred_element_type=F32))
                a_right = a[:, half:] + jnp.dot(vt_bot, p_bot, preferred_element_type=F32)
                acc_scr[st] = jnp.concatenate([a[:, :half], a_right], axis=1)
            finish_head(h, hs)

    lam = (jnp.exp(jnp.sum(lam_ref[0:1, :] * lam_ref[1:2, :], keepdims=True))
           - jnp.exp(jnp.sum(lam_ref[2:3, :] * lam_ref[3:4, :], keepdims=True)) + lambda_init)

    def finish_head(h, hs):
        a0, a1 = acc_scr[2 * h], acc_scr[2 * h + 1]
        l0, l1 = a0[B_VDIM:B_VDIM + 1], a1[B_VDIM:B_VDIM + 1]
        ot = a0[:B_VDIM] * (1.0 / l0) - a1[:B_VDIM] * (lam / l1)
        ms = jnp.mean(ot * ot, axis=0, keepdims=True)
        y = ot * lax.rsqrt(ms + EPS) * sgb_ref[...] * (1.0 - lambda_init)
        o_ref[:, hs] = y.T.astype(BF16)

    qs = split_components(q_ref)
    key_tile = lambda j: (lambda hs: k_ref[j * tq:(j + 1) * tq, hs])
    stage = ((sa_scr, ma_scr), (sb_scr, mb_scr))

    @pl.when((b == 0) & (i == 0))
    def _():
        scores(qs, key_tile(0), *stage[0])

    for n in range(k_ref.shape[0] // tq):
        @pl.when(i == n)
        def _(n=n):
            for j in range(n):
                if j + 1 < n:
                    scores(qs, key_tile(j + 1), *stage[(j + 1) % 2])
                else:
                    scores_diag(n, qs, *stage[n % 2])
                consume(j, *stage[j % 2], first=(j == 0))
            consume_diag(n, *stage[n % 2], first=(n == 0))
            scores(split_components(qn_ref), lambda hs: k0n_ref[:, hs], *stage[0])


def _attention(q, k, vt, lam_vecs, sgb, lambda_init, layer):
    n = q.shape[0]
    assert n % SEQ == 0 and SEQ % TQ == 0 and TQ % (2 * BF16_SUBLANES) == 0
    bsz = n // SEQ
    nq = SEQ // TQ
    last = bsz * nq - 1
    stream = lambda *shape: pltpu.VMEM((N_STREAMS,) + shape, F32)
    return pl.pallas_call(
        functools.partial(_attn_kernel, lambda_init=lambda_init),
        grid=(bsz, nq),
        in_specs=[
            pl.BlockSpec((TQ, B_WIDTH), lambda b, i: (b * nq + i, 0)),
            pl.BlockSpec((TQ, B_WIDTH), lambda b, i: (jnp.minimum(b * nq + i + 1, last), 0)),
            pl.BlockSpec((SEQ, B_WIDTH), lambda b, i: (b, 0)),
            pl.BlockSpec((TQ, B_WIDTH),
                         lambda b, i: (jnp.minimum(b * nq + i + 1, last) // nq * nq, 0)),
            pl.BlockSpec((nq, B_WIDTH, TQ), lambda b, i: (b, 0, 0)),
            _layer_spec(layer, (4, B_HEAD_DIM)),
            _layer_spec(layer, (B_VDIM, TQ)),
        ],
        out_specs=pl.BlockSpec((TQ, B_WIDTH), lambda b, i: (b * nq + i, 0)),
        out_shape=jax.ShapeDtypeStruct((n, B_WIDTH), BF16),
        scratch_shapes=[
            stream(TQ, TQ), stream(TQ, TQ),
            stream(1, TQ), stream(1, TQ),
            stream(1, TQ),
            stream(B_VDIM + ONES_ROWS, TQ),
        ],
        compiler_params=pltpu.CompilerParams(
            dimension_semantics=("arbitrary", "arbitrary"), vmem_limit_bytes=VMEM_LIMIT),
        name=f"attn_l{layer}",
    )(q, q, k, k, vt, lam_vecs, sgb)


def _merge_kernel(ya_ref, yb_ref, yc_ref, gate_ref, x_ref, wb_ref, wo_ref, vec_ref, wfi_ref, wfo_ref,
                  o_ref):
    gpost, gfpre, gfpost = (vec_ref[r:r + 1, :] for r in range(3))
    merged = None
    for j, y_ref in enumerate((ya_ref, yb_ref, yc_ref)):
        up = jnp.dot(y_ref[...], wb_ref[j * BRANCH_WIDTH:(j + 1) * BRANCH_WIDTH, :],
                     preferred_element_type=F32)
        term = gate_ref[:, j * D_MODEL:(j + 1) * D_MODEL] * up
        merged = term if merged is None else merged + term
    mix = jnp.dot(merged.astype(BF16), wo_ref[...], preferred_element_type=F32)
    x1 = x_ref[...] + _rms(mix, gpost)

    h = _rms(x1, gfpre).astype(BF16)
    f = None
    for lo in range(0, D_FF, FF_CHUNK):
        width = min(FF_CHUNK, D_FF - lo)
        g = jnp.dot(h, wfi_ref[:, lo:lo + width], preferred_element_type=F32)
        u = jnp.dot(h, wfi_ref[:, D_FF + lo:D_FF + lo + width], preferred_element_type=F32)
        a = (g * _sigmoid(g) * u).astype(BF16)
        part = jnp.dot(a, wfo_ref[lo:lo + width, :], preferred_element_type=F32)
        f = part if f is None else f + part
    o_ref[...] = x1 + _rms(f, gfpost)


def _merge(ya, yb, yc, gates, x, wb, wo, vecs, wfi, wfo, layer):
    n = x.shape[0]
    tm = TM_MERGE
    assert n % tm == 0
    row = lambda width: pl.BlockSpec((tm, width), lambda i: (i, 0))
    return pl.pallas_call(
        _merge_kernel,
        grid=(n // tm,),
        in_specs=[
            row(BRANCH_WIDTH), row(BRANCH_WIDTH), row(BRANCH_WIDTH),
            row(N_BRANCH * D_MODEL), row(D_MODEL),
            _resident_spec((N_BRANCH * BRANCH_WIDTH, D_MODEL)),
            _resident_spec((D_MODEL, D_MODEL)),
            _layer_spec(layer, (3, D_MODEL)),
            _resident_spec((D_MODEL, 2 * D_FF)),
            _resident_spec((D_FF, D_MODEL)),
        ],
        out_specs=row(D_MODEL),
        out_shape=jax.ShapeDtypeStruct((n, D_MODEL), F32),
        compiler_params=pltpu.CompilerParams(
            dimension_semantics=("arbitrary",), vmem_limit_bytes=VMEM_LIMIT),
        name=f"merge_l{layer}",
    )(ya, yb, yc, gates, x, wb, wo, vecs, wfi, wfo)


def kernel(x, norm_mix_pre, w_in, gmlp_norm_g, gmlp_norm_b, gmlp_w_s, gmlp_b_s, lambda_q1, lambda_k1,
           lambda_q2, lambda_k2, diff_subln_g, pool_w, pool_scale, w_branch, w_out, norm_mix_post,
           norm_ffn_pre, w_ffn_in, w_ffn_out, norm_ffn_post):
    bsz, s, d = x.shape
    assert (s, d) == (SEQ, D_MODEL)
    depth = w_in.shape[0]
    xf = x.reshape(bsz * s, d)
    pad = lambda a: jnp.pad(a, ((0, 0), (0, D_MODEL - a.shape[1])))
    proj_vecs = jnp.stack([norm_mix_pre, pad(gmlp_norm_g), pad(gmlp_norm_b), pad(pool_scale)], axis=1)
    lam_vecs = jnp.stack([lambda_q1, lambda_k1, lambda_q2, lambda_k2], axis=1)
    merge_vecs = jnp.stack([norm_mix_post, norm_ffn_pre, norm_ffn_post], axis=1)
    pool_w = pool_w.astype(BF16)
    merge_weights = (w_branch.reshape(depth, N_BRANCH * BRANCH_WIDTH, D_MODEL), w_out, w_ffn_in,
                     w_ffn_out)
    bs = jnp.broadcast_to(gmlp_b_s[:, :, :, None], (depth, A_GROUPS, CHUNK, CHUNK))
    sgb = jnp.broadcast_to(diff_subln_g[:, :, None], (depth, B_VDIM, TQ))
    for l in range(depth):
        lambda_init = 0.8 - 0.6 * math.exp(-0.3 * l)
        ya, q, k, vt, yc, gates, wb, wo, wfi, wfo = _in_proj(
            xf, proj_vecs, w_in, gmlp_w_s, bs, pool_w, merge_weights, l)
        yb = _attention(q, k, vt, lam_vecs, sgb, lambda_init, l)
        xf = _merge(ya, yb, yc, gates, xf, wb, wo, merge_vecs, wfi, wfo, l)
    return xf.reshape(bsz, s, d)
```

```python
import functools
import math

import jax
import jax.numpy as jnp
from jax import lax
from jax.experimental import pallas as pl
from jax.experimental.pallas import tpu as pltpu

F32 = jnp.float32
BF16 = jnp.bfloat16

D_MODEL = 1024
SEQ = 2048
CHUNK = 128
A_GROUPS = 4
A_WIDTH = 512
B_HEADS = 4
B_HEAD_DIM = 64
B_VDIM = 128
B_WIDTH = 512
QK_WIDTH = B_HEADS * 2 * B_HEAD_DIM
C_WINDOWS = (2, 4, 8, 16)
C_GROUP_DIM = 128
C_WIDTH = 512
N_BRANCH = 3
BRANCH_WIDTH = 512
D_FF = 2816
EPS = 1e-6

COL_A = 0
COL_Q = 2 * A_WIDTH
COL_K = COL_Q + QK_WIDTH
COL_V = COL_K + QK_WIDTH
COL_C = COL_V + B_WIDTH
COL_G = COL_C + C_WIDTH
IN_TOTAL = COL_G + N_BRANCH * D_MODEL

HALO = 16
VMEM_LIMIT = 58 * 1024 * 1024

TM_PROJ = 512
TM_MERGE = 512
TQ = 512
FF_CHUNK = 512


def _rms(x, g):
    ms = jnp.mean(x * x, axis=-1, keepdims=True)
    return x * lax.rsqrt(ms + EPS) * g


def _sigmoid(x):
    return 0.5 * (jnp.tanh(0.5 * x) + 1.0)


def _resident_spec(shape):
    nd = len(shape)
    return pl.BlockSpec(tuple(shape), lambda *_: (0,) * nd, pipeline_mode=pl.Buffered(1))


def _layer_spec(layer, shape):
    nd = len(shape)
    return pl.BlockSpec((None,) + tuple(shape), lambda *_: (layer,) + (0,) * nd,
                        pipeline_mode=pl.Buffered(1))


def _in_proj_kernel(x_ref, vec_ref, w_ref, ws_ref, bs_ref, pw_ref, wb32_ref, wo32_ref, wfi32_ref,
                    wfo32_ref, ya_ref, q_ref, k_ref, vt_ref, yc_ref, gate_ref, wb_ref, wo_ref,
                    wfi_ref, wfo_ref, cbuf):
    tm = x_ref.shape[0]
    pos = (pl.program_id(0) * tm) % SEQ
    gpre = vec_ref[0:1, :]
    lng, lnb, psc = (vec_ref[r:r + 1, :A_WIDTH] for r in (1, 2, 3))
    h = _rms(x_ref[...], gpre).astype(BF16)

    def proj(lo, width):
        return jnp.dot(h, w_ref[:, lo:lo + width].astype(BF16), preferred_element_type=F32)

    def gate(j):
        zg = proj(COL_G + j * D_MODEL, D_MODEL)
        gate_ref[:, j * D_MODEL:(j + 1) * D_MODEL] = _sigmoid(zg)


    zc = proj(COL_C, C_WIDTH)
    gate(0)

    @pl.when(pos == 0)
    def _():
        cbuf[0:HALO, :] = jnp.zeros((HALO, C_WIDTH), F32)

    cbuf[HALO:, :] = zc
    p_all = cbuf[...]
    cbuf[0:HALO, :] = p_all[tm:, :]
    s2 = p_all + pltpu.roll(p_all, 1, 0)
    rest = lambda a: a[:, C_GROUP_DIM:]
    s4 = rest(s2) + pltpu.roll(rest(s2), 2, 0)
    s8 = rest(s4) + pltpu.roll(rest(s4), 4, 0)
    s16 = rest(s8) + pltpu.roll(rest(s8), 8, 0)
    sums = [a[HALO:, :C_GROUP_DIM] for a in (s2, s4, s8, s16)]
    head_pos = 1 + lax.broadcasted_iota(jnp.int32, (HALO, C_GROUP_DIM), 0)
    pooled = []
    for g, w in enumerate(C_WINDOWS):
        cs = slice(g * C_GROUP_DIM, (g + 1) * C_GROUP_DIM)
        head_cnt = jnp.where(pos == 0, jnp.minimum(head_pos, w), w).astype(F32)
        mean = jnp.concatenate([sums[g][:HALO] * (1.0 / head_cnt), sums[g][HALO:] * (1.0 / w)],
                               axis=0)
        pooled.append((mean - zc[:, cs]).astype(BF16))
    zero_w = jnp.zeros((C_GROUP_DIM, C_GROUP_DIM), BF16)
    for g in range(0, len(C_WINDOWS), 2):
        cs = slice(g * C_GROUP_DIM, (g + 2) * C_GROUP_DIM)
        w_pair = jnp.concatenate([jnp.concatenate([pw_ref[g], zero_w], axis=1),
                                  jnp.concatenate([zero_w, pw_ref[g + 1]], axis=1)], axis=0)
        yc = jnp.dot(jnp.concatenate(pooled[g:g + 2], axis=1), w_pair,
                     preferred_element_type=F32) * psc[:, cs]
        yc_ref[:, cs] = yc.astype(BF16)

    za = proj(COL_A, 2 * A_WIDTH)
    gate(1)
    ga = 0.5 * za * (1.0 + lax.erf(za * math.sqrt(0.5)))
    u = ga[:, :A_WIDTH]
    vv = ga[:, A_WIDTH:]
    mu = jnp.mean(vv, axis=-1, keepdims=True)
    dv = vv - mu
    var = jnp.mean(dv * dv, axis=-1, keepdims=True)
    vn = (dv * lax.rsqrt(var + EPS) * lng + lnb).astype(BF16)
    trow = lax.broadcasted_iota(jnp.int32, (CHUNK, CHUNK), 0)
    tcol = lax.broadcasted_iota(jnp.int32, (CHUNK, CHUNK), 1)
    chunks = [slice(c * CHUNK, (c + 1) * CHUNK) for c in range(tm // CHUNK)]
    for g in range(A_GROUPS):
        wsg = jnp.where(tcol <= trow, ws_ref[g], 0.0).astype(BF16)
        cs = slice(g * CHUNK, (g + 1) * CHUNK)
        v_side = jnp.concatenate([vn[rs, cs] for rs in chunks], axis=1)
        mixed = jnp.dot(wsg, v_side, preferred_element_type=F32)
        for rs, ls in zip(chunks, chunks):
            ya_ref[rs, cs] = (u[rs, cs] * (mixed[:, ls] + bs_ref[g])).astype(BF16)

    gate(2)

    vt = proj(COL_V, B_WIDTH).T.astype(BF16)
    for j in range(tm // TQ):
        vt_ref[j] = vt[:, j * TQ:(j + 1) * TQ]
    q_ref[...] = (proj(COL_Q, QK_WIDTH) * (B_HEAD_DIM ** -0.5 * math.log2(math.e))).astype(BF16)
    k_ref[...] = proj(COL_K, QK_WIDTH).astype(BF16)

    for src, dst in ((wb32_ref, wb_ref), (wo32_ref, wo_ref), (wfi32_ref, wfi_ref),
                     (wfo32_ref, wfo_ref)):
        dst[...] = src[...].astype(BF16)


BF16_SUBLANES = 16


def _in_proj(x, vecs, w_in, ws, bs, pw, merge_weights, layer):
    n = x.shape[0]
    tm = TM_PROJ
    assert n % SEQ == 0 and SEQ % tm == 0 and tm % CHUNK == 0 and tm % TQ == 0
    steps = n // tm
    row = lambda width: pl.BlockSpec((tm, width), lambda i: (i, 0))
    out_shapes = [
        jax.ShapeDtypeStruct((n, A_WIDTH), BF16),
        jax.ShapeDtypeStruct((n, QK_WIDTH), BF16),
        jax.ShapeDtypeStruct((n, QK_WIDTH), BF16),
        jax.ShapeDtypeStruct((n // TQ, B_WIDTH, TQ), BF16),
        jax.ShapeDtypeStruct((n, C_WIDTH), BF16),
        jax.ShapeDtypeStruct((n, N_BRANCH * D_MODEL), F32),
    ]
    cast_in, cast_out = [], []
    for w in merge_weights:
        rows, cols = w.shape[1:]
        every = 1
        while rows * every % (steps * BF16_SUBLANES):
            every *= 2
        slab = rows * every // steps
        cast_in.append(pl.BlockSpec((None, slab, cols), lambda i, e=every: (layer, i // e, 0)))
        cast_out.append(pl.BlockSpec((slab, cols), lambda i, e=every: (i // e, 0)))
        out_shapes.append(jax.ShapeDtypeStruct((rows, cols), BF16))
    return pl.pallas_call(
        _in_proj_kernel,
        grid=(steps,),
        in_specs=[
            row(D_MODEL),
            _layer_spec(layer, (4, D_MODEL)),
            _layer_spec(layer, (D_MODEL, IN_TOTAL)),
            _layer_spec(layer, (A_GROUPS, CHUNK, CHUNK)),
            _layer_spec(layer, (A_GROUPS, CHUNK, CHUNK)),
            _layer_spec(layer, (len(C_WINDOWS), C_GROUP_DIM, C_GROUP_DIM)),
        ] + cast_in,
        out_specs=[row(A_WIDTH), row(QK_WIDTH), row(QK_WIDTH),
                   pl.BlockSpec((tm // TQ, B_WIDTH, TQ), lambda i: (i, 0, 0)), row(C_WIDTH),
                   row(N_BRANCH * D_MODEL)] + cast_out,
        out_shape=out_shapes,
        scratch_shapes=[pltpu.VMEM((HALO + tm, C_WIDTH), F32)],
        compiler_params=pltpu.CompilerParams(
            dimension_semantics=("arbitrary",), vmem_limit_bytes=VMEM_LIMIT),
        name=f"in_proj_l{layer}",
    )(x, vecs, w_in, ws, bs, pw, *merge_weights)


ONES_ROWS = 16
N_STREAMS = 2 * B_HEADS


def _attn_kernel(q_ref, qn_ref, k_ref, k0n_ref, vt_ref, lam_ref, sgb_ref,
                 o_ref, sa_scr, sb_scr, ma_scr, mb_scr, m_scr, acc_scr, *, lambda_init):
    b = pl.program_id(0)
    i = pl.program_id(1)
    tq = q_ref.shape[0]
    heads = [slice(h * B_VDIM, (h + 1) * B_VDIM) for h in range(B_HEADS)]
    lane = lax.broadcasted_iota(jnp.int32, (tq, B_VDIM), 1)
    zero = jnp.zeros((tq, B_VDIM), BF16)
    nt = (((1,), (1,)), ((), ()))

    def split_components(qr):
        out = []
        for hs in heads:
            qh = qr[:, hs]
            out += [jnp.where(lane < B_HEAD_DIM, qh, zero), jnp.where(lane >= B_HEAD_DIM, qh, zero)]
        return out

    def scores(qs, kb_of, s_scr, mx_scr):
        for h, hs in enumerate(heads):
            kb = kb_of(hs)
            for c in range(2):
                st = 2 * h + c
                s = lax.dot_general(kb, qs[st], nt, preferred_element_type=F32)
                s_scr[st] = s
                mx_scr[st] = jnp.max(s, axis=0, keepdims=True)

    def rescaled(st, m_new, first):
        if first:
            return None
        return jnp.exp2(m_scr[st] - m_new) * acc_scr[st]

    def plus(a, b):
        return b if a is None else a + b

    ones_rows = jnp.ones((ONES_ROWS, tq), BF16)

    def consume(j, s_scr, mx_scr, first):
        for h, hs in enumerate(heads):
            vtb = jnp.concatenate([vt_ref[j, hs, :], ones_rows], axis=0)
            for c in range(2):
                st = 2 * h + c
                mx = mx_scr[st]
                m_new = mx if first else jnp.maximum(m_scr[st], mx)
                p = jnp.exp2(s_scr[st] - m_new)
                acc_scr[st] = plus(rescaled(st, m_new, first),
                                   jnp.dot(vtb, p.astype(BF16), preferred_element_type=F32))
                m_scr[st] = m_new

    half = tq // 2
    causal_edge = lambda shape: (lax.broadcasted_iota(jnp.int32, shape, 0)
                                 <= lax.broadcasted_iota(jnp.int32, shape, 1))
    vis_top = causal_edge((half, tq))
    vis_bot = causal_edge((half, half))

    future = jnp.full((half, half), -jnp.inf, F32)
    ones_half = jnp.ones((ONES_ROWS, half), BF16)

    def diag_max(s_top, s_bot):
        s_bot_wide = jnp.concatenate([future, s_bot], axis=1)
        return jnp.maximum(jnp.max(s_top, axis=0, keepdims=True),
                           jnp.max(s_bot_wide, axis=0, keepdims=True))

    def scores_diag(n, qs, s_scr, mx_scr):
        top, bot = n * tq, n * tq + half
        for h, hs in enumerate(heads):
            k_top, k_bot = k_ref[top:top + half, hs], k_ref[bot:bot + half, hs]
            for c in range(2):
                st = 2 * h + c
                s_top = lax.dot_general(k_top, qs[st], nt, preferred_element_type=F32)
                s_bot = lax.dot_general(k_bot, qs[st][half:], nt, preferred_element_type=F32)
                s_top = jnp.where(vis_top, s_top, -jnp.inf)
                s_bot = jnp.where(vis_bot, s_bot, -jnp.inf)
                s_scr[st, 0:half, :] = s_top
                s_scr[st, half:, half:] = s_bot
                mx_scr[st] = diag_max(s_top, s_bot)

    def consume_diag(n, s_scr, mx_scr, first):
        for h, hs in enumerate(heads):
            vt_top = jnp.concatenate([vt_ref[n, hs, 0:half], ones_half], axis=0)
            vt_bot = jnp.concatenate([vt_ref[n, hs, half:], ones_half], axis=0)
            for c in range(2):
                st = 2 * h + c
                s_top, s_bot = s_scr[st, 0:half, :], s_scr[st, half:, half:]
                if first:
                    s_top = jnp.where(vis_top, s_top, -jnp.inf)
                    s_bot = jnp.where(vis_bot, s_bot, -jnp.inf)
                    m_new = diag_max(s_top, s_bot)
                else:
                    m_new = jnp.maximum(m_scr[st], mx_scr[st])
                a = rescaled(st, m_new, first)
                m_scr[st] = m_new
                p_top = jnp.exp2(s_top - m_new).astype(BF16)
                p_bot = jnp.exp2(s_bot - m_scr[st, :, half:]).astype(BF16)
                a = plus(a, jnp.dot(vt_top, p_top, preferred_element_type=F32))
                a_right = a[:, half:] + jnp.dot(vt_bot, p_bot, preferred_element_type=F32)
                acc_scr[st] = jnp.concatenate([a[:, :half], a_right], axis=1)
            finish_head(h, hs)

    lam = (jnp.exp(jnp.sum(lam_ref[0:1, :] * lam_ref[1:2, :], keepdims=True))
           - jnp.exp(jnp.sum(lam_ref[2:3, :] * lam_ref[3:4, :], keepdims=True)) + lambda_init)

    def finish_head(h, hs):
        a0, a1 = acc_scr[2 * h], acc_scr[2 * h + 1]
        l0, l1 = a0[B_VDIM:B_VDIM + 1], a1[B_VDIM:B_VDIM + 1]
        ot = a0[:B_VDIM] * (1.0 / l0) - a1[:B_VDIM] * (lam / l1)
        ms = jnp.mean(ot * ot, axis=0, keepdims=True)
        y = ot * lax.rsqrt(ms + EPS) * sgb_ref[...] * (1.0 - lambda_init)
        o_ref[:, hs] = y.T.astype(BF16)

    qs = split_components(q_ref)
    key_tile = lambda j: (lambda hs: k_ref[j * tq:(j + 1) * tq, hs])
    stage = ((sa_scr, ma_scr), (sb_scr, mb_scr))

    @pl.when((b == 0) & (i == 0))
    def _():
        scores(qs, key_tile(0), *stage[0])

    for n in range(k_ref.shape[0] // tq):
        @pl.when(i == n)
        def _(n=n):
            for j in range(n):
                if j + 1 < n:
                    scores(qs, key_tile(j + 1), *stage[(j + 1) % 2])
                else:
                    scores_diag(n, qs, *stage[n % 2])
                consume(j, *stage[j % 2], first=(j == 0))
            consume_diag(n, *stage[n % 2], first=(n == 0))
            scores(split_components(qn_ref), lambda hs: k0n_ref[:, hs], *stage[0])


def _attention(q, k, vt, lam_vecs, sgb, lambda_init, layer):
    n = q.shape[0]
    assert n % SEQ == 0 and SEQ % TQ == 0 and TQ % (2 * BF16_SUBLANES) == 0
    bsz = n // SEQ
    nq = SEQ // TQ
    last = bsz * nq - 1
    stream = lambda *shape: pltpu.VMEM((N_STREAMS,) + shape, F32)
    return pl.pallas_call(
        functools.partial(_attn_kernel, lambda_init=lambda_init),
        grid=(bsz, nq),
        in_specs=[
            pl.BlockSpec((TQ, B_WIDTH), lambda b, i: (b * nq + i, 0)),
            pl.BlockSpec((TQ, B_WIDTH), lambda b, i: (jnp.minimum(b * nq + i + 1, last), 0)),
            pl.BlockSpec((SEQ, B_WIDTH), lambda b, i: (b, 0)),
            pl.BlockSpec((TQ, B_WIDTH),
                         lambda b, i: (jnp.minimum(b * nq + i + 1, last) // nq * nq, 0)),
            pl.BlockSpec((nq, B_WIDTH, TQ), lambda b, i: (b, 0, 0)),
            _layer_spec(layer, (4, B_HEAD_DIM)),
            _layer_spec(layer, (B_VDIM, TQ)),
        ],
        out_specs=pl.BlockSpec((TQ, B_WIDTH), lambda b, i: (b * nq + i, 0)),
        out_shape=jax.ShapeDtypeStruct((n, B_WIDTH), BF16),
        scratch_shapes=[
            stream(TQ, TQ), stream(TQ, TQ),
            stream(1, TQ), stream(1, TQ),
            stream(1, TQ),
            stream(B_VDIM + ONES_ROWS, TQ),
        ],
        compiler_params=pltpu.CompilerParams(
            dimension_semantics=("arbitrary", "arbitrary"), vmem_limit_bytes=VMEM_LIMIT),
        name=f"attn_l{layer}",
    )(q, q, k, k, vt, lam_vecs, sgb)


def _merge_kernel(ya_ref, yb_ref, yc_ref, gate_ref, x_ref, wb_ref, wo_ref, vec_ref, wfi_ref, wfo_ref,
                  o_ref):
    gpost, gfpre, gfpost = (vec_ref[r:r + 1, :] for r in range(3))
    merged = None
    for j, y_ref in enumerate((ya_ref, yb_ref, yc_ref)):
        up = jnp.dot(y_ref[...], wb_ref[j * BRANCH_WIDTH:(j + 1) * BRANCH_WIDTH, :],
                     preferred_element_type=F32)
        term = gate_ref[:, j * D_MODEL:(j + 1) * D_MODEL] * up
        merged = term if merged is None else merged + term
    mix = jnp.dot(merged.astype(BF16), wo_ref[...], preferred_element_type=F32)
    x1 = x_ref[...] + _rms(mix, gpost)

    h = _rms(x1, gfpre).astype(BF16)
    f = None
    for lo in range(0, D_FF, FF_CHUNK):
        width = min(FF_CHUNK, D_FF - lo)
        g = jnp.dot(h, wfi_ref[:, lo:lo + width], preferred_element_type=F32)
        u = jnp.dot(h, wfi_ref[:, D_FF + lo:D_FF + lo + width], preferred_element_type=F32)
        a = (g * _sigmoid(g) * u).astype(BF16)
        part = jnp.dot(a, wfo_ref[lo:lo + width, :], preferred_element_type=F32)
        f = part if f is None else f + part
    o_ref[...] = x1 + _rms(f, gfpost)


def _merge(ya, yb, yc, gates, x, wb, wo, vecs, wfi, wfo, layer):
    n = x.shape[0]
    tm = TM_MERGE
    assert n % tm == 0
    row = lambda width: pl.BlockSpec((tm, width), lambda i: (i, 0))
    return pl.pallas_call(
        _merge_kernel,
        grid=(n // tm,),
        in_specs=[
            row(BRANCH_WIDTH), row(BRANCH_WIDTH), row(BRANCH_WIDTH),
            row(N_BRANCH * D_MODEL), row(D_MODEL),
            _resident_spec((N_BRANCH * BRANCH_WIDTH, D_MODEL)),
            _resident_spec((D_MODEL, D_MODEL)),
            _layer_spec(layer, (3, D_MODEL)),
            _resident_spec((D_MODEL, 2 * D_FF)),
            _resident_spec((D_FF, D_MODEL)),
        ],
        out_specs=row(D_MODEL),
        out_shape=jax.ShapeDtypeStruct((n, D_MODEL), F32),
        compiler_params=pltpu.CompilerParams(
            dimension_semantics=("arbitrary",), vmem_limit_bytes=VMEM_LIMIT),
        name=f"merge_l{layer}",
    )(ya, yb, yc, gates, x, wb, wo, vecs, wfi, wfo)


def kernel(x, norm_mix_pre, w_in, gmlp_norm_g, gmlp_norm_b, gmlp_w_s, gmlp_b_s, lambda_q1, lambda_k1,
           lambda_q2, lambda_k2, diff_subln_g, pool_w, pool_scale, w_branch, w_out, norm_mix_post,
           norm_ffn_pre, w_ffn_in, w_ffn_out, norm_ffn_post):
    bsz, s, d = x.shape
    assert (s, d) == (SEQ, D_MODEL)
    depth = w_in.shape[0]
    xf = x.reshape(bsz * s, d)
    pad = lambda a: jnp.pad(a, ((0, 0), (0, D_MODEL - a.shape[1])))
    proj_vecs = jnp.stack([norm_mix_pre, pad(gmlp_norm_g), pad(gmlp_norm_b), pad(pool_scale)], axis=1)
    lam_vecs = jnp.stack([lambda_q1, lambda_k1, lambda_q2, lambda_k2], axis=1)
    merge_vecs = jnp.stack([norm_mix_post, norm_ffn_pre, norm_ffn_post], axis=1)
    pool_w = pool_w.astype(BF16)
    merge_weights = (w_branch.reshape(depth, N_BRANCH * BRANCH_WIDTH, D_MODEL), w_out, w_ffn_in,
                     w_ffn_out)
    bs = jnp.broadcast_to(gmlp_b_s[:, :, :, None], (depth, A_GROUPS, CHUNK, CHUNK))
    sgb = jnp.broadcast_to(diff_subln_g[:, :, None], (depth, B_VDIM, TQ))
    for l in range(depth):
        lambda_init = 0.8 - 0.6 * math.exp(-0.3 * l)
        ya, q, k, vt, yc, gates, wb, wo, wfi, wfo = _in_proj(
            xf, proj_vecs, w_in, gmlp_w_s, bs, pool_w, merge_weights, l)
        yb = _attention(q, k, vt, lam_vecs, sgb, lambda_init, l)
        xf = _merge(ya, yb, yc, gates, xf, wb, wo, merge_vecs, wfi, wfo, l)
    return xf.reshape(bsz, s, d)
```

```python
import functools
import math

import jax
import jax.numpy as jnp
from jax import lax
from jax.experimental import pallas as pl
from jax.experimental.pallas import tpu as pltpu

F32 = jnp.float32
BF16 = jnp.bfloat16

D_MODEL = 1024
SEQ = 2048
CHUNK = 128
A_GROUPS = 4
A_WIDTH = 512
B_HEADS = 4
B_HEAD_DIM = 64
B_VDIM = 128
B_WIDTH = 512
QK_WIDTH = B_HEADS * 2 * B_HEAD_DIM
C_WINDOWS = (2, 4, 8, 16)
C_GROUP_DIM = 128
C_WIDTH = 512
N_BRANCH = 3
BRANCH_WIDTH = 512
D_FF = 2816
EPS = 1e-6

COL_A = 0
COL_Q = 2 * A_WIDTH
COL_K = COL_Q + QK_WIDTH
COL_V = COL_K + QK_WIDTH
COL_C = COL_V + B_WIDTH
COL_G = COL_C + C_WIDTH
IN_TOTAL = COL_G + N_BRANCH * D_MODEL

HALO = 16
VMEM_LIMIT = 58 * 1024 * 1024

TM_PROJ = 512
TM_MERGE = 512
TQ = 512
FF_CHUNK = 512


def _rms(x, g):
    ms = jnp.mean(x * x, axis=-1, keepdims=True)
    return x * lax.rsqrt(ms + EPS) * g


def _sigmoid(x):
    return 0.5 * (jnp.tanh(0.5 * x) + 1.0)


def _resident_spec(shape):
    nd = len(shape)
    return pl.BlockSpec(tuple(shape), lambda *_: (0,) * nd, pipeline_mode=pl.Buffered(1))


def _layer_spec(layer, shape):
    nd = len(shape)
    return pl.BlockSpec((None,) + tuple(shape), lambda *_: (layer,) + (0,) * nd,
                        pipeline_mode=pl.Buffered(1))


def _in_proj_kernel(x_ref, vec_ref, w_ref, ws_ref, bs_ref, pw_ref, wb32_ref, wo32_ref, wfi32_ref,
                    wfo32_ref, ya_ref, q_ref, k_ref, vt_ref, yc_ref, gate_ref, wb_ref, wo_ref,
                    wfi_ref, wfo_ref, cbuf):
    tm = x_ref.shape[0]
    pos = (pl.program_id(0) * tm) % SEQ
    gpre = vec_ref[0:1, :]
    lng, lnb, psc = (vec_ref[r:r + 1, :A_WIDTH] for r in (1, 2, 3))
    h = _rms(x_ref[...], gpre).astype(BF16)

    def proj(lo, width):
        return jnp.dot(h, w_ref[:, lo:lo + width].astype(BF16), preferred_element_type=F32)

    def gate(j):
        zg = proj(COL_G + j * D_MODEL, D_MODEL)
        gate_ref[:, j * D_MODEL:(j + 1) * D_MODEL] = _sigmoid(zg)


    zc = proj(COL_C, C_WIDTH)
    gate(0)

    @pl.when(pos == 0)
    def _():
        cbuf[0:HALO, :] = jnp.zeros((HALO, C_WIDTH), F32)

    cbuf[HALO:, :] = zc
    p_all = cbuf[...]
    cbuf[0:HALO, :] = p_all[tm:, :]
    s2 = p_all + pltpu.roll(p_all, 1, 0)
    rest = lambda a: a[:, C_GROUP_DIM:]
    s4 = rest(s2) + pltpu.roll(rest(s2), 2, 0)
    s8 = rest(s4) + pltpu.roll(rest(s4), 4, 0)
    s16 = rest(s8) + pltpu.roll(rest(s8), 8, 0)
    sums = [a[HALO:, :C_GROUP_DIM] for a in (s2, s4, s8, s16)]
    head_pos = 1 + lax.broadcasted_iota(jnp.int32, (HALO, C_GROUP_DIM), 0)
    pooled = []
    for g, w in enumerate(C_WINDOWS):
        cs = slice(g * C_GROUP_DIM, (g + 1) * C_GROUP_DIM)
        head_cnt = jnp.where(pos == 0, jnp.minimum(head_pos, w), w).astype(F32)
        mean = jnp.concatenate([sums[g][:HALO] * (1.0 / head_cnt), sums[g][HALO:] * (1.0 / w)],
                               axis=0)
        pooled.append((mean - zc[:, cs]).astype(BF16))
    zero_w = jnp.zeros((C_GROUP_DIM, C_GROUP_DIM), BF16)
    for g in range(0, len(C_WINDOWS), 2):
        cs = slice(g * C_GROUP_DIM, (g + 2) * C_GROUP_DIM)
        w_pair = jnp.concatenate([jnp.concatenate([pw_ref[g], zero_w], axis=1),
                                  jnp.concatenate([zero_w, pw_ref[g + 1]], axis=1)], axis=0)
        yc = jnp.dot(jnp.concatenate(pooled[g:g + 2], axis=1), w_pair,
                     preferred_element_type=F32) * psc[:, cs]
        yc_ref[:, cs] = yc.astype(BF16)

    za = proj(COL_A, 2 * A_WIDTH)
    gate(1)
    ga = 0.5 * za * (1.0 + lax.erf(za * math.sqrt(0.5)))
    u = ga[:, :A_WIDTH]
    vv = ga[:, A_WIDTH:]
    mu = jnp.mean(vv, axis=-1, keepdims=True)
    dv = vv - mu
    var = jnp.mean(dv * dv, axis=-1, keepdims=True)
    vn = (dv * lax.rsqrt(var + EPS) * lng + lnb).astype(BF16)
    trow = lax.broadcasted_iota(jnp.int32, (CHUNK, CHUNK), 0)
    tcol = lax.broadcasted_iota(jnp.int32, (CHUNK, CHUNK), 1)
    chunks = [slice(c * CHUNK, (c + 1) * CHUNK) for c in range(tm // CHUNK)]
    for g in range(A_GROUPS):
        wsg = jnp.where(tcol <= trow, ws_ref[g], 0.0).astype(BF16)
        cs = slice(g * CHUNK, (g + 1) * CHUNK)
        v_side = jnp.concatenate([vn[rs, cs] for rs in chunks], axis=1)
        mixed = jnp.dot(wsg, v_side, preferred_element_type=F32)
        for rs, ls in zip(chunks, chunks):
            ya_ref[rs, cs] = (u[rs, cs] * (mixed[:, ls] + bs_ref[g])).astype(BF16)

    gate(2)

    vt = proj(COL_V, B_WIDTH).T.astype(BF16)
    for j in range(tm // TQ):
        vt_ref[j] = vt[:, j * TQ:(j + 1) * TQ]
    q_ref[...] = (proj(COL_Q, QK_WIDTH) * (B_HEAD_DIM ** -0.5 * math.log2(math.e))).astype(BF16)
    k_ref[...] = proj(COL_K, QK_WIDTH).astype(BF16)

    for src, dst in ((wb32_ref, wb_ref), (wo32_ref, wo_ref), (wfi32_ref, wfi_ref),
                     (wfo32_ref, wfo_ref)):
        dst[...] = src[...].astype(BF16)


BF16_SUBLANES = 16


def _in_proj(x, vecs, w_in, ws, bs, pw, merge_weights, layer):
    n = x.shape[0]
    tm = TM_PROJ
    assert n % SEQ == 0 and SEQ % tm == 0 and tm % CHUNK == 0 and tm % TQ == 0
    steps = n // tm
    row = lambda width: pl.BlockSpec((tm, width), lambda i: (i, 0))
    out_shapes = [
        jax.ShapeDtypeStruct((n, A_WIDTH), BF16),
        jax.ShapeDtypeStruct((n, QK_WIDTH), BF16),
        jax.ShapeDtypeStruct((n, QK_WIDTH), BF16),
        jax.ShapeDtypeStruct((n // TQ, B_WIDTH, TQ), BF16),
        jax.ShapeDtypeStruct((n, C_WIDTH), BF16),
        jax.ShapeDtypeStruct((n, N_BRANCH * D_MODEL), F32),
    ]
    cast_in, cast_out = [], []
    for w in merge_weights:
        rows, cols = w.shape[1:]
        every = 1
        while rows * every % (steps * BF16_SUBLANES):
            every *= 2
        slab = rows * every // steps
        cast_in.append(pl.BlockSpec((None, slab, cols), lambda i, e=every: (layer, i // e, 0)))
        cast_out.append(pl.BlockSpec((slab, cols), lambda i, e=every: (i // e, 0)))
        out_shapes.append(jax.ShapeDtypeStruct((rows, cols), BF16))
    return pl.pallas_call(
        _in_proj_kernel,
        grid=(steps,),
        in_specs=[
            row(D_MODEL),
            _layer_spec(layer, (4, D_MODEL)),
            _layer_spec(layer, (D_MODEL, IN_TOTAL)),
            _layer_spec(layer, (A_GROUPS, CHUNK, CHUNK)),
            _layer_spec(layer, (A_GROUPS, CHUNK, CHUNK)),
            _layer_spec(layer, (len(C_WINDOWS), C_GROUP_DIM, C_GROUP_DIM)),
        ] + cast_in,
        out_specs=[row(A_WIDTH), row(QK_WIDTH), row(QK_WIDTH),
                   pl.BlockSpec((tm // TQ, B_WIDTH, TQ), lambda i: (i, 0, 0)), row(C_WIDTH),
                   row(N_BRANCH * D_MODEL)] + cast_out,
        out_shape=out_shapes,
        scratch_shapes=[pltpu.VMEM((HALO + tm, C_WIDTH), F32)],
        compiler_params=pltpu.CompilerParams(
            dimension_semantics=("arbitrary",), vmem_limit_bytes=VMEM_LIMIT),
        name=f"in_proj_l{layer}",
    )(x, vecs, w_in, ws, bs, pw, *merge_weights)


ONES_ROWS = 16
N_STREAMS = 2 * B_HEADS


def _attn_kernel(q_ref, qn_ref, k_ref, k0n_ref, vt_ref, lam_ref, sgb_ref,
                 o_ref, sa_scr, sb_scr, ma_scr, mb_scr, m_scr, acc_scr, *, lambda_init):
    b = pl.program_id(0)
    tq = qn_ref.shape[0]
    heads = [slice(h * B_VDIM, (h + 1) * B_VDIM) for h in range(B_HEADS)]
    lane = lax.broadcasted_iota(jnp.int32, (tq, B_VDIM), 1)
    zero = jnp.zeros((tq, B_VDIM), BF16)
    nt = (((1,), (1,)), ((), ()))

    def split_components(qr):
        out = []
        for hs in heads:
            qh = qr[:, hs]
            out += [jnp.where(lane < B_HEAD_DIM, qh, zero), jnp.where(lane >= B_HEAD_DIM, qh, zero)]
        return out

    def scores(qs, kb_of, s_scr, mx_scr):
        for h, hs in enumerate(heads):
            kb = kb_of(hs)
            for c in range(2):
                st = 2 * h + c
                s = lax.dot_general(kb, qs[st], nt, preferred_element_type=F32)
                s_scr[st] = s
                mx_scr[st] = jnp.max(s, axis=0, keepdims=True)

    def rescaled(st, m_new, first):
        if first:
            return None
        return jnp.exp2(m_scr[st] - m_new) * acc_scr[st]

    def plus(a, b):
        return b if a is None else a + b

    ones_rows = jnp.ones((ONES_ROWS, tq), BF16)

    def consume(j, s_scr, mx_scr, first):
        for h, hs in enumerate(heads):
            vtb = jnp.concatenate([vt_ref[j, hs, :], ones_rows], axis=0)
            for c in range(2):
                st = 2 * h + c
                mx = mx_scr[st]
                m_new = mx if first else jnp.maximum(m_scr[st], mx)
                p = jnp.exp2(s_scr[st] - m_new)
                acc_scr[st] = plus(rescaled(st, m_new, first),
                                   jnp.dot(vtb, p.astype(BF16), preferred_element_type=F32))
                m_scr[st] = m_new

    half = tq // 2
    causal_edge = lambda shape: (lax.broadcasted_iota(jnp.int32, shape, 0)
                                 <= lax.broadcasted_iota(jnp.int32, shape, 1))
    vis_top = causal_edge((half, tq))
    vis_bot = causal_edge((half, half))

    future = jnp.full((half, half), -jnp.inf, F32)
    ones_half = jnp.ones((ONES_ROWS, half), BF16)

    def diag_max(s_top, s_bot):
        s_bot_wide = jnp.concatenate([future, s_bot], axis=1)
        return jnp.maximum(jnp.max(s_top, axis=0, keepdims=True),
                           jnp.max(s_bot_wide, axis=0, keepdims=True))

    def scores_diag(n, qs, s_scr, mx_scr):
        top, bot = n * tq, n * tq + half
        for h, hs in enumerate(heads):
            k_top, k_bot = k_ref[top:top + half, hs], k_ref[bot:bot + half, hs]
            for c in range(2):
                st = 2 * h + c
                s_top = lax.dot_general(k_top, qs[st], nt, preferred_element_type=F32)
                s_bot = lax.dot_general(k_bot, qs[st][half:], nt, preferred_element_type=F32)
                s_top = jnp.where(vis_top, s_top, -jnp.inf)
                s_bot = jnp.where(vis_bot, s_bot, -jnp.inf)
                s_scr[st, 0:half, :] = s_top
                s_scr[st, half:, half:] = s_bot
                mx_scr[st] = diag_max(s_top, s_bot)

    def consume_diag(n, s_scr, mx_scr, first):
        for h, hs in enumerate(heads):
            vt_top = jnp.concatenate([vt_ref[n, hs, 0:half], ones_half], axis=0)
            vt_bot = jnp.concatenate([vt_ref[n, hs, half:], ones_half], axis=0)
            for c in range(2):
                st = 2 * h + c
                s_top, s_bot = s_scr[st, 0:half, :], s_scr[st, half:, half:]
                if first:
                    s_top = jnp.where(vis_top, s_top, -jnp.inf)
                    s_bot = jnp.where(vis_bot, s_bot, -jnp.inf)
                    m_new = diag_max(s_top, s_bot)
                else:
                    m_new = jnp.maximum(m_scr[st], mx_scr[st])
                a = rescaled(st, m_new, first)
                m_scr[st] = m_new
                p_top = jnp.exp2(s_top - m_new).astype(BF16)
                p_bot = jnp.exp2(s_bot - m_scr[st, :, half:]).astype(BF16)
                a = plus(a, jnp.dot(vt_top, p_top, preferred_element_type=F32))
                a_right = a[:, half:] + jnp.dot(vt_bot, p_bot, preferred_element_type=F32)
                acc_scr[st] = jnp.concatenate([a[:, :half], a_right], axis=1)
            finish_head(n, h, hs)

    lam = (jnp.exp(jnp.sum(lam_ref[0:1, :] * lam_ref[1:2, :], keepdims=True))
           - jnp.exp(jnp.sum(lam_ref[2:3, :] * lam_ref[3:4, :], keepdims=True)) + lambda_init)

    def finish_head(n, h, hs):
        a0, a1 = acc_scr[2 * h], acc_scr[2 * h + 1]
        l0, l1 = a0[B_VDIM:B_VDIM + 1], a1[B_VDIM:B_VDIM + 1]
        ot = a0[:B_VDIM] * (1.0 / l0) - a1[:B_VDIM] * (lam / l1)
        ms = jnp.mean(ot * ot, axis=0, keepdims=True)
        y = ot * lax.rsqrt(ms + EPS) * sgb_ref[...] * (1.0 - lambda_init)
        o_ref[n * tq:(n + 1) * tq, hs] = y.T.astype(BF16)

    query_tile = lambda n: split_components(q_ref[n * tq:(n + 1) * tq, :])
    key_tile = lambda j: (lambda hs: k_ref[j * tq:(j + 1) * tq, hs])
    stage = ((sa_scr, ma_scr), (sb_scr, mb_scr))
    nq = k_ref.shape[0] // tq

    @pl.when(b == 0)
    def _():
        scores(query_tile(0), key_tile(0), *stage[0])

    for n in range(nq):
        qs = query_tile(n)
        for j in range(n):
            if j + 1 < n:
                scores(qs, key_tile(j + 1), *stage[(j + 1) % 2])
            else:
                scores_diag(n, qs, *stage[n % 2])
            consume(j, *stage[j % 2], first=(j == 0))
        consume_diag(n, *stage[n % 2], first=(n == 0))
        if n + 1 < nq:
            scores(query_tile(n + 1), key_tile(0), *stage[0])
        else:
            scores(split_components(qn_ref[...]), lambda hs: k0n_ref[:, hs], *stage[0])


def _attention(q, k, vt, lam_vecs, sgb, lambda_init, layer):
    n = q.shape[0]
    assert n % SEQ == 0 and SEQ % TQ == 0 and TQ % (2 * BF16_SUBLANES) == 0
    bsz = n // SEQ
    nq = SEQ // TQ
    next_first = lambda b: (jnp.minimum(b + 1, bsz - 1) * nq, 0)
    stream = lambda *shape: pltpu.VMEM((N_STREAMS,) + shape, F32)
    return pl.pallas_call(
        functools.partial(_attn_kernel, lambda_init=lambda_init),
        grid=(bsz,),
        in_specs=[
            pl.BlockSpec((SEQ, B_WIDTH), lambda b: (b, 0)),
            pl.BlockSpec((TQ, B_WIDTH), next_first),
            pl.BlockSpec((SEQ, B_WIDTH), lambda b: (b, 0)),
            pl.BlockSpec((TQ, B_WIDTH), next_first),
            pl.BlockSpec((nq, B_WIDTH, TQ), lambda b: (b, 0, 0)),
            _layer_spec(layer, (4, B_HEAD_DIM)),
            _layer_spec(layer, (B_VDIM, TQ)),
        ],
        out_specs=pl.BlockSpec((SEQ, B_WIDTH), lambda b: (b, 0)),
        out_shape=jax.ShapeDtypeStruct((n, B_WIDTH), BF16),
        scratch_shapes=[
            stream(TQ, TQ), stream(TQ, TQ),
            stream(1, TQ), stream(1, TQ),
            stream(1, TQ),
            stream(B_VDIM + ONES_ROWS, TQ),
        ],
        compiler_params=pltpu.CompilerParams(
            dimension_semantics=("arbitrary",), vmem_limit_bytes=VMEM_LIMIT),
        name=f"attn_l{layer}",
    )(q, q, k, k, vt, lam_vecs, sgb)


def _merge_kernel(ya_ref, yb_ref, yc_ref, gate_ref, x_ref, wb_ref, wo_ref, vec_ref, wfi_ref, wfo_ref,
                  o_ref):
    gpost, gfpre, gfpost = (vec_ref[r:r + 1, :] for r in range(3))
    merged = None
    for j, y_ref in enumerate((ya_ref, yb_ref, yc_ref)):
        up = jnp.dot(y_ref[...], wb_ref[j * BRANCH_WIDTH:(j + 1) * BRANCH_WIDTH, :],
                     preferred_element_type=F32)
        term = gate_ref[:, j * D_MODEL:(j + 1) * D_MODEL] * up
        merged = term if merged is None else merged + term
    mix = jnp.dot(merged.astype(BF16), wo_ref[...], preferred_element_type=F32)
    x1 = x_ref[...] + _rms(mix, gpost)

    h = _rms(x1, gfpre).astype(BF16)
    f = None
    for lo in range(0, D_FF, FF_CHUNK):
        width = min(FF_CHUNK, D_FF - lo)
        g = jnp.dot(h, wfi_ref[:, lo:lo + width], preferred_element_type=F32)
        u = jnp.dot(h, wfi_ref[:, D_FF + lo:D_FF + lo + width], preferred_element_type=F32)
        a = (g * _sigmoid(g) * u).astype(BF16)
        part = jnp.dot(a, wfo_ref[lo:lo + width, :], preferred_element_type=F32)
        f = part if f is None else f + part
    o_ref[...] = x1 + _rms(f, gfpost)


def _merge(ya, yb, yc, gates, x, wb, wo, vecs, wfi, wfo, layer):
    n = x.shape[0]
    tm = TM_MERGE
    assert n % tm == 0
    row = lambda width: pl.BlockSpec((tm, width), lambda i: (i, 0))
    return pl.pallas_call(
        _merge_kernel,
        grid=(n // tm,),
        in_specs=[
            row(BRANCH_WIDTH), row(BRANCH_WIDTH), row(BRANCH_WIDTH),
            row(N_BRANCH * D_MODEL), row(D_MODEL),
            _resident_spec((N_BRANCH * BRANCH_WIDTH, D_MODEL)),
            _resident_spec((D_MODEL, D_MODEL)),
            _layer_spec(layer, (3, D_MODEL)),
            _resident_spec((D_MODEL, 2 * D_FF)),
            _resident_spec((D_FF, D_MODEL)),
        ],
        out_specs=row(D_MODEL),
        out_shape=jax.ShapeDtypeStruct((n, D_MODEL), F32),
        compiler_params=pltpu.CompilerParams(
            dimension_semantics=("arbitrary",), vmem_limit_bytes=VMEM_LIMIT),
        name=f"merge_l{layer}",
    )(ya, yb, yc, gates, x, wb, wo, vecs, wfi, wfo)


def kernel(x, norm_mix_pre, w_in, gmlp_norm_g, gmlp_norm_b, gmlp_w_s, gmlp_b_s, lambda_q1, lambda_k1,
           lambda_q2, lambda_k2, diff_subln_g, pool_w, pool_scale, w_branch, w_out, norm_mix_post,
           norm_ffn_pre, w_ffn_in, w_ffn_out, norm_ffn_post):
    bsz, s, d = x.shape
    assert (s, d) == (SEQ, D_MODEL)
    depth = w_in.shape[0]
    xf = x.reshape(bsz * s, d)
    pad = lambda a: jnp.pad(a, ((0, 0), (0, D_MODEL - a.shape[1])))
    proj_vecs = jnp.stack([norm_mix_pre, pad(gmlp_norm_g), pad(gmlp_norm_b), pad(pool_scale)], axis=1)
    lam_vecs = jnp.stack([lambda_q1, lambda_k1, lambda_q2, lambda_k2], axis=1)
    merge_vecs = jnp.stack([norm_mix_post, norm_ffn_pre, norm_ffn_post], axis=1)
    pool_w = pool_w.astype(BF16)
    merge_weights = (w_branch.reshape(depth, N_BRANCH * BRANCH_WIDTH, D_MODEL), w_out, w_ffn_in,
                     w_ffn_out)
    bs = jnp.broadcast_to(gmlp_b_s[:, :, :, None], (depth, A_GROUPS, CHUNK, CHUNK))
    sgb = jnp.broadcast_to(diff_subln_g[:, :, None], (depth, B_VDIM, TQ))
    for l in range(depth):
        lambda_init = 0.8 - 0.6 * math.exp(-0.3 * l)
        ya, q, k, vt, yc, gates, wb, wo, wfi, wfo = _in_proj(
            xf, proj_vecs, w_in, gmlp_w_s, bs, pool_w, merge_weights, l)
        yb = _attention(q, k, vt, lam_vecs, sgb, lambda_init, l)
        xf = _merge(ya, yb, yc, gates, xf, wb, wo, merge_vecs, wfi, wfo, l)
    return xf.reshape(bsz, s, d)
```

```python
import functools
import math

import jax
import jax.numpy as jnp
from jax import lax
from jax.experimental import pallas as pl
from jax.experimental.pallas import tpu as pltpu

F32 = jnp.float32
BF16 = jnp.bfloat16

D_MODEL = 1024
SEQ = 2048
CHUNK = 128
A_GROUPS = 4
A_WIDTH = 512
B_HEADS = 4
B_HEAD_DIM = 64
B_VDIM = 128
B_WIDTH = 512
QK_WIDTH = B_HEADS * 2 * B_HEAD_DIM
C_WINDOWS = (2, 4, 8, 16)
C_GROUP_DIM = 128
C_WIDTH = 512
N_BRANCH = 3
BRANCH_WIDTH = 512
D_FF = 2816
EPS = 1e-6

COL_A = 0
COL_Q = 2 * A_WIDTH
COL_K = COL_Q + QK_WIDTH
COL_V = COL_K + QK_WIDTH
COL_C = COL_V + B_WIDTH
COL_G = COL_C + C_WIDTH
IN_TOTAL = COL_G + N_BRANCH * D_MODEL

HALO = 16
VMEM_LIMIT = 58 * 1024 * 1024

TM_PROJ = 512
TM_MERGE = 512
TQ = 512
FF_CHUNK = 512


def _rms(x, g):
    ms = jnp.mean(x * x, axis=-1, keepdims=True)
    return x * lax.rsqrt(ms + EPS) * g


def _sigmoid(x):
    return 0.5 * (jnp.tanh(0.5 * x) + 1.0)


def _resident_spec(shape):
    nd = len(shape)
    return pl.BlockSpec(tuple(shape), lambda *_: (0,) * nd, pipeline_mode=pl.Buffered(1))


def _layer_spec(layer, shape):
    nd = len(shape)
    return pl.BlockSpec((None,) + tuple(shape), lambda *_: (layer,) + (0,) * nd,
                        pipeline_mode=pl.Buffered(1))


def _in_proj_kernel(x_ref, vec_ref, w_ref, ws_ref, bs_ref, pw_ref, wb32_ref, wo32_ref, wfi32_ref,
                    wfo32_ref, ya_ref, q_ref, k_ref, vt_ref, yc_ref, gate_ref, wb_ref, wo_ref,
                    wfi_ref, wfo_ref, hist_ref):
    tm = x_ref.shape[0]
    pos = (pl.program_id(0) * tm) % SEQ

    @pl.when(pl.program_id(0) == 0)
    def _():
        hist_ref[...] = jnp.zeros(hist_ref.shape, F32)

    gpre = vec_ref[0:1, :]
    lng, lnb, psc = (vec_ref[r:r + 1, :A_WIDTH] for r in (1, 2, 3))
    h = _rms(x_ref[...], gpre).astype(BF16)

    def proj(lo, width):
        return jnp.dot(h, w_ref[:, lo:lo + width].astype(BF16), preferred_element_type=F32)

    def gate(j):
        zg = proj(COL_G + j * D_MODEL, D_MODEL)
        gate_ref[:, j * D_MODEL:(j + 1) * D_MODEL] = _sigmoid(zg)


    zc = proj(COL_C, C_WIDTH)
    gate(0)

    history = jnp.where(pos == 0, 0.0, hist_ref[...])
    p_all = jnp.concatenate([history, zc], axis=0)
    hist_ref[...] = zc[tm - HALO:, :]
    s2 = p_all + pltpu.roll(p_all, 1, 0)
    rest = lambda a: a[:, C_GROUP_DIM:]
    s4 = rest(s2) + pltpu.roll(rest(s2), 2, 0)
    s8 = rest(s4) + pltpu.roll(rest(s4), 4, 0)
    s16 = rest(s8) + pltpu.roll(rest(s8), 8, 0)
    sums = [a[HALO:, :C_GROUP_DIM] for a in (s2, s4, s8, s16)]
    head_pos = 1 + lax.broadcasted_iota(jnp.int32, (HALO, C_GROUP_DIM), 0)
    pooled = []
    for g, w in enumerate(C_WINDOWS):
        cs = slice(g * C_GROUP_DIM, (g + 1) * C_GROUP_DIM)
        head_cnt = jnp.where(pos == 0, jnp.minimum(head_pos, w), w).astype(F32)
        mean = jnp.concatenate([sums[g][:HALO] * (1.0 / head_cnt), sums[g][HALO:] * (1.0 / w)],
                               axis=0)
        pooled.append((mean - zc[:, cs]).astype(BF16))
    zero_w = jnp.zeros((C_GROUP_DIM, C_GROUP_DIM), BF16)
    for g in range(0, len(C_WINDOWS), 2):
        cs = slice(g * C_GROUP_DIM, (g + 2) * C_GROUP_DIM)
        w_pair = jnp.concatenate([jnp.concatenate([pw_ref[g], zero_w], axis=1),
                                  jnp.concatenate([zero_w, pw_ref[g + 1]], axis=1)], axis=0)
        yc = jnp.dot(jnp.concatenate(pooled[g:g + 2], axis=1), w_pair,
                     preferred_element_type=F32) * psc[:, cs]
        yc_ref[:, cs] = yc.astype(BF16)

    za = proj(COL_A, 2 * A_WIDTH)
    gate(1)
    ga = 0.5 * za * (1.0 + lax.erf(za * math.sqrt(0.5)))
    u = ga[:, :A_WIDTH]
    vv = ga[:, A_WIDTH:]
    mu = jnp.mean(vv, axis=-1, keepdims=True)
    dv = vv - mu
    var = jnp.mean(dv * dv, axis=-1, keepdims=True)
    vn = (dv * lax.rsqrt(var + EPS) * lng + lnb).astype(BF16)
    trow = lax.broadcasted_iota(jnp.int32, (CHUNK, CHUNK), 0)
    tcol = lax.broadcasted_iota(jnp.int32, (CHUNK, CHUNK), 1)
    chunks = [slice(c * CHUNK, (c + 1) * CHUNK) for c in range(tm // CHUNK)]
    for g in range(A_GROUPS):
        wsg = jnp.where(tcol <= trow, ws_ref[g], 0.0).astype(BF16)
        cs = slice(g * CHUNK, (g + 1) * CHUNK)
        v_side = jnp.concatenate([vn[rs, cs] for rs in chunks], axis=1)
        mixed = jnp.dot(wsg, v_side, preferred_element_type=F32)
        for rs, ls in zip(chunks, chunks):
            ya_ref[rs, cs] = (u[rs, cs] * (mixed[:, ls] + bs_ref[g])).astype(BF16)

    gate(2)

    vt = proj(COL_V, B_WIDTH).T.astype(BF16)
    for j in range(tm // TQ):
        vt_ref[j] = vt[:, j * TQ:(j + 1) * TQ]
    q_ref[...] = (proj(COL_Q, QK_WIDTH) * (B_HEAD_DIM ** -0.5 * math.log2(math.e))).astype(BF16)
    k_ref[...] = proj(COL_K, QK_WIDTH).astype(BF16)

    for src, dst in ((wb32_ref, wb_ref), (wo32_ref, wo_ref), (wfi32_ref, wfi_ref),
                     (wfo32_ref, wfo_ref)):
        dst[...] = src[...].astype(BF16)


BF16_SUBLANES = 16


def _in_proj(x, vecs, w_in, ws, bs, pw, merge_weights, layer):
    n = x.shape[0]
    tm = TM_PROJ
    assert n % SEQ == 0 and SEQ % tm == 0 and tm % CHUNK == 0 and tm % TQ == 0
    steps = n // tm
    row = lambda width: pl.BlockSpec((tm, width), lambda i: (i, 0))
    out_shapes = [
        jax.ShapeDtypeStruct((n, A_WIDTH), BF16),
        jax.ShapeDtypeStruct((n, QK_WIDTH), BF16),
        jax.ShapeDtypeStruct((n, QK_WIDTH), BF16),
        jax.ShapeDtypeStruct((n // TQ, B_WIDTH, TQ), BF16),
        jax.ShapeDtypeStruct((n, C_WIDTH), BF16),
        jax.ShapeDtypeStruct((n, N_BRANCH * D_MODEL), F32),
    ]
    cast_in, cast_out = [], []
    for w in merge_weights:
        rows, cols = w.shape[1:]
        every = 1
        while rows * every % (steps * BF16_SUBLANES):
            every *= 2
        slab = rows * every // steps
        cast_in.append(pl.BlockSpec((None, slab, cols), lambda i, e=every: (layer, i // e, 0)))
        cast_out.append(pl.BlockSpec((slab, cols), lambda i, e=every: (i // e, 0)))
        out_shapes.append(jax.ShapeDtypeStruct((rows, cols), BF16))
    return pl.pallas_call(
        _in_proj_kernel,
        grid=(steps,),
        in_specs=[
            row(D_MODEL),
            _layer_spec(layer, (4, D_MODEL)),
            _layer_spec(layer, (D_MODEL, IN_TOTAL)),
            _layer_spec(layer, (A_GROUPS, CHUNK, CHUNK)),
            _layer_spec(layer, (A_GROUPS, CHUNK, CHUNK)),
            _layer_spec(layer, (len(C_WINDOWS), C_GROUP_DIM, C_GROUP_DIM)),
        ] + cast_in,
        out_specs=[row(A_WIDTH), row(QK_WIDTH), row(QK_WIDTH),
                   pl.BlockSpec((tm // TQ, B_WIDTH, TQ), lambda i: (i, 0, 0)), row(C_WIDTH),
                   row(N_BRANCH * D_MODEL)] + cast_out,
        out_shape=out_shapes,
        scratch_shapes=[pltpu.VMEM((HALO, C_WIDTH), F32)],
        compiler_params=pltpu.CompilerParams(
            dimension_semantics=("arbitrary",), vmem_limit_bytes=VMEM_LIMIT),
        name=f"in_proj_l{layer}",
    )(x, vecs, w_in, ws, bs, pw, *merge_weights)


ONES_ROWS = 16
N_STREAMS = 2 * B_HEADS


def _attn_kernel(q_ref, qn_ref, k_ref, k0n_ref, vt_ref, lam_ref, sgb_ref,
                 o_ref, sa_scr, sb_scr, ma_scr, mb_scr, m_scr, acc_scr, *, lambda_init):
    b = pl.program_id(0)
    tq = qn_ref.shape[0]
    heads = [slice(h * B_VDIM, (h + 1) * B_VDIM) for h in range(B_HEADS)]
    lane = lax.broadcasted_iota(jnp.int32, (tq, B_VDIM), 1)
    zero = jnp.zeros((tq, B_VDIM), BF16)
    nt = (((1,), (1,)), ((), ()))

    def split_components(qr):
        out = []
        for hs in heads:
            qh = qr[:, hs]
            out += [jnp.where(lane < B_HEAD_DIM, qh, zero), jnp.where(lane >= B_HEAD_DIM, qh, zero)]
        return out

    def scores(qs, kb_of, s_scr, mx_scr):
        for h, hs in enumerate(heads):
            kb = kb_of(hs)
            for c in range(2):
                st = 2 * h + c
                s = lax.dot_general(kb, qs[st], nt, preferred_element_type=F32)
                s_scr[st] = s
                mx_scr[st] = jnp.max(s, axis=0, keepdims=True)

    def rescaled(st, m_new, first):
        if first:
            return None
        return jnp.exp2(m_scr[st] - m_new) * acc_scr[st]

    def plus(a, b):
        return b if a is None else a + b

    ones_rows = jnp.ones((ONES_ROWS, tq), BF16)

    def consume(j, s_scr, mx_scr, first):
        for h, hs in enumerate(heads):
            vtb = jnp.concatenate([vt_ref[j, hs, :], ones_rows], axis=0)
            for c in range(2):
                st = 2 * h + c
                mx = mx_scr[st]
                m_new = mx if first else jnp.maximum(m_scr[st], mx)
                p = jnp.exp2(s_scr[st] - m_new)
                acc_scr[st] = plus(rescaled(st, m_new, first),
                                   jnp.dot(vtb, p.astype(BF16), preferred_element_type=F32))
                m_scr[st] = m_new

    half = tq // 2
    causal_edge = lambda shape: (lax.broadcasted_iota(jnp.int32, shape, 0)
                                 <= lax.broadcasted_iota(jnp.int32, shape, 1))
    vis_top = causal_edge((half, tq))
    vis_bot = causal_edge((half, half))

    future = jnp.full((half, half), -jnp.inf, F32)
    ones_half = jnp.ones((ONES_ROWS, half), BF16)

    def diag_max(s_top, s_bot):
        s_bot_wide = jnp.concatenate([future, s_bot], axis=1)
        return jnp.maximum(jnp.max(s_top, axis=0, keepdims=True),
                           jnp.max(s_bot_wide, axis=0, keepdims=True))

    def scores_diag(n, qs, s_scr, mx_scr):
        top, bot = n * tq, n * tq + half
        for h, hs in enumerate(heads):
            k_top, k_bot = k_ref[top:top + half, hs], k_ref[bot:bot + half, hs]
            for c in range(2):
                st = 2 * h + c
                s_top = lax.dot_general(k_top, qs[st], nt, preferred_element_type=F32)
                s_bot = lax.dot_general(k_bot, qs[st][half:], nt, preferred_element_type=F32)
                s_top = jnp.where(vis_top, s_top, -jnp.inf)
                s_bot = jnp.where(vis_bot, s_bot, -jnp.inf)
                s_scr[st, 0:half, :] = s_top
                s_scr[st, half:, half:] = s_bot
                mx_scr[st] = diag_max(s_top, s_bot)

    def consume_diag(n, s_scr, mx_scr, first):
        for h, hs in enumerate(heads):
            vt_top = jnp.concatenate([vt_ref[n, hs, 0:half], ones_half], axis=0)
            vt_bot = jnp.concatenate([vt_ref[n, hs, half:], ones_half], axis=0)
            for c in range(2):
                st = 2 * h + c
                s_top, s_bot = s_scr[st, 0:half, :], s_scr[st, half:, half:]
                if first:
                    s_top = jnp.where(vis_top, s_top, -jnp.inf)
                    s_bot = jnp.where(vis_bot, s_bot, -jnp.inf)
                    m_new = diag_max(s_top, s_bot)
                else:
                    m_new = jnp.maximum(m_scr[st], mx_scr[st])
                a = rescaled(st, m_new, first)
                m_scr[st] = m_new
                p_top = jnp.exp2(s_top - m_new).astype(BF16)
                p_bot = jnp.exp2(s_bot - m_scr[st, :, half:]).astype(BF16)
                a = plus(a, jnp.dot(vt_top, p_top, preferred_element_type=F32))
                a_right = a[:, half:] + jnp.dot(vt_bot, p_bot, preferred_element_type=F32)
                acc_scr[st] = jnp.concatenate([a[:, :half], a_right], axis=1)
            finish_head(n, h, hs)

    lam = (jnp.exp(jnp.sum(lam_ref[0:1, :] * lam_ref[1:2, :], keepdims=True))
           - jnp.exp(jnp.sum(lam_ref[2:3, :] * lam_ref[3:4, :], keepdims=True)) + lambda_init)

    def finish_head(n, h, hs):
        a0, a1 = acc_scr[2 * h], acc_scr[2 * h + 1]
        l0, l1 = a0[B_VDIM:B_VDIM + 1], a1[B_VDIM:B_VDIM + 1]
        ot = a0[:B_VDIM] * (1.0 / l0) - a1[:B_VDIM] * (lam / l1)
        ms = jnp.mean(ot * ot, axis=0, keepdims=True)
        y = ot * lax.rsqrt(ms + EPS) * sgb_ref[...] * (1.0 - lambda_init)
        o_ref[n * tq:(n + 1) * tq, hs] = y.T.astype(BF16)

    query_tile = lambda n: split_components(q_ref[n * tq:(n + 1) * tq, :])
    key_tile = lambda j: (lambda hs: k_ref[j * tq:(j + 1) * tq, hs])
    stage = ((sa_scr, ma_scr), (sb_scr, mb_scr))
    nq = k_ref.shape[0] // tq

    @pl.when(b == 0)
    def _():
        scores(query_tile(0), key_tile(0), *stage[0])

    for n in range(nq):
        qs = query_tile(n)
        for j in range(n):
            if j + 1 < n:
                scores(qs, key_tile(j + 1), *stage[(j + 1) % 2])
            else:
                scores_diag(n, qs, *stage[n % 2])
            consume(j, *stage[j % 2], first=(j == 0))
        consume_diag(n, *stage[n % 2], first=(n == 0))
        if n + 1 < nq:
            scores(query_tile(n + 1), key_tile(0), *stage[0])
        else:
            scores(split_components(qn_ref[...]), lambda hs: k0n_ref[:, hs], *stage[0])


def _attention(q, k, vt, lam_vecs, sgb, lambda_init, layer):
    n = q.shape[0]
    assert n % SEQ == 0 and SEQ % TQ == 0 and TQ % (2 * BF16_SUBLANES) == 0
    bsz = n // SEQ
    nq = SEQ // TQ
    next_first = lambda b: (jnp.minimum(b + 1, bsz - 1) * nq, 0)
    stream = lambda *shape: pltpu.VMEM((N_STREAMS,) + shape, F32)
    return pl.pallas_call(
        functools.partial(_attn_kernel, lambda_init=lambda_init),
        grid=(bsz,),
        in_specs=[
            pl.BlockSpec((SEQ, B_WIDTH), lambda b: (b, 0)),
            pl.BlockSpec((TQ, B_WIDTH), next_first),
            pl.BlockSpec((SEQ, B_WIDTH), lambda b: (b, 0)),
            pl.BlockSpec((TQ, B_WIDTH), next_first),
            pl.BlockSpec((nq, B_WIDTH, TQ), lambda b: (b, 0, 0)),
            _layer_spec(layer, (4, B_HEAD_DIM)),
            _layer_spec(layer, (B_VDIM, TQ)),
        ],
        out_specs=pl.BlockSpec((SEQ, B_WIDTH), lambda b: (b, 0)),
        out_shape=jax.ShapeDtypeStruct((n, B_WIDTH), BF16),
        scratch_shapes=[
            stream(TQ, TQ), stream(TQ, TQ),
            stream(1, TQ), stream(1, TQ),
            stream(1, TQ),
            stream(B_VDIM + ONES_ROWS, TQ),
        ],
        compiler_params=pltpu.CompilerParams(
            dimension_semantics=("arbitrary",), vmem_limit_bytes=VMEM_LIMIT),
        name=f"attn_l{layer}",
    )(q, q, k, k, vt, lam_vecs, sgb)


def _merge_kernel(ya_ref, yb_ref, yc_ref, gate_ref, x_ref, wb_ref, wo_ref, vec_ref, wfi_ref, wfo_ref,
                  o_ref):
    gpost, gfpre, gfpost = (vec_ref[r:r + 1, :] for r in range(3))
    merged = None
    for j, y_ref in enumerate((ya_ref, yb_ref, yc_ref)):
        up = jnp.dot(y_ref[...], wb_ref[j * BRANCH_WIDTH:(j + 1) * BRANCH_WIDTH, :],
                     preferred_element_type=F32)
        term = gate_ref[:, j * D_MODEL:(j + 1) * D_MODEL] * up
        merged = term if merged is None else merged + term
    mix = jnp.dot(merged.astype(BF16), wo_ref[...], preferred_element_type=F32)
    x1 = x_ref[...] + _rms(mix, gpost)

    h = _rms(x1, gfpre).astype(BF16)
    f = None
    for lo in range(0, D_FF, FF_CHUNK):
        width = min(FF_CHUNK, D_FF - lo)
        g = jnp.dot(h, wfi_ref[:, lo:lo + width], preferred_element_type=F32)
        u = jnp.dot(h, wfi_ref[:, D_FF + lo:D_FF + lo + width], preferred_element_type=F32)
        a = (g * _sigmoid(g) * u).astype(BF16)
        part = jnp.dot(a, wfo_ref[lo:lo + width, :], preferred_element_type=F32)
        f = part if f is None else f + part
    o_ref[...] = x1 + _rms(f, gfpost)


def _merge(ya, yb, yc, gates, x, wb, wo, vecs, wfi, wfo, layer):
    n = x.shape[0]
    tm = TM_MERGE
    assert n % tm == 0
    row = lambda width: pl.BlockSpec((tm, width), lambda i: (i, 0))
    return pl.pallas_call(
        _merge_kernel,
        grid=(n // tm,),
        in_specs=[
            row(BRANCH_WIDTH), row(BRANCH_WIDTH), row(BRANCH_WIDTH),
            row(N_BRANCH * D_MODEL), row(D_MODEL),
            _resident_spec((N_BRANCH * BRANCH_WIDTH, D_MODEL)),
            _resident_spec((D_MODEL, D_MODEL)),
            _layer_spec(layer, (3, D_MODEL)),
            _resident_spec((D_MODEL, 2 * D_FF)),
            _resident_spec((D_FF, D_MODEL)),
        ],
        out_specs=row(D_MODEL),
        out_shape=jax.ShapeDtypeStruct((n, D_MODEL), F32),
        compiler_params=pltpu.CompilerParams(
            dimension_semantics=("arbitrary",), vmem_limit_bytes=VMEM_LIMIT),
        name=f"merge_l{layer}",
    )(ya, yb, yc, gates, x, wb, wo, vecs, wfi, wfo)


def kernel(x, norm_mix_pre, w_in, gmlp_norm_g, gmlp_norm_b, gmlp_w_s, gmlp_b_s, lambda_q1, lambda_k1,
           lambda_q2, lambda_k2, diff_subln_g, pool_w, pool_scale, w_branch, w_out, norm_mix_post,
           norm_ffn_pre, w_ffn_in, w_ffn_out, norm_ffn_post):
    bsz, s, d = x.shape
    assert (s, d) == (SEQ, D_MODEL)
    depth = w_in.shape[0]
    xf = x.reshape(bsz * s, d)
    pad = lambda a: jnp.pad(a, ((0, 0), (0, D_MODEL - a.shape[1])))
    proj_vecs = jnp.stack([norm_mix_pre, pad(gmlp_norm_g), pad(gmlp_norm_b), pad(pool_scale)], axis=1)
    lam_vecs = jnp.stack([lambda_q1, lambda_k1, lambda_q2, lambda_k2], axis=1)
    merge_vecs = jnp.stack([norm_mix_post, norm_ffn_pre, norm_ffn_post], axis=1)
    pool_w = pool_w.astype(BF16)
    merge_weights = (w_branch.reshape(depth, N_BRANCH * BRANCH_WIDTH, D_MODEL), w_out, w_ffn_in,
                     w_ffn_out)
    bs = jnp.broadcast_to(gmlp_b_s[:, :, :, None], (depth, A_GROUPS, CHUNK, CHUNK))
    sgb = jnp.broadcast_to(diff_subln_g[:, :, None], (depth, B_VDIM, TQ))
    for l in range(depth):
        lambda_init = 0.8 - 0.6 * math.exp(-0.3 * l)
        ya, q, k, vt, yc, gates, wb, wo, wfi, wfo = _in_proj(
            xf, proj_vecs, w_in, gmlp_w_s, bs, pool_w, merge_weights, l)
        yb = _attention(q, k, vt, lam_vecs, sgb, lambda_init, l)
        xf = _merge(ya, yb, yc, gates, xf, wb, wo, merge_vecs, wfi, wfo, l)
    return xf.reshape(bsz, s, d)
```

```python
import functools
import math

import jax
import jax.numpy as jnp
from jax import lax
from jax.experimental import pallas as pl
from jax.experimental.pallas import tpu as pltpu

F32 = jnp.float32
BF16 = jnp.bfloat16

D_MODEL = 1024
SEQ = 2048
CHUNK = 128
A_GROUPS = 4
A_WIDTH = 512
B_HEADS = 4
B_HEAD_DIM = 64
B_VDIM = 128
B_WIDTH = 512
QK_WIDTH = B_HEADS * 2 * B_HEAD_DIM
C_WINDOWS = (2, 4, 8, 16)
C_GROUP_DIM = 128
C_WIDTH = 512
N_BRANCH = 3
BRANCH_WIDTH = 512
D_FF = 2816
EPS = 1e-6

COL_A = 0
COL_Q = 2 * A_WIDTH
COL_K = COL_Q + QK_WIDTH
COL_V = COL_K + QK_WIDTH
COL_C = COL_V + B_WIDTH
COL_G = COL_C + C_WIDTH
IN_TOTAL = COL_G + N_BRANCH * D_MODEL

HALO = 16
VMEM_LIMIT = 58 * 1024 * 1024

TM_PROJ = 512
TM_MERGE = 512
TQ = 512
FF_CHUNK = 512


def _rms(x, g):
    ms = jnp.mean(x * x, axis=-1, keepdims=True)
    return x * lax.rsqrt(ms + EPS) * g


def _sigmoid(x):
    return 0.5 * (jnp.tanh(0.5 * x) + 1.0)


def _resident_spec(shape):
    nd = len(shape)
    return pl.BlockSpec(tuple(shape), lambda *_: (0,) * nd, pipeline_mode=pl.Buffered(1))


def _layer_spec(layer, shape):
    nd = len(shape)
    return pl.BlockSpec((None,) + tuple(shape), lambda *_: (layer,) + (0,) * nd,
                        pipeline_mode=pl.Buffered(1))


def _in_proj_kernel(x_ref, vec_ref, w_ref, ws_ref, bs_ref, pw_ref, wb32_ref, wo32_ref, wfi32_ref,
                    wfo32_ref, ya_ref, q_ref, k_ref, vt_ref, yc_ref, gate_ref, wb_ref, wo_ref,
                    wfi_ref, wfo_ref, hist_ref):
    tm = x_ref.shape[0]
    pos = (pl.program_id(0) * tm) % SEQ

    @pl.when(pl.program_id(0) == 0)
    def _():
        hist_ref[...] = jnp.zeros(hist_ref.shape, F32)

    gpre = vec_ref[0:1, :]
    lng, lnb, psc = (vec_ref[r:r + 1, :A_WIDTH] for r in (1, 2, 3))
    h = _rms(x_ref[...], gpre).astype(BF16)

    def proj(lo, width):
        return jnp.dot(h, w_ref[:, lo:lo + width].astype(BF16), preferred_element_type=F32)

    def gate(j):
        zg = proj(COL_G + j * D_MODEL, D_MODEL)
        gate_ref[:, j * D_MODEL:(j + 1) * D_MODEL] = _sigmoid(zg)


    zc = proj(COL_C, C_WIDTH)
    za = proj(COL_A, 2 * A_WIDTH)
    gate(0)

    history = jnp.where(pos == 0, 0.0, hist_ref[...])
    p_all = jnp.concatenate([history, zc], axis=0)
    hist_ref[...] = zc[tm - HALO:, :]
    s2 = p_all + pltpu.roll(p_all, 1, 0)
    rest = lambda a: a[:, C_GROUP_DIM:]
    s4 = rest(s2) + pltpu.roll(rest(s2), 2, 0)
    s8 = rest(s4) + pltpu.roll(rest(s4), 4, 0)
    s16 = rest(s8) + pltpu.roll(rest(s8), 8, 0)
    sums = [a[HALO:, :C_GROUP_DIM] for a in (s2, s4, s8, s16)]
    head_pos = 1 + lax.broadcasted_iota(jnp.int32, (HALO, C_GROUP_DIM), 0)
    pooled = []
    for g, w in enumerate(C_WINDOWS):
        cs = slice(g * C_GROUP_DIM, (g + 1) * C_GROUP_DIM)
        head_cnt = jnp.where(pos == 0, jnp.minimum(head_pos, w), w).astype(F32)
        mean = jnp.concatenate([sums[g][:HALO] * (1.0 / head_cnt), sums[g][HALO:] * (1.0 / w)],
                               axis=0)
        pooled.append((mean - zc[:, cs]).astype(BF16))
    zero_w = jnp.zeros((C_GROUP_DIM, C_GROUP_DIM), BF16)
    for g in range(0, len(C_WINDOWS), 2):
        cs = slice(g * C_GROUP_DIM, (g + 2) * C_GROUP_DIM)
        w_pair = jnp.concatenate([jnp.concatenate([pw_ref[g], zero_w], axis=1),
                                  jnp.concatenate([zero_w, pw_ref[g + 1]], axis=1)], axis=0)
        yc = jnp.dot(jnp.concatenate(pooled[g:g + 2], axis=1), w_pair,
                     preferred_element_type=F32) * psc[:, cs]
        yc_ref[:, cs] = yc.astype(BF16)

    gate(1)
    ga = 0.5 * za * (1.0 + lax.erf(za * math.sqrt(0.5)))
    u = ga[:, :A_WIDTH]
    vv = ga[:, A_WIDTH:]
    mu = jnp.mean(vv, axis=-1, keepdims=True)
    dv = vv - mu
    var = jnp.mean(dv * dv, axis=-1, keepdims=True)
    vn = (dv * lax.rsqrt(var + EPS) * lng + lnb).astype(BF16)
    trow = lax.broadcasted_iota(jnp.int32, (CHUNK, CHUNK), 0)
    tcol = lax.broadcasted_iota(jnp.int32, (CHUNK, CHUNK), 1)
    chunks = [slice(c * CHUNK, (c + 1) * CHUNK) for c in range(tm // CHUNK)]
    for g in range(A_GROUPS):
        wsg = jnp.where(tcol <= trow, ws_ref[g], 0.0).astype(BF16)
        cs = slice(g * CHUNK, (g + 1) * CHUNK)
        v_side = jnp.concatenate([vn[rs, cs] for rs in chunks], axis=1)
        mixed = jnp.dot(wsg, v_side, preferred_element_type=F32)
        for rs, ls in zip(chunks, chunks):
            ya_ref[rs, cs] = (u[rs, cs] * (mixed[:, ls] + bs_ref[g])).astype(BF16)

    gate(2)

    vt = proj(COL_V, B_WIDTH).T.astype(BF16)
    for j in range(tm // TQ):
        vt_ref[j] = vt[:, j * TQ:(j + 1) * TQ]
    q_ref[...] = (proj(COL_Q, QK_WIDTH) * (B_HEAD_DIM ** -0.5 * math.log2(math.e))).astype(BF16)
    k_ref[...] = proj(COL_K, QK_WIDTH).astype(BF16)

    for src, dst in ((wb32_ref, wb_ref), (wo32_ref, wo_ref), (wfi32_ref, wfi_ref),
                     (wfo32_ref, wfo_ref)):
        dst[...] = src[...].astype(BF16)


BF16_SUBLANES = 16


def _in_proj(x, vecs, w_in, ws, bs, pw, merge_weights, layer):
    n = x.shape[0]
    tm = TM_PROJ
    assert n % SEQ == 0 and SEQ % tm == 0 and tm % CHUNK == 0 and tm % TQ == 0
    steps = n // tm
    row = lambda width: pl.BlockSpec((tm, width), lambda i: (i, 0))
    out_shapes = [
        jax.ShapeDtypeStruct((n, A_WIDTH), BF16),
        jax.ShapeDtypeStruct((n, QK_WIDTH), BF16),
        jax.ShapeDtypeStruct((n, QK_WIDTH), BF16),
        jax.ShapeDtypeStruct((n // TQ, B_WIDTH, TQ), BF16),
        jax.ShapeDtypeStruct((n, C_WIDTH), BF16),
        jax.ShapeDtypeStruct((n, N_BRANCH * D_MODEL), F32),
    ]
    cast_in, cast_out = [], []
    for w in merge_weights:
        rows, cols = w.shape[1:]
        every = 1
        while rows * every % (steps * BF16_SUBLANES):
            every *= 2
        slab = rows * every // steps
        cast_in.append(pl.BlockSpec((None, slab, cols), lambda i, e=every: (layer, i // e, 0)))
        cast_out.append(pl.BlockSpec((slab, cols), lambda i, e=every: (i // e, 0)))
        out_shapes.append(jax.ShapeDtypeStruct((rows, cols), BF16))
    return pl.pallas_call(
        _in_proj_kernel,
        grid=(steps,),
        in_specs=[
            row(D_MODEL),
            _layer_spec(layer, (4, D_MODEL)),
            _layer_spec(layer, (D_MODEL, IN_TOTAL)),
            _layer_spec(layer, (A_GROUPS, CHUNK, CHUNK)),
            _layer_spec(layer, (A_GROUPS, CHUNK, CHUNK)),
            _layer_spec(layer, (len(C_WINDOWS), C_GROUP_DIM, C_GROUP_DIM)),
        ] + cast_in,
        out_specs=[row(A_WIDTH), row(QK_WIDTH), row(QK_WIDTH),
                   pl.BlockSpec((tm // TQ, B_WIDTH, TQ), lambda i: (i, 0, 0)), row(C_WIDTH),
                   row(N_BRANCH * D_MODEL)] + cast_out,
        out_shape=out_shapes,
        scratch_shapes=[pltpu.VMEM((HALO, C_WIDTH), F32)],
        compiler_params=pltpu.CompilerParams(
            dimension_semantics=("arbitrary",), vmem_limit_bytes=VMEM_LIMIT),
        name=f"in_proj_l{layer}",
    )(x, vecs, w_in, ws, bs, pw, *merge_weights)


ONES_ROWS = 16
N_STREAMS = 2 * B_HEADS


def _attn_kernel(q_ref, qn_ref, k_ref, k0n_ref, vt_ref, lam_ref, sgb_ref,
                 o_ref, sa_scr, sb_scr, ma_scr, mb_scr, m_scr, acc_scr, *, lambda_init):
    b = pl.program_id(0)
    tq = qn_ref.shape[0]
    heads = [slice(h * B_VDIM, (h + 1) * B_VDIM) for h in range(B_HEADS)]
    lane = lax.broadcasted_iota(jnp.int32, (tq, B_VDIM), 1)
    zero = jnp.zeros((tq, B_VDIM), BF16)
    nt = (((1,), (1,)), ((), ()))

    def split_components(qr):
        out = []
        for hs in heads:
            qh = qr[:, hs]
            out += [jnp.where(lane < B_HEAD_DIM, qh, zero), jnp.where(lane >= B_HEAD_DIM, qh, zero)]
        return out

    def scores(qs, kb_of, s_scr, mx_scr):
        for h, hs in enumerate(heads):
            kb = kb_of(hs)
            for c in range(2):
                st = 2 * h + c
                s = lax.dot_general(kb, qs[st], nt, preferred_element_type=F32)
                s_scr[st] = s
                mx_scr[st] = jnp.max(s, axis=0, keepdims=True)

    def rescaled(st, m_new, first):
        if first:
            return None
        return jnp.exp2(m_scr[st] - m_new) * acc_scr[st]

    def plus(a, b):
        return b if a is None else a + b

    ones_rows = jnp.ones((ONES_ROWS, tq), BF16)

    def consume(j, s_scr, mx_scr, first):
        for h, hs in enumerate(heads):
            vtb = jnp.concatenate([vt_ref[j, hs, :], ones_rows], axis=0)
            for c in range(2):
                st = 2 * h + c
                mx = mx_scr[st]
                m_new = mx if first else jnp.maximum(m_scr[st], mx)
                p = jnp.exp2(s_scr[st] - m_new)
                acc_scr[st] = plus(rescaled(st, m_new, first),
                                   jnp.dot(vtb, p.astype(BF16), preferred_element_type=F32))
                m_scr[st] = m_new

    half = tq // 2
    causal_edge = lambda shape: (lax.broadcasted_iota(jnp.int32, shape, 0)
                                 <= lax.broadcasted_iota(jnp.int32, shape, 1))
    vis_top = causal_edge((half, tq))
    vis_bot = causal_edge((half, half))

    future = jnp.full((half, half), -jnp.inf, F32)
    ones_half = jnp.ones((ONES_ROWS, half), BF16)

    def diag_max(s_top, s_bot):
        s_bot_wide = jnp.concatenate([future, s_bot], axis=1)
        return jnp.maximum(jnp.max(s_top, axis=0, keepdims=True),
                           jnp.max(s_bot_wide, axis=0, keepdims=True))

    def scores_diag(n, qs, s_scr, mx_scr):
        top, bot = n * tq, n * tq + half
        for h, hs in enumerate(heads):
            k_top, k_bot = k_ref[top:top + half, hs], k_ref[bot:bot + half, hs]
            for c in range(2):
                st = 2 * h + c
                s_top = lax.dot_general(k_top, qs[st], nt, preferred_element_type=F32)
                s_bot = lax.dot_general(k_bot, qs[st][half:], nt, preferred_element_type=F32)
                s_top = jnp.where(vis_top, s_top, -jnp.inf)
                s_bot = jnp.where(vis_bot, s_bot, -jnp.inf)
                s_scr[st, 0:half, :] = s_top
                s_scr[st, half:, half:] = s_bot
                mx_scr[st] = diag_max(s_top, s_bot)

    def consume_diag(n, s_scr, mx_scr, first):
        for h, hs in enumerate(heads):
            vt_top = jnp.concatenate([vt_ref[n, hs, 0:half], ones_half], axis=0)
            vt_bot = jnp.concatenate([vt_ref[n, hs, half:], ones_half], axis=0)
            for c in range(2):
                st = 2 * h + c
                s_top, s_bot = s_scr[st, 0:half, :], s_scr[st, half:, half:]
                if first:
                    s_top = jnp.where(vis_top, s_top, -jnp.inf)
                    s_bot = jnp.where(vis_bot, s_bot, -jnp.inf)
                    m_new = diag_max(s_top, s_bot)
                else:
                    m_new = jnp.maximum(m_scr[st], mx_scr[st])
                a = rescaled(st, m_new, first)
                m_scr[st] = m_new
                p_top = jnp.exp2(s_top - m_new).astype(BF16)
                p_bot = jnp.exp2(s_bot - m_scr[st, :, half:]).astype(BF16)
                a = plus(a, jnp.dot(vt_top, p_top, preferred_element_type=F32))
                a_right = a[:, half:] + jnp.dot(vt_bot, p_bot, preferred_element_type=F32)
                acc_scr[st] = jnp.concatenate([a[:, :half], a_right], axis=1)
            finish_head(n, h, hs)

    lam = (jnp.exp(jnp.sum(lam_ref[0:1, :] * lam_ref[1:2, :], keepdims=True))
           - jnp.exp(jnp.sum(lam_ref[2:3, :] * lam_ref[3:4, :], keepdims=True)) + lambda_init)

    def finish_head(n, h, hs):
        a0, a1 = acc_scr[2 * h], acc_scr[2 * h + 1]
        l0, l1 = a0[B_VDIM:B_VDIM + 1], a1[B_VDIM:B_VDIM + 1]
        ot = a0[:B_VDIM] * (1.0 / l0) - a1[:B_VDIM] * (lam / l1)
        ms = jnp.mean(ot * ot, axis=0, keepdims=True)
        y = ot * lax.rsqrt(ms + EPS) * sgb_ref[...] * (1.0 - lambda_init)
        o_ref[n * tq:(n + 1) * tq, hs] = y.T.astype(BF16)

    query_tile = lambda n: split_components(q_ref[n * tq:(n + 1) * tq, :])
    key_tile = lambda j: (lambda hs: k_ref[j * tq:(j + 1) * tq, hs])
    stage = ((sa_scr, ma_scr), (sb_scr, mb_scr))
    nq = k_ref.shape[0] // tq

    @pl.when(b == 0)
    def _():
        scores(query_tile(0), key_tile(0), *stage[0])

    for n in range(nq):
        qs = query_tile(n)
        for j in range(n):
            if j + 1 < n:
                scores(qs, key_tile(j + 1), *stage[(j + 1) % 2])
            else:
                scores_diag(n, qs, *stage[n % 2])
            consume(j, *stage[j % 2], first=(j == 0))
        consume_diag(n, *stage[n % 2], first=(n == 0))
        if n + 1 < nq:
            scores(query_tile(n + 1), key_tile(0), *stage[0])
        else:
            scores(split_components(qn_ref[...]), lambda hs: k0n_ref[:, hs], *stage[0])


def _attention(q, k, vt, lam_vecs, sgb, lambda_init, layer):
    n = q.shape[0]
    assert n % SEQ == 0 and SEQ % TQ == 0 and TQ % (2 * BF16_SUBLANES) == 0
    bsz = n // SEQ
    nq = SEQ // TQ
    next_first = lambda b: (jnp.minimum(b + 1, bsz - 1) * nq, 0)
    stream = lambda *shape: pltpu.VMEM((N_STREAMS,) + shape, F32)
    return pl.pallas_call(
        functools.partial(_attn_kernel, lambda_init=lambda_init),
        grid=(bsz,),
        in_specs=[
            pl.BlockSpec((SEQ, B_WIDTH), lambda b: (b, 0)),
            pl.BlockSpec((TQ, B_WIDTH), next_first),
            pl.BlockSpec((SEQ, B_WIDTH), lambda b: (b, 0)),
            pl.BlockSpec((TQ, B_WIDTH), next_first),
            pl.BlockSpec((nq, B_WIDTH, TQ), lambda b: (b, 0, 0)),
            _layer_spec(layer, (4, B_HEAD_DIM)),
            _layer_spec(layer, (B_VDIM, TQ)),
        ],
        out_specs=pl.BlockSpec((SEQ, B_WIDTH), lambda b: (b, 0)),
        out_shape=jax.ShapeDtypeStruct((n, B_WIDTH), BF16),
        scratch_shapes=[
            stream(TQ, TQ), stream(TQ, TQ),
            stream(1, TQ), stream(1, TQ),
            stream(1, TQ),
            stream(B_VDIM + ONES_ROWS, TQ),
        ],
        compiler_params=pltpu.CompilerParams(
            dimension_semantics=("arbitrary",), vmem_limit_bytes=VMEM_LIMIT),
        name=f"attn_l{layer}",
    )(q, q, k, k, vt, lam_vecs, sgb)


def _merge_kernel(ya_ref, yb_ref, yc_ref, gate_ref, x_ref, wb_ref, wo_ref, vec_ref, wfi_ref, wfo_ref,
                  o_ref):
    gpost, gfpre, gfpost = (vec_ref[r:r + 1, :] for r in range(3))
    merged = None
    for j, y_ref in enumerate((ya_ref, yb_ref, yc_ref)):
        up = jnp.dot(y_ref[...], wb_ref[j * BRANCH_WIDTH:(j + 1) * BRANCH_WIDTH, :],
                     preferred_element_type=F32)
        term = gate_ref[:, j * D_MODEL:(j + 1) * D_MODEL] * up
        merged = term if merged is None else merged + term
    mix = jnp.dot(merged.astype(BF16), wo_ref[...], preferred_element_type=F32)
    x1 = x_ref[...] + _rms(mix, gpost)

    h = _rms(x1, gfpre).astype(BF16)
    f = None
    for lo in range(0, D_FF, FF_CHUNK):
        width = min(FF_CHUNK, D_FF - lo)
        g = jnp.dot(h, wfi_ref[:, lo:lo + width], preferred_element_type=F32)
        u = jnp.dot(h, wfi_ref[:, D_FF + lo:D_FF + lo + width], preferred_element_type=F32)
        a = (g * _sigmoid(g) * u).astype(BF16)
        part = jnp.dot(a, wfo_ref[lo:lo + width, :], preferred_element_type=F32)
        f = part if f is None else f + part
    o_ref[...] = x1 + _rms(f, gfpost)


def _merge(ya, yb, yc, gates, x, wb, wo, vecs, wfi, wfo, layer):
    n = x.shape[0]
    tm = TM_MERGE
    assert n % tm == 0
    row = lambda width: pl.BlockSpec((tm, width), lambda i: (i, 0))
    return pl.pallas_call(
        _merge_kernel,
        grid=(n // tm,),
        in_specs=[
            row(BRANCH_WIDTH), row(BRANCH_WIDTH), row(BRANCH_WIDTH),
            row(N_BRANCH * D_MODEL), row(D_MODEL),
            _resident_spec((N_BRANCH * BRANCH_WIDTH, D_MODEL)),
            _resident_spec((D_MODEL, D_MODEL)),
            _layer_spec(layer, (3, D_MODEL)),
            _resident_spec((D_MODEL, 2 * D_FF)),
            _resident_spec((D_FF, D_MODEL)),
        ],
        out_specs=row(D_MODEL),
        out_shape=jax.ShapeDtypeStruct((n, D_MODEL), F32),
        compiler_params=pltpu.CompilerParams(
            dimension_semantics=("arbitrary",), vmem_limit_bytes=VMEM_LIMIT),
        name=f"merge_l{layer}",
    )(ya, yb, yc, gates, x, wb, wo, vecs, wfi, wfo)


def kernel(x, norm_mix_pre, w_in, gmlp_norm_g, gmlp_norm_b, gmlp_w_s, gmlp_b_s, lambda_q1, lambda_k1,
           lambda_q2, lambda_k2, diff_subln_g, pool_w, pool_scale, w_branch, w_out, norm_mix_post,
           norm_ffn_pre, w_ffn_in, w_ffn_out, norm_ffn_post):
    bsz, s, d = x.shape
    assert (s, d) == (SEQ, D_MODEL)
    depth = w_in.shape[0]
    xf = x.reshape(bsz * s, d)
    pad = lambda a: jnp.pad(a, ((0, 0), (0, D_MODEL - a.shape[1])))
    proj_vecs = jnp.stack([norm_mix_pre, pad(gmlp_norm_g), pad(gmlp_norm_b), pad(pool_scale)], axis=1)
    lam_vecs = jnp.stack([lambda_q1, lambda_k1, lambda_q2, lambda_k2], axis=1)
    merge_vecs = jnp.stack([norm_mix_post, norm_ffn_pre, norm_ffn_post], axis=1)
    pool_w = pool_w.astype(BF16)
    merge_weights = (w_branch.reshape(depth, N_BRANCH * BRANCH_WIDTH, D_MODEL), w_out, w_ffn_in,
                     w_ffn_out)
    bs = jnp.broadcast_to(gmlp_b_s[:, :, :, None], (depth, A_GROUPS, CHUNK, CHUNK))
    sgb = jnp.broadcast_to(diff_subln_g[:, :, None], (depth, B_VDIM, TQ))
    for l in range(depth):
        lambda_init = 0.8 - 0.6 * math.exp(-0.3 * l)
        ya, q, k, vt, yc, gates, wb, wo, wfi, wfo = _in_proj(
            xf, proj_vecs, w_in, gmlp_w_s, bs, pool_w, merge_weights, l)
        yb = _attention(q, k, vt, lam_vecs, sgb, lambda_init, l)
        xf = _merge(ya, yb, yc, gates, xf, wb, wo, merge_vecs, wfi, wfo, l)
    return xf.reshape(bsz, s, d)
```

```python
import functools
import math

import jax
import jax.numpy as jnp
from jax import lax
from jax.experimental import pallas as pl
from jax.experimental.pallas import tpu as pltpu

F32 = jnp.float32
BF16 = jnp.bfloat16

D_MODEL = 1024
SEQ = 2048
CHUNK = 128
A_GROUPS = 4
A_WIDTH = 512
B_HEADS = 4
B_HEAD_DIM = 64
B_VDIM = 128
B_WIDTH = 512
QK_WIDTH = B_HEADS * 2 * B_HEAD_DIM
C_WINDOWS = (2, 4, 8, 16)
C_GROUP_DIM = 128
C_WIDTH = 512
N_BRANCH = 3
BRANCH_WIDTH = 512
D_FF = 2816
EPS = 1e-6

COL_A = 0
COL_Q = 2 * A_WIDTH
COL_K = COL_Q + QK_WIDTH
COL_V = COL_K + QK_WIDTH
COL_C = COL_V + B_WIDTH
COL_G = COL_C + C_WIDTH
IN_TOTAL = COL_G + N_BRANCH * D_MODEL

HALO = 16
VMEM_LIMIT = 58 * 1024 * 1024

TM_PROJ = 512
TM_MERGE = 512
TQ = 512
FF_CHUNK = 512


def _rms(x, g):
    ms = jnp.mean(x * x, axis=-1, keepdims=True)
    return x * lax.rsqrt(ms + EPS) * g


def _sigmoid(x):
    return 0.5 * (jnp.tanh(0.5 * x) + 1.0)


def _resident_spec(shape):
    nd = len(shape)
    return pl.BlockSpec(tuple(shape), lambda *_: (0,) * nd, pipeline_mode=pl.Buffered(1))


def _layer_spec(layer, shape):
    nd = len(shape)
    return pl.BlockSpec((None,) + tuple(shape), lambda *_: (layer,) + (0,) * nd,
                        pipeline_mode=pl.Buffered(1))


def _in_proj_kernel(x_ref, vec_ref, w_ref, ws_ref, bs_ref, pw_ref, wb32_ref, wo32_ref, wfi32_ref,
                    wfo32_ref, ya_ref, q_ref, k_ref, vt_ref, yc_ref, gate_ref, wb_ref, wo_ref,
                    wfi_ref, wfo_ref, hist_ref):
    tm = x_ref.shape[0]
    pos = (pl.program_id(0) * tm) % SEQ

    @pl.when(pl.program_id(0) == 0)
    def _():
        hist_ref[...] = jnp.zeros(hist_ref.shape, F32)

    gpre = vec_ref[0:1, :]
    lng, lnb, psc = (vec_ref[r:r + 1, :A_WIDTH] for r in (1, 2, 3))
    h = _rms(x_ref[...], gpre).astype(BF16)

    def proj(lo, width):
        return jnp.dot(h, w_ref[:, lo:lo + width].astype(BF16), preferred_element_type=F32)

    def gate(j):
        zg = proj(COL_G + j * D_MODEL, D_MODEL)
        gate_ref[:, j * D_MODEL:(j + 1) * D_MODEL] = _sigmoid(zg)


    zc = proj(COL_C, C_WIDTH)
    gate(0)

    history = jnp.where(pos == 0, 0.0, hist_ref[...])
    p_all = jnp.concatenate([history, zc], axis=0)
    hist_ref[...] = zc[tm - HALO:, :]
    s2 = p_all + pltpu.roll(p_all, 1, 0)
    rest = lambda a: a[:, C_GROUP_DIM:]
    s4 = rest(s2) + pltpu.roll(rest(s2), 2, 0)
    s8 = rest(s4) + pltpu.roll(rest(s4), 4, 0)
    s16 = rest(s8) + pltpu.roll(rest(s8), 8, 0)
    sums = [a[HALO:, :C_GROUP_DIM] for a in (s2, s4, s8, s16)]
    head_pos = 1 + lax.broadcasted_iota(jnp.int32, (HALO, C_GROUP_DIM), 0)
    pooled = []
    for g, w in enumerate(C_WINDOWS):
        cs = slice(g * C_GROUP_DIM, (g + 1) * C_GROUP_DIM)
        head_cnt = jnp.where(pos == 0, jnp.minimum(head_pos, w), w).astype(F32)
        mean = jnp.concatenate([sums[g][:HALO] * (1.0 / head_cnt), sums[g][HALO:] * (1.0 / w)],
                               axis=0)
        pooled.append((mean - zc[:, cs]).astype(BF16))
    zero_w = jnp.zeros((C_GROUP_DIM, C_GROUP_DIM), BF16)
    for g in range(0, len(C_WINDOWS), 2):
        cs = slice(g * C_GROUP_DIM, (g + 2) * C_GROUP_DIM)
        w_pair = jnp.concatenate([jnp.concatenate([pw_ref[g], zero_w], axis=1),
                                  jnp.concatenate([zero_w, pw_ref[g + 1]], axis=1)], axis=0)
        yc = jnp.dot(jnp.concatenate(pooled[g:g + 2], axis=1), w_pair,
                     preferred_element_type=F32) * psc[:, cs]
        yc_ref[:, cs] = yc.astype(BF16)

    za = proj(COL_A, 2 * A_WIDTH)
    gate(1)
    ga = 0.5 * za * (1.0 + lax.erf(za * math.sqrt(0.5)))
    u = ga[:, :A_WIDTH]
    vv = ga[:, A_WIDTH:]
    mu = jnp.mean(vv, axis=-1, keepdims=True)
    dv = vv - mu
    var = jnp.mean(dv * dv, axis=-1, keepdims=True)
    vn = (dv * lax.rsqrt(var + EPS) * lng + lnb).astype(BF16)
    trow = lax.broadcasted_iota(jnp.int32, (CHUNK, CHUNK), 0)
    tcol = lax.broadcasted_iota(jnp.int32, (CHUNK, CHUNK), 1)
    chunks = [slice(c * CHUNK, (c + 1) * CHUNK) for c in range(tm // CHUNK)]
    for g in range(A_GROUPS):
        wsg = jnp.where(tcol <= trow, ws_ref[g], 0.0).astype(BF16)
        cs = slice(g * CHUNK, (g + 1) * CHUNK)
        v_side = jnp.concatenate([vn[rs, cs] for rs in chunks], axis=1)
        mixed = jnp.dot(wsg, v_side, preferred_element_type=F32)
        for rs, ls in zip(chunks, chunks):
            ya_ref[rs, cs] = (u[rs, cs] * (mixed[:, ls] + bs_ref[g])).astype(BF16)

    gate(2)

    vt = proj(COL_V, B_WIDTH).T.astype(BF16)
    for j in range(tm // TQ):
        vt_ref[j] = vt[:, j * TQ:(j + 1) * TQ]
    q_ref[...] = (proj(COL_Q, QK_WIDTH) * (B_HEAD_DIM ** -0.5 * math.log2(math.e))).astype(BF16)
    k_ref[...] = proj(COL_K, QK_WIDTH).astype(BF16)

    for src, dst in ((wb32_ref, wb_ref), (wo32_ref, wo_ref), (wfi32_ref, wfi_ref),
                     (wfo32_ref, wfo_ref)):
        dst[...] = src[...].astype(BF16)


BF16_SUBLANES = 16


def _in_proj(x, vecs, w_in, ws, bs, pw, merge_weights, layer):
    n = x.shape[0]
    tm = TM_PROJ
    assert n % SEQ == 0 and SEQ % tm == 0 and tm % CHUNK == 0 and tm % TQ == 0
    steps = n // tm
    row = lambda width: pl.BlockSpec((tm, width), lambda i: (i, 0))
    out_shapes = [
        jax.ShapeDtypeStruct((n, A_WIDTH), BF16),
        jax.ShapeDtypeStruct((n, QK_WIDTH), BF16),
        jax.ShapeDtypeStruct((n, QK_WIDTH), BF16),
        jax.ShapeDtypeStruct((n // TQ, B_WIDTH, TQ), BF16),
        jax.ShapeDtypeStruct((n, C_WIDTH), BF16),
        jax.ShapeDtypeStruct((n, N_BRANCH * D_MODEL), F32),
    ]
    cast_in, cast_out = [], []
    for w in merge_weights:
        rows, cols = w.shape[1:]
        every = 1
        while rows * every % (steps * BF16_SUBLANES):
            every *= 2
        slab = rows * every // steps
        cast_in.append(pl.BlockSpec((None, slab, cols), lambda i, e=every: (layer, i // e, 0)))
        cast_out.append(pl.BlockSpec((slab, cols), lambda i, e=every: (i // e, 0)))
        out_shapes.append(jax.ShapeDtypeStruct((rows, cols), BF16))
    return pl.pallas_call(
        _in_proj_kernel,
        grid=(steps,),
        in_specs=[
            row(D_MODEL),
            _layer_spec(layer, (4, D_MODEL)),
            _layer_spec(layer, (D_MODEL, IN_TOTAL)),
            _layer_spec(layer, (A_GROUPS, CHUNK, CHUNK)),
            _layer_spec(layer, (A_GROUPS, CHUNK, CHUNK)),
            _layer_spec(layer, (len(C_WINDOWS), C_GROUP_DIM, C_GROUP_DIM)),
        ] + cast_in,
        out_specs=[row(A_WIDTH), row(QK_WIDTH), row(QK_WIDTH),
                   pl.BlockSpec((tm // TQ, B_WIDTH, TQ), lambda i: (i, 0, 0)), row(C_WIDTH),
                   row(N_BRANCH * D_MODEL)] + cast_out,
        out_shape=out_shapes,
        scratch_shapes=[pltpu.VMEM((HALO, C_WIDTH), F32)],
        compiler_params=pltpu.CompilerParams(
            dimension_semantics=("arbitrary",), vmem_limit_bytes=VMEM_LIMIT),
        name=f"in_proj_l{layer}",
    )(x, vecs, w_in, ws, bs, pw, *merge_weights)


ONES_ROWS = 16
N_STREAMS = 2 * B_HEADS


def _attn_kernel(q_ref, qn_ref, k_ref, k0n_ref, vt_ref, lam_ref, sgb_ref,
                 o_ref, sa_scr, sb_scr, ma_scr, mb_scr, m_scr, acc_scr, *, lambda_init):
    b = pl.program_id(0)
    tq = qn_ref.shape[0]
    heads = [slice(h * B_VDIM, (h + 1) * B_VDIM) for h in range(B_HEADS)]
    lane = lax.broadcasted_iota(jnp.int32, (tq, B_VDIM), 1)
    zero = jnp.zeros((tq, B_VDIM), BF16)
    nt = (((1,), (1,)), ((), ()))

    def split_components(qr):
        out = []
        for hs in heads:
            qh = qr[:, hs]
            out += [jnp.where(lane < B_HEAD_DIM, qh, zero), jnp.where(lane >= B_HEAD_DIM, qh, zero)]
        return out

    def scores(qs, kb_of, s_scr, mx_scr):
        for h, hs in enumerate(heads):
            kb = kb_of(hs)
            for c in range(2):
                st = 2 * h + c
                s = lax.dot_general(kb, qs[st], nt, preferred_element_type=F32)
                s_scr[st] = s
                mx_scr[st] = jnp.max(s, axis=0, keepdims=True)

    def rescaled(st, m_new, first):
        if first:
            return None
        return jnp.exp2(m_scr[st] - m_new) * acc_scr[st]

    def plus(a, b):
        return b if a is None else a + b

    ones_rows = jnp.ones((ONES_ROWS, tq), BF16)

    def consume(j, s_scr, mx_scr, first):
        for h, hs in enumerate(heads):
            vtb = jnp.concatenate([vt_ref[j, hs, :], ones_rows], axis=0)
            for c in range(2):
                st = 2 * h + c
                mx = mx_scr[st]
                m_new = mx if first else jnp.maximum(m_scr[st], mx)
                p = jnp.exp2(s_scr[st] - m_new)
                acc_scr[st] = plus(rescaled(st, m_new, first),
                                   jnp.dot(vtb, p.astype(BF16), preferred_element_type=F32))
                m_scr[st] = m_new

    half = tq // 2
    causal_edge = lambda shape: (lax.broadcasted_iota(jnp.int32, shape, 0)
                                 <= lax.broadcasted_iota(jnp.int32, shape, 1))
    vis_top = causal_edge((half, tq))
    vis_bot = causal_edge((half, half))

    future = jnp.full((half, half), -jnp.inf, F32)
    ones_half = jnp.ones((ONES_ROWS, half), BF16)

    def diag_max(s_top, s_bot):
        s_bot_wide = jnp.concatenate([future, s_bot], axis=1)
        return jnp.maximum(jnp.max(s_top, axis=0, keepdims=True),
                           jnp.max(s_bot_wide, axis=0, keepdims=True))

    def scores_diag(n, qs, s_scr, mx_scr):
        top, bot = n * tq, n * tq + half
        for h, hs in enumerate(heads):
            k_top, k_bot = k_ref[top:top + half, hs], k_ref[bot:bot + half, hs]
            for c in range(2):
                st = 2 * h + c
                s_top = lax.dot_general(k_top, qs[st], nt, preferred_element_type=F32)
                s_bot = lax.dot_general(k_bot, qs[st][half:], nt, preferred_element_type=F32)
                s_top = jnp.where(vis_top, s_top, -jnp.inf)
                s_bot = jnp.where(vis_bot, s_bot, -jnp.inf)
                s_scr[st, 0:half, :] = s_top
                s_scr[st, half:, half:] = s_bot
                mx_scr[st] = diag_max(s_top, s_bot)

    def consume_diag(n, s_scr, mx_scr, first):
        for h, hs in enumerate(heads):
            vt_top = jnp.concatenate([vt_ref[n, hs, 0:half], ones_half], axis=0)
            vt_bot = jnp.concatenate([vt_ref[n, hs, half:], ones_half], axis=0)
            for c in range(2):
                st = 2 * h + c
                s_top, s_bot = s_scr[st, 0:half, :], s_scr[st, half:, half:]
                if first:
                    s_top = jnp.where(vis_top, s_top, -jnp.inf)
                    s_bot = jnp.where(vis_bot, s_bot, -jnp.inf)
                    m_new = diag_max(s_top, s_bot)
                else:
                    m_new = jnp.maximum(m_scr[st], mx_scr[st])
                a = rescaled(st, m_new, first)
                m_scr[st] = m_new
                p_top = jnp.exp2(s_top - m_new).astype(BF16)
                p_bot = jnp.exp2(s_bot - m_scr[st, :, half:]).astype(BF16)
                a = plus(a, jnp.dot(vt_top, p_top, preferred_element_type=F32))
                a_right = a[:, half:] + jnp.dot(vt_bot, p_bot, preferred_element_type=F32)
                acc_scr[st] = jnp.concatenate([a[:, :half], a_right], axis=1)
            finish_head(n, h, hs)

    lam = (jnp.exp(jnp.sum(lam_ref[0:1, :] * lam_ref[1:2, :], keepdims=True))
           - jnp.exp(jnp.sum(lam_ref[2:3, :] * lam_ref[3:4, :], keepdims=True)) + lambda_init)

    def finish_head(n, h, hs):
        a0, a1 = acc_scr[2 * h], acc_scr[2 * h + 1]
        l0, l1 = a0[B_VDIM:B_VDIM + 1], a1[B_VDIM:B_VDIM + 1]
        ot = a0[:B_VDIM] * (1.0 / l0) - a1[:B_VDIM] * (lam / l1)
        ms = jnp.mean(ot * ot, axis=0, keepdims=True)
        y = ot * lax.rsqrt(ms + EPS) * sgb_ref[...] * (1.0 - lambda_init)
        o_ref[n * tq:(n + 1) * tq, hs] = y.T.astype(BF16)

    query_tile = lambda n: split_components(q_ref[n * tq:(n + 1) * tq, :])
    key_tile = lambda j: (lambda hs: k_ref[j * tq:(j + 1) * tq, hs])
    stage = ((sa_scr, ma_scr), (sb_scr, mb_scr))
    nq = k_ref.shape[0] // tq

    @pl.when(b == 0)
    def _():
        scores(query_tile(0), key_tile(0), *stage[0])

    for n in range(nq):
        qs = query_tile(n)
        for j in range(n):
            if j + 1 < n:
                scores(qs, key_tile(j + 1), *stage[(j + 1) % 2])
            else:
                scores_diag(n, qs, *stage[n % 2])
            consume(j, *stage[j % 2], first=(j == 0))
        consume_diag(n, *stage[n % 2], first=(n == 0))
        if n + 1 < nq:
            scores(query_tile(n + 1), key_tile(0), *stage[0])
        else:
            scores(split_components(qn_ref[...]), lambda hs: k0n_ref[:, hs], *stage[0])


def _attention(q, k, vt, lam_vecs, sgb, lambda_init, layer):
    n = q.shape[0]
    assert n % SEQ == 0 and SEQ % TQ == 0 and TQ % (2 * BF16_SUBLANES) == 0
    bsz = n // SEQ
    nq = SEQ // TQ
    next_first = lambda b: (jnp.minimum(b + 1, bsz - 1) * nq, 0)
    stream = lambda *shape: pltpu.VMEM((N_STREAMS,) + shape, F32)
    return pl.pallas_call(
        functools.partial(_attn_kernel, lambda_init=lambda_init),
        grid=(bsz,),
        in_specs=[
            pl.BlockSpec((SEQ, B_WIDTH), lambda b: (b, 0)),
            pl.BlockSpec((TQ, B_WIDTH), next_first),
            pl.BlockSpec((SEQ, B_WIDTH), lambda b: (b, 0)),
            pl.BlockSpec((TQ, B_WIDTH), next_first),
            pl.BlockSpec((nq, B_WIDTH, TQ), lambda b: (b, 0, 0)),
            _layer_spec(layer, (4, B_HEAD_DIM)),
            _layer_spec(layer, (B_VDIM, TQ)),
        ],
        out_specs=pl.BlockSpec((SEQ, B_WIDTH), lambda b: (b, 0)),
        out_shape=jax.ShapeDtypeStruct((n, B_WIDTH), BF16),
        scratch_shapes=[
            stream(TQ, TQ), stream(TQ, TQ),
            stream(1, TQ), stream(1, TQ),
            stream(1, TQ),
            stream(B_VDIM + ONES_ROWS, TQ),
        ],
        compiler_params=pltpu.CompilerParams(
            dimension_semantics=("arbitrary",), vmem_limit_bytes=VMEM_LIMIT),
        name=f"attn_l{layer}",
    )(q, q, k, k, vt, lam_vecs, sgb)


def _merge_kernel(ya_ref, yb_ref, yc_ref, gate_ref, x_ref, wb_ref, wo_ref, vec_ref, wfi_ref, wfo_ref,
                  o_ref, x1_scr, f_scr):
    i = pl.program_id(0)
    n_tiles = pl.num_programs(0) - 1
    gpost, gfpre, gfpost = (vec_ref[r:r + 1, :] for r in range(3))

    def finish_previous():
        out = x1_scr[...] + _rms(f_scr[...], gfpost)
        o_ref[...] = out
        return out

    def zero_after(v):
        bits = lax.bitcast_convert_type(v, jnp.uint32)
        bits = lax.shift_right_logical(lax.shift_right_logical(bits, jnp.uint32(16)), jnp.uint32(16))
        return lax.bitcast_convert_type(bits, F32)

    def compute_tile(prev_out):
        merged = None
        for j, y_ref in enumerate((ya_ref, yb_ref, yc_ref)):
            up = jnp.dot(y_ref[...], wb_ref[j * BRANCH_WIDTH:(j + 1) * BRANCH_WIDTH, :],
                         preferred_element_type=F32)
            term = gate_ref[:, j * D_MODEL:(j + 1) * D_MODEL] * up
            merged = term if merged is None else merged + term
        mix = jnp.dot(merged.astype(BF16), wo_ref[...], preferred_element_type=F32)
        x1 = x_ref[...] + _rms(mix, gpost)

        h = (_rms(x1, gfpre) + zero_after(prev_out)).astype(BF16)
        f = None
        for lo in range(0, D_FF, FF_CHUNK):
            width = min(FF_CHUNK, D_FF - lo)
            g = jnp.dot(h, wfi_ref[:, lo:lo + width], preferred_element_type=F32)
            u = jnp.dot(h, wfi_ref[:, D_FF + lo:D_FF + lo + width], preferred_element_type=F32)
            a = (g * _sigmoid(g) * u).astype(BF16)
            part = jnp.dot(a, wfo_ref[lo:lo + width, :], preferred_element_type=F32)
            f = part if f is None else f + part
        x1_scr[...] = x1
        f_scr[...] = f

    @pl.when(i == 0)
    def _():
        x1_scr[...] = jnp.zeros(x1_scr.shape, F32)
        f_scr[...] = jnp.zeros(f_scr.shape, F32)

    @pl.when(i < n_tiles)
    def _():
        compute_tile(finish_previous())

    @pl.when(i == n_tiles)
    def _():
        finish_previous()


def _merge(ya, yb, yc, gates, x, wb, wo, vecs, wfi, wfo, layer):
    n = x.shape[0]
    tm = TM_MERGE
    assert n % tm == 0
    n_tiles = n // tm
    row = lambda width: pl.BlockSpec((tm, width), lambda i: (jnp.minimum(i, n_tiles - 1), 0))
    return pl.pallas_call(
        _merge_kernel,
        grid=(n_tiles + 1,),
        in_specs=[
            row(BRANCH_WIDTH), row(BRANCH_WIDTH), row(BRANCH_WIDTH),
            row(N_BRANCH * D_MODEL), row(D_MODEL),
            _resident_spec((N_BRANCH * BRANCH_WIDTH, D_MODEL)),
            _resident_spec((D_MODEL, D_MODEL)),
            _layer_spec(layer, (3, D_MODEL)),
            _resident_spec((D_MODEL, 2 * D_FF)),
            _resident_spec((D_FF, D_MODEL)),
        ],
        out_specs=pl.BlockSpec((tm, D_MODEL), lambda i: (jnp.maximum(i - 1, 0), 0)),
        out_shape=jax.ShapeDtypeStruct((n, D_MODEL), F32),
        scratch_shapes=[pltpu.VMEM((tm, D_MODEL), F32), pltpu.VMEM((tm, D_MODEL), F32)],
        compiler_params=pltpu.CompilerParams(
            dimension_semantics=("arbitrary",), vmem_limit_bytes=VMEM_LIMIT),
        name=f"merge_l{layer}",
    )(ya, yb, yc, gates, x, wb, wo, vecs, wfi, wfo)


def kernel(x, norm_mix_pre, w_in, gmlp_norm_g, gmlp_norm_b, gmlp_w_s, gmlp_b_s, lambda_q1, lambda_k1,
           lambda_q2, lambda_k2, diff_subln_g, pool_w, pool_scale, w_branch, w_out, norm_mix_post,
           norm_ffn_pre, w_ffn_in, w_ffn_out, norm_ffn_post):
    bsz, s, d = x.shape
    assert (s, d) == (SEQ, D_MODEL)
    depth = w_in.shape[0]
    xf = x.reshape(bsz * s, d)
    pad = lambda a: jnp.pad(a, ((0, 0), (0, D_MODEL - a.shape[1])))
    proj_vecs = jnp.stack([norm_mix_pre, pad(gmlp_norm_g), pad(gmlp_norm_b), pad(pool_scale)], axis=1)
    lam_vecs = jnp.stack([lambda_q1, lambda_k1, lambda_q2, lambda_k2], axis=1)
    merge_vecs = jnp.stack([norm_mix_post, norm_ffn_pre, norm_ffn_post], axis=1)
    pool_w = pool_w.astype(BF16)
    merge_weights = (w_branch.reshape(depth, N_BRANCH * BRANCH_WIDTH, D_MODEL), w_out, w_ffn_in,
                     w_ffn_out)
    bs = jnp.broadcast_to(gmlp_b_s[:, :, :, None], (depth, A_GROUPS, CHUNK, CHUNK))
    sgb = jnp.broadcast_to(diff_subln_g[:, :, None], (depth, B_VDIM, TQ))
    for l in range(depth):
        lambda_init = 0.8 - 0.6 * math.exp(-0.3 * l)
        ya, q, k, vt, yc, gates, wb, wo, wfi, wfo = _in_proj(
            xf, proj_vecs, w_in, gmlp_w_s, bs, pool_w, merge_weights, l)
        yb = _attention(q, k, vt, lam_vecs, sgb, lambda_init, l)
        xf = _merge(ya, yb, yc, gates, xf, wb, wo, merge_vecs, wfi, wfo, l)
    return xf.reshape(bsz, s, d)
```

```python
import functools
import math

import jax
import jax.numpy as jnp
from jax import lax
from jax.experimental import pallas as pl
from jax.experimental.pallas import tpu as pltpu

F32 = jnp.float32
BF16 = jnp.bfloat16

D_MODEL = 1024
SEQ = 2048
CHUNK = 128
A_GROUPS = 4
A_WIDTH = 512
B_HEADS = 4
B_HEAD_DIM = 64
B_VDIM = 128
B_WIDTH = 512
QK_WIDTH = B_HEADS * 2 * B_HEAD_DIM
C_WINDOWS = (2, 4, 8, 16)
C_GROUP_DIM = 128
C_WIDTH = 512
N_BRANCH = 3
BRANCH_WIDTH = 512
D_FF = 2816
EPS = 1e-6

COL_A = 0
COL_Q = 2 * A_WIDTH
COL_K = COL_Q + QK_WIDTH
COL_V = COL_K + QK_WIDTH
COL_C = COL_V + B_WIDTH
COL_G = COL_C + C_WIDTH
IN_TOTAL = COL_G + N_BRANCH * D_MODEL

HALO = 16
VMEM_LIMIT = 58 * 1024 * 1024

TM_PROJ = 512
TM_MERGE = 512
TQ = 512
FF_CHUNK = 512


def _rms(x, g):
    ms = jnp.mean(x * x, axis=-1, keepdims=True)
    return x * lax.rsqrt(ms + EPS) * g


def _sigmoid(x):
    return 0.5 * (jnp.tanh(0.5 * x) + 1.0)


def _resident_spec(shape):
    nd = len(shape)
    return pl.BlockSpec(tuple(shape), lambda *_: (0,) * nd, pipeline_mode=pl.Buffered(1))


def _layer_spec(layer, shape):
    nd = len(shape)
    return pl.BlockSpec((None,) + tuple(shape), lambda *_: (layer,) + (0,) * nd,
                        pipeline_mode=pl.Buffered(1))


def _in_proj_kernel(x_ref, vec_ref, w_ref, ws_ref, bs_ref, pw_ref, wb32_ref, wo32_ref, wfi32_ref,
                    wfo32_ref, ya_ref, q_ref, k_ref, vt_ref, yc_ref, gate_ref, wb_ref, wo_ref,
                    wfi_ref, wfo_ref, hist_ref):
    tm = x_ref.shape[0]
    pos = (pl.program_id(0) * tm) % SEQ

    @pl.when(pl.program_id(0) == 0)
    def _():
        hist_ref[...] = jnp.zeros(hist_ref.shape, F32)

    gpre = vec_ref[0:1, :]
    lng, lnb, psc = (vec_ref[r:r + 1, :A_WIDTH] for r in (1, 2, 3))
    h = _rms(x_ref[...], gpre).astype(BF16)

    def proj(lo, width):
        return jnp.dot(h, w_ref[:, lo:lo + width].astype(BF16), preferred_element_type=F32)

    def gate(j):
        zg = proj(COL_G + j * D_MODEL, D_MODEL)
        gate_ref[:, j * D_MODEL:(j + 1) * D_MODEL] = _sigmoid(zg)


    zc = proj(COL_C, C_WIDTH)
    gate(0)

    history = jnp.where(pos == 0, 0.0, hist_ref[...])
    p_all = jnp.concatenate([history, zc], axis=0)
    hist_ref[...] = zc[tm - HALO:, :]
    s2 = p_all + pltpu.roll(p_all, 1, 0)
    rest = lambda a: a[:, C_GROUP_DIM:]
    s4 = rest(s2) + pltpu.roll(rest(s2), 2, 0)
    s8 = rest(s4) + pltpu.roll(rest(s4), 4, 0)
    s16 = rest(s8) + pltpu.roll(rest(s8), 8, 0)
    sums = [a[HALO:, :C_GROUP_DIM] for a in (s2, s4, s8, s16)]
    head_pos = 1 + lax.broadcasted_iota(jnp.int32, (HALO, C_GROUP_DIM), 0)
    pooled = []
    for g, w in enumerate(C_WINDOWS):
        cs = slice(g * C_GROUP_DIM, (g + 1) * C_GROUP_DIM)
        head_cnt = jnp.where(pos == 0, jnp.minimum(head_pos, w), w).astype(F32)
        mean = jnp.concatenate([sums[g][:HALO] * (1.0 / head_cnt), sums[g][HALO:] * (1.0 / w)],
                               axis=0)
        pooled.append((mean - zc[:, cs]).astype(BF16))
    zero_w = jnp.zeros((C_GROUP_DIM, C_GROUP_DIM), BF16)
    for g in range(0, len(C_WINDOWS), 2):
        cs = slice(g * C_GROUP_DIM, (g + 2) * C_GROUP_DIM)
        w_pair = jnp.concatenate([jnp.concatenate([pw_ref[g], zero_w], axis=1),
                                  jnp.concatenate([zero_w, pw_ref[g + 1]], axis=1)], axis=0)
        yc = jnp.dot(jnp.concatenate(pooled[g:g + 2], axis=1), w_pair,
                     preferred_element_type=F32) * psc[:, cs]
        yc_ref[:, cs] = yc.astype(BF16)

    za = proj(COL_A, 2 * A_WIDTH)
    gate(1)
    ga = 0.5 * za * (1.0 + lax.erf(za * math.sqrt(0.5)))
    u = ga[:, :A_WIDTH]
    vv = ga[:, A_WIDTH:]
    mu = jnp.mean(vv, axis=-1, keepdims=True)
    dv = vv - mu
    var = jnp.mean(dv * dv, axis=-1, keepdims=True)
    vn = (dv * lax.rsqrt(var + EPS) * lng + lnb).astype(BF16)
    trow = lax.broadcasted_iota(jnp.int32, (CHUNK, CHUNK), 0)
    tcol = lax.broadcasted_iota(jnp.int32, (CHUNK, CHUNK), 1)
    chunks = [slice(c * CHUNK, (c + 1) * CHUNK) for c in range(tm // CHUNK)]
    for g in range(A_GROUPS):
        wsg = jnp.where(tcol <= trow, ws_ref[g], 0.0).astype(BF16)
        cs = slice(g * CHUNK, (g + 1) * CHUNK)
        v_side = jnp.concatenate([vn[rs, cs] for rs in chunks], axis=1)
        mixed = jnp.dot(wsg, v_side, preferred_element_type=F32)
        for rs, ls in zip(chunks, chunks):
            ya_ref[rs, cs] = (u[rs, cs] * (mixed[:, ls] + bs_ref[g])).astype(BF16)

    gate(2)

    vt = proj(COL_V, B_WIDTH).T.astype(BF16)
    for j in range(tm // TQ):
        vt_ref[j] = vt[:, j * TQ:(j + 1) * TQ]
    q_ref[...] = (proj(COL_Q, QK_WIDTH) * (B_HEAD_DIM ** -0.5 * math.log2(math.e))).astype(BF16)
    k_ref[...] = proj(COL_K, QK_WIDTH).astype(BF16)

    for src, dst in ((wb32_ref, wb_ref), (wo32_ref, wo_ref), (wfi32_ref, wfi_ref),
                     (wfo32_ref, wfo_ref)):
        dst[...] = src[...].astype(BF16)


BF16_SUBLANES = 16


def _in_proj(x, vecs, w_in, ws, bs, pw, merge_weights, layer):
    n = x.shape[0]
    tm = TM_PROJ
    assert n % SEQ == 0 and SEQ % tm == 0 and tm % CHUNK == 0 and tm % TQ == 0
    steps = n // tm
    row = lambda width: pl.BlockSpec((tm, width), lambda i: (i, 0))
    out_shapes = [
        jax.ShapeDtypeStruct((n, A_WIDTH), BF16),
        jax.ShapeDtypeStruct((n, QK_WIDTH), BF16),
        jax.ShapeDtypeStruct((n, QK_WIDTH), BF16),
        jax.ShapeDtypeStruct((n // TQ, B_WIDTH, TQ), BF16),
        jax.ShapeDtypeStruct((n, C_WIDTH), BF16),
        jax.ShapeDtypeStruct((n, N_BRANCH * D_MODEL), F32),
    ]
    cast_in, cast_out = [], []
    for w in merge_weights:
        rows, cols = w.shape[1:]
        every = 1
        while rows * every % (steps * BF16_SUBLANES):
            every *= 2
        slab = rows * every // steps
        cast_in.append(pl.BlockSpec((None, slab, cols), lambda i, e=every: (layer, i // e, 0)))
        cast_out.append(pl.BlockSpec((slab, cols), lambda i, e=every: (i // e, 0)))
        out_shapes.append(jax.ShapeDtypeStruct((rows, cols), BF16))
    return pl.pallas_call(
        _in_proj_kernel,
        grid=(steps,),
        in_specs=[
            row(D_MODEL),
            _layer_spec(layer, (4, D_MODEL)),
            _layer_spec(layer, (D_MODEL, IN_TOTAL)),
            _layer_spec(layer, (A_GROUPS, CHUNK, CHUNK)),
            _layer_spec(layer, (A_GROUPS, CHUNK, CHUNK)),
            _layer_spec(layer, (len(C_WINDOWS), C_GROUP_DIM, C_GROUP_DIM)),
        ] + cast_in,
        out_specs=[row(A_WIDTH), row(QK_WIDTH), row(QK_WIDTH),
                   pl.BlockSpec((tm // TQ, B_WIDTH, TQ), lambda i: (i, 0, 0)), row(C_WIDTH),
                   row(N_BRANCH * D_MODEL)] + cast_out,
        out_shape=out_shapes,
        scratch_shapes=[pltpu.VMEM((HALO, C_WIDTH), F32)],
        compiler_params=pltpu.CompilerParams(
            dimension_semantics=("arbitrary",), vmem_limit_bytes=VMEM_LIMIT),
        name=f"in_proj_l{layer}",
    )(x, vecs, w_in, ws, bs, pw, *merge_weights)


ONES_ROWS = 16
N_STREAMS = 2 * B_HEADS


def _attn_kernel(q_ref, qn_ref, k_ref, k0n_ref, vt_ref, lam_ref, sgb_ref,
                 o_ref, sa_scr, sb_scr, ma_scr, mb_scr, m_scr, acc_scr, *, lambda_init):
    b = pl.program_id(0)
    tq = qn_ref.shape[0]
    heads = [slice(h * B_VDIM, (h + 1) * B_VDIM) for h in range(B_HEADS)]
    lane = lax.broadcasted_iota(jnp.int32, (tq, B_VDIM), 1)
    zero = jnp.zeros((tq, B_VDIM), BF16)
    nt = (((1,), (1,)), ((), ()))

    def split_components(qr):
        out = []
        for hs in heads:
            qh = qr[:, hs]
            out += [jnp.where(lane < B_HEAD_DIM, qh, zero), jnp.where(lane >= B_HEAD_DIM, qh, zero)]
        return out

    def scores(qs, kb_of, s_scr, mx_scr):
        for h, hs in enumerate(heads):
            kb = kb_of(hs)
            for c in range(2):
                st = 2 * h + c
                s = lax.dot_general(kb, qs[st], nt, preferred_element_type=F32)
                s_scr[st] = s
                mx_scr[st] = jnp.max(s, axis=0, keepdims=True)

    def rescaled(st, m_new, first):
        if first:
            return None
        return jnp.exp2(m_scr[st] - m_new) * acc_scr[st]

    def plus(a, b):
        return b if a is None else a + b

    ones_rows = jnp.ones((ONES_ROWS, tq), BF16)

    def consume(j, s_scr, mx_scr, first):
        for h, hs in enumerate(heads):
            vtb = jnp.concatenate([vt_ref[j, hs, :], ones_rows], axis=0)
            for c in range(2):
                st = 2 * h + c
                mx = mx_scr[st]
                m_new = mx if first else jnp.maximum(m_scr[st], mx)
                p = jnp.exp2(s_scr[st] - m_new)
                acc_scr[st] = plus(rescaled(st, m_new, first),
                                   jnp.dot(vtb, p.astype(BF16), preferred_element_type=F32))
                m_scr[st] = m_new

    half = tq // 2
    causal_edge = lambda shape: (lax.broadcasted_iota(jnp.int32, shape, 0)
                                 <= lax.broadcasted_iota(jnp.int32, shape, 1))
    vis_top = causal_edge((half, tq))
    vis_bot = causal_edge((half, half))

    future = jnp.full((half, half), -jnp.inf, F32)
    ones_half = jnp.ones((ONES_ROWS, half), BF16)

    def diag_max(s_top, s_bot):
        s_bot_wide = jnp.concatenate([future, s_bot], axis=1)
        return jnp.maximum(jnp.max(s_top, axis=0, keepdims=True),
                           jnp.max(s_bot_wide, axis=0, keepdims=True))

    def scores_diag(n, qs, s_scr, mx_scr):
        top, bot = n * tq, n * tq + half
        for h, hs in enumerate(heads):
            k_top, k_bot = k_ref[top:top + half, hs], k_ref[bot:bot + half, hs]
            for c in range(2):
                st = 2 * h + c
                s_top = lax.dot_general(k_top, qs[st], nt, preferred_element_type=F32)
                s_bot = lax.dot_general(k_bot, qs[st][half:], nt, preferred_element_type=F32)
                s_top = jnp.where(vis_top, s_top, -jnp.inf)
                s_bot = jnp.where(vis_bot, s_bot, -jnp.inf)
                s_scr[st, 0:half, :] = s_top
                s_scr[st, half:, half:] = s_bot
                mx_scr[st] = diag_max(s_top, s_bot)

    def consume_diag(n, s_scr, mx_scr, first):
        for h, hs in enumerate(heads):
            vt_top = jnp.concatenate([vt_ref[n, hs, 0:half], ones_half], axis=0)
            vt_bot = jnp.concatenate([vt_ref[n, hs, half:], ones_half], axis=0)
            for c in range(2):
                st = 2 * h + c
                s_top, s_bot = s_scr[st, 0:half, :], s_scr[st, half:, half:]
                if first:
                    s_top = jnp.where(vis_top, s_top, -jnp.inf)
                    s_bot = jnp.where(vis_bot, s_bot, -jnp.inf)
                    m_new = diag_max(s_top, s_bot)
                else:
                    m_new = jnp.maximum(m_scr[st], mx_scr[st])
                a = rescaled(st, m_new, first)
                m_scr[st] = m_new
                p_top = jnp.exp2(s_top - m_new).astype(BF16)
                p_bot = jnp.exp2(s_bot - m_scr[st, :, half:]).astype(BF16)
                a = plus(a, jnp.dot(vt_top, p_top, preferred_element_type=F32))
                a_right = a[:, half:] + jnp.dot(vt_bot, p_bot, preferred_element_type=F32)
                acc_scr[st] = jnp.concatenate([a[:, :half], a_right], axis=1)
            finish_head(n, h, hs)

    lam = (jnp.exp(jnp.sum(lam_ref[0:1, :] * lam_ref[1:2, :], keepdims=True))
           - jnp.exp(jnp.sum(lam_ref[2:3, :] * lam_ref[3:4, :], keepdims=True)) + lambda_init)

    def finish_head(n, h, hs):
        a0, a1 = acc_scr[2 * h], acc_scr[2 * h + 1]
        l0, l1 = a0[B_VDIM:B_VDIM + 1], a1[B_VDIM:B_VDIM + 1]
        ot = a0[:B_VDIM] * (1.0 / l0) - a1[:B_VDIM] * (lam / l1)
        ms = jnp.mean(ot * ot, axis=0, keepdims=True)
        y = ot * lax.rsqrt(ms + EPS) * sgb_ref[...] * (1.0 - lambda_init)
        o_ref[n * tq:(n + 1) * tq, hs] = y.T.astype(BF16)

    query_tile = lambda n: split_components(q_ref[n * tq:(n + 1) * tq, :])
    key_tile = lambda j: (lambda hs: k_ref[j * tq:(j + 1) * tq, hs])
    stage = ((sa_scr, ma_scr), (sb_scr, mb_scr))
    nq = k_ref.shape[0] // tq

    @pl.when(b == 0)
    def _():
        scores(query_tile(0), key_tile(0), *stage[0])

    for n in range(nq):
        qs = query_tile(n)
        for j in range(n):
            if j + 1 < n:
                scores(qs, key_tile(j + 1), *stage[(j + 1) % 2])
            else:
                scores_diag(n, qs, *stage[n % 2])
            consume(j, *stage[j % 2], first=(j == 0))
        consume_diag(n, *stage[n % 2], first=(n == 0))
        if n + 1 < nq:
            scores(query_tile(n + 1), key_tile(0), *stage[0])
        else:
            scores(split_components(qn_ref[...]), lambda hs: k0n_ref[:, hs], *stage[0])


def _attention(q, k, vt, lam_vecs, sgb, lambda_init, layer):
    n = q.shape[0]
    assert n % SEQ == 0 and SEQ % TQ == 0 and TQ % (2 * BF16_SUBLANES) == 0
    bsz = n // SEQ
    nq = SEQ // TQ
    next_first = lambda b: (jnp.minimum(b + 1, bsz - 1) * nq, 0)
    stream = lambda *shape: pltpu.VMEM((N_STREAMS,) + shape, F32)
    return pl.pallas_call(
        functools.partial(_attn_kernel, lambda_init=lambda_init),
        grid=(bsz,),
        in_specs=[
            pl.BlockSpec((SEQ, B_WIDTH), lambda b: (b, 0)),
            pl.BlockSpec((TQ, B_WIDTH), next_first),
            pl.BlockSpec((SEQ, B_WIDTH), lambda b: (b, 0)),
            pl.BlockSpec((TQ, B_WIDTH), next_first),
            pl.BlockSpec((nq, B_WIDTH, TQ), lambda b: (b, 0, 0)),
            _layer_spec(layer, (4, B_HEAD_DIM)),
            _layer_spec(layer, (B_VDIM, TQ)),
        ],
        out_specs=pl.BlockSpec((SEQ, B_WIDTH), lambda b: (b, 0)),
        out_shape=jax.ShapeDtypeStruct((n, B_WIDTH), BF16),
        scratch_shapes=[
            stream(TQ, TQ), stream(TQ, TQ),
            stream(1, TQ), stream(1, TQ),
            stream(1, TQ),
            stream(B_VDIM + ONES_ROWS, TQ),
        ],
        compiler_params=pltpu.CompilerParams(
            dimension_semantics=("arbitrary",), vmem_limit_bytes=VMEM_LIMIT),
        name=f"attn_l{layer}",
    )(q, q, k, k, vt, lam_vecs, sgb)


def _merge_kernel(ya_ref, yb_ref, yc_ref, gate_ref, x_ref, wb_ref, wo_ref, vec_ref, wfi_ref, wfo_ref,
                  o_ref):
    gpost, gfpre, gfpost = (vec_ref[r:r + 1, :] for r in range(3))
    tm = x_ref.shape[0]
    halves = (slice(0, tm // 2), slice(tm // 2, tm))
    merged = [None, None]
    for j, y_ref in enumerate((ya_ref, yb_ref, yc_ref)):
        for k, rs in enumerate(halves):
            up = jnp.dot(y_ref[rs, :], wb_ref[j * BRANCH_WIDTH:(j + 1) * BRANCH_WIDTH, :],
                         preferred_element_type=F32)
            term = gate_ref[rs, j * D_MODEL:(j + 1) * D_MODEL] * up
            merged[k] = term if merged[k] is None else merged[k] + term
    x1, h = [], []
    for k, rs in enumerate(halves):
        mix = jnp.dot(merged[k].astype(BF16), wo_ref[...], preferred_element_type=F32)
        x1.append(x_ref[rs, :] + _rms(mix, gpost))
        h.append(_rms(x1[k], gfpre).astype(BF16))
    f = [None, None]
    for lo in range(0, D_FF, FF_CHUNK):
        width = min(FF_CHUNK, D_FF - lo)
        for k in range(2):
            g = jnp.dot(h[k], wfi_ref[:, lo:lo + width], preferred_element_type=F32)
            u = jnp.dot(h[k], wfi_ref[:, D_FF + lo:D_FF + lo + width], preferred_element_type=F32)
            a = (g * _sigmoid(g) * u).astype(BF16)
            part = jnp.dot(a, wfo_ref[lo:lo + width, :], preferred_element_type=F32)
            f[k] = part if f[k] is None else f[k] + part
    for k, rs in enumerate(halves):
        o_ref[rs, :] = x1[k] + _rms(f[k], gfpost)


def _merge(ya, yb, yc, gates, x, wb, wo, vecs, wfi, wfo, layer):
    n = x.shape[0]
    tm = TM_MERGE
    assert n % tm == 0
    row = lambda width: pl.BlockSpec((tm, width), lambda i: (i, 0))
    return pl.pallas_call(
        _merge_kernel,
        grid=(n // tm,),
        in_specs=[
            row(BRANCH_WIDTH), row(BRANCH_WIDTH), row(BRANCH_WIDTH),
            row(N_BRANCH * D_MODEL), row(D_MODEL),
            _resident_spec((N_BRANCH * BRANCH_WIDTH, D_MODEL)),
            _resident_spec((D_MODEL, D_MODEL)),
            _layer_spec(layer, (3, D_MODEL)),
            _resident_spec((D_MODEL, 2 * D_FF)),
            _resident_spec((D_FF, D_MODEL)),
        ],
        out_specs=row(D_MODEL),
        out_shape=jax.ShapeDtypeStruct((n, D_MODEL), F32),
        compiler_params=pltpu.CompilerParams(
            dimension_semantics=("arbitrary",), vmem_limit_bytes=VMEM_LIMIT),
        name=f"merge_l{layer}",
    )(ya, yb, yc, gates, x, wb, wo, vecs, wfi, wfo)


def kernel(x, norm_mix_pre, w_in, gmlp_norm_g, gmlp_norm_b, gmlp_w_s, gmlp_b_s, lambda_q1, lambda_k1,
           lambda_q2, lambda_k2, diff_subln_g, pool_w, pool_scale, w_branch, w_out, norm_mix_post,
           norm_ffn_pre, w_ffn_in, w_ffn_out, norm_ffn_post):
    bsz, s, d = x.shape
    assert (s, d) == (SEQ, D_MODEL)
    depth = w_in.shape[0]
    xf = x.reshape(bsz * s, d)
    pad = lambda a: jnp.pad(a, ((0, 0), (0, D_MODEL - a.shape[1])))
    proj_vecs = jnp.stack([norm_mix_pre, pad(gmlp_norm_g), pad(gmlp_norm_b), pad(pool_scale)], axis=1)
    lam_vecs = jnp.stack([lambda_q1, lambda_k1, lambda_q2, lambda_k2], axis=1)
    merge_vecs = jnp.stack([norm_mix_post, norm_ffn_pre, norm_ffn_post], axis=1)
    pool_w = pool_w.astype(BF16)
    merge_weights = (w_branch.reshape(depth, N_BRANCH * BRANCH_WIDTH, D_MODEL), w_out, w_ffn_in,
                     w_ffn_out)
    bs = jnp.broadcast_to(gmlp_b_s[:, :, :, None], (depth, A_GROUPS, CHUNK, CHUNK))
    sgb = jnp.broadcast_to(diff_subln_g[:, :, None], (depth, B_VDIM, TQ))
    for l in range(depth):
        lambda_init = 0.8 - 0.6 * math.exp(-0.3 * l)
        ya, q, k, vt, yc, gates, wb, wo, wfi, wfo = _in_proj(
            xf, proj_vecs, w_in, gmlp_w_s, bs, pool_w, merge_weights, l)
        yb = _attention(q, k, vt, lam_vecs, sgb, lambda_init, l)
        xf = _merge(ya, yb, yc, gates, xf, wb, wo, merge_vecs, wfi, wfo, l)
    return xf.reshape(bsz, s, d)
```
